```python
import math
import jax
import jax.numpy as jnp
from jax import lax
import numpy as np

D_MODEL = 1024
BATCH = 16
SEQ = 4096
DEPTH = 1
DEC_BATCH = 32
DEC_SEQ = 64
PAST_LEN = 1024

CHUNK = 64
MIX_WIDTH = D_MODEL
DA_HEADS = 4
DA_QK_DIM = 64
DA_V_DIM = 2 * DA_QK_DIM
DA_WIDTH = DA_HEADS * DA_V_DIM
DA_SCALE = DA_QK_DIM ** -0.5
HG_HEADS = 4
HG_EXPAND = 128
HG_WIDTH = MIX_WIDTH - DA_WIDTH
HG_V_DIM = HG_WIDTH // HG_HEADS
HG_K_WIDTH = HG_HEADS * HG_EXPAND
DA_QK_WIDTH = DA_HEADS * 2 * DA_QK_DIM
IN_WIDTH = 2 * DA_QK_WIDTH + DA_WIDTH + 2 * HG_K_WIDTH + 2 * HG_WIDTH
SPLIT_POINTS = (
    DA_QK_WIDTH,
    2 * DA_QK_WIDTH,
    2 * DA_QK_WIDTH + DA_WIDTH,
    2 * DA_QK_WIDTH + DA_WIDTH + HG_K_WIDTH,
    2 * DA_QK_WIDTH + DA_WIDTH + 2 * HG_K_WIDTH,
    2 * DA_QK_WIDTH + DA_WIDTH + 2 * HG_K_WIDTH + HG_WIDTH,
)
Q_BLOCK = 128
N_MEM = 256
MEM_HEADS = 4
MEM_HEAD_DIM = D_MODEL // MEM_HEADS
N_GROUPS = 4
EXPERTS_PER_GROUP = 8
N_EXPERTS = N_GROUPS * EXPERTS_PER_GROUP
TOP_K = 2
EXPERT_FF = 512
EXPERT_BLOCK = 128
EPS = 1e-5

kernel_name = "hybrid_diffattn_hgrn2_hmoe_stream_step"


def rms_norm(x, g):
    x32 = x.astype(jnp.float32)
    y = x32 * lax.rsqrt(jnp.mean(x32 * x32, axis=-1, keepdims=True) + EPS)
    return (y * g.astype(jnp.float32)).astype(x.dtype)


def lambda_init(layer):
    return 0.8 - 0.6 * math.exp(-0.3 * layer)


def alibi_slopes(n_heads):
    return jnp.exp2(-8.0 * jnp.arange(1, n_heads + 1, dtype=jnp.float32) / n_heads)


def _diff_attn_core(q, k, v, q_pos, k_pos, lam):
    s = jnp.einsum("bqhcd,bkhcd->bchqk", q, k, preferred_element_type=jnp.float32) * DA_SCALE
    dist = jnp.abs(q_pos[:, None] - k_pos[None, :]).astype(jnp.float32)
    bias = -alibi_slopes(DA_HEADS)[:, None, None] * dist
    visible = (k_pos[None, :] // CHUNK) <= (q_pos[:, None] // CHUNK)
    s = jnp.where(visible, s + bias, -jnp.inf)
    p = jax.nn.softmax(s, axis=-1)
    w = p[:, 0] - lam * p[:, 1]
    return jnp.einsum("bhqk,bkhd->bqhd", w, v.astype(jnp.float32))


def diff_attention(q, k, v, q_pos, k_pos, lam):
    B, T = q.shape[0], q.shape[1]
    if T > Q_BLOCK and T % Q_BLOCK == 0:
        n = T // Q_BLOCK
        qb = jnp.moveaxis(q.reshape(B, n, Q_BLOCK, *q.shape[2:]), 1, 0)
        pb = q_pos.reshape(n, Q_BLOCK)
        ob = lax.map(lambda a: _diff_attn_core(a[0], k, v, a[1], k_pos, lam), (qb, pb))
        return jnp.moveaxis(ob, 0, 1).reshape(B, T, *ob.shape[3:])
    return _diff_attn_core(q, k, v, q_pos, k_pos, lam)


def hgrn2_chunkwise(q, k, log_f, v, s0, block):
    B, T, H, DK = q.shape
    DV = v.shape[-1]
    n = T // block

    def to_blocks(a):
        return a.reshape(B, n, block, H, a.shape[-1]).transpose(1, 0, 3, 2, 4)

    causal = jnp.tril(jnp.ones((block, block), dtype=bool))[:, :, None]

    def step(s, inp):
        qc, kc, gc, vc = inp
        b = jnp.cumsum(gc, axis=2)
        rel = b[:, :, :, None, :] - b[:, :, None, :, :]
        decay = jnp.exp(jnp.where(causal, rel, -jnp.inf))
        scores = jnp.einsum("bhtd,bhsd,bhtsd->bhts", qc, kc, decay)
        o = (jnp.einsum("bhts,bhsv->bhtv", scores, vc)
             + jnp.einsum("bhtd,bhdv->bhtv", qc * jnp.exp(b), s))
        b_last = b[:, :, -1:, :]
        s_new = (jnp.exp(b_last[:, :, 0, :])[..., None] * s
                 + jnp.einsum("bhsd,bhsv->bhdv", kc * jnp.exp(b_last - b), vc))
        return s_new, o

    s_fin, o = lax.scan(step, s0, (to_blocks(q), to_blocks(k), to_blocks(log_f), to_blocks(v)))
    return o.transpose(1, 0, 3, 2, 4).reshape(B, T, H, DV), s_fin


def grouped_experts(xf, e_idx, gate, w_gate, w_up, w_down):
    N, D = xf.shape
    K = e_idx.shape[1]
    M = N * K
    flat_e = e_idx.reshape(M).astype(jnp.int32)
    flat_tok = jnp.arange(M, dtype=jnp.int32) // K
    flat_w = gate.reshape(M)
    order = jnp.argsort(flat_e)
    se = flat_e[order]
    counts = jnp.bincount(flat_e, length=N_EXPERTS)
    padded = (counts + EXPERT_BLOCK - 1) // EXPERT_BLOCK * EXPERT_BLOCK
    pad_end = jnp.cumsum(padded)
    pad_start = pad_end - padded
    start = jnp.cumsum(counts) - counts
    dest = pad_start[se] + jnp.arange(M, dtype=jnp.int32) - start[se]
    n_blocks = (M + EXPERT_BLOCK - 1) // EXPERT_BLOCK + N_EXPERTS
    P = n_blocks * EXPERT_BLOCK
    row_tok = jnp.full((P,), N, jnp.int32).at[dest].set(flat_tok[order])
    row_w = jnp.zeros((P,), jnp.float32).at[dest].set(flat_w[order])
    blk_start = jnp.arange(n_blocks, dtype=jnp.int32) * EXPERT_BLOCK
    blk_e = jnp.minimum(jnp.searchsorted(pad_end, blk_start, side="right"), N_EXPERTS - 1)
    x_pad = jnp.concatenate([xf, jnp.zeros((1, D), xf.dtype)], axis=0)

    def one_block(args):
        toks, e = args
        xb = x_pad[toks]
        hb = jax.nn.silu(xb @ w_gate[e]) * (xb @ w_up[e])
        return hb @ w_down[e]

    yb = lax.map(one_block, (row_tok.reshape(n_blocks, EXPERT_BLOCK), blk_e))
    yb = yb.reshape(P, D) * row_w[:, None].astype(yb.dtype)
    return jnp.zeros((N + 1, D), yb.dtype).at[row_tok].add(yb)[:N]


def hier_moe(h, wg_r, bg_r, we_r, be_r, w_gate, w_up, w_down):
    B, T, D = h.shape
    xf = h.reshape(B * T, D)
    N = xf.shape[0]
    glog = (xf @ wg_r).astype(jnp.float32) + bg_r.astype(jnp.float32)
    gprob = jax.nn.softmax(glog, axis=-1)
    g_idx = jnp.argmax(glog, axis=-1).astype(jnp.int32)
    g_w = jnp.take_along_axis(gprob, g_idx[:, None], axis=-1)[:, 0]
    elog = ((xf @ we_r).astype(jnp.float32) + be_r.astype(jnp.float32)).reshape(N, N_GROUPS, EXPERTS_PER_GROUP)
    elog_g = jnp.take_along_axis(elog, g_idx[:, None, None], axis=1)[:, 0]
    top_v, top_i = lax.top_k(elog_g, TOP_K)
    gate = jax.nn.softmax(top_v, axis=-1) * g_w[:, None]
    e_idx = g_idx[:, None] * EXPERTS_PER_GROUP + top_i.astype(jnp.int32)
    y = grouped_experts(xf, e_idx, gate, w_gate, w_up, w_down)
    return y.reshape(B, T, D)


def memory_kv(mem, g, wk, wv):
    B = mem.shape[0]
    mn = rms_norm(mem, g)
    k = (mn @ wk).reshape(B, N_MEM, MEM_HEADS, MEM_HEAD_DIM)
    v = (mn @ wv).reshape(B, N_MEM, MEM_HEADS, MEM_HEAD_DIM)
    return k, v


def encoder_layer(x, past_len, past_k, past_v, s0, mem_k, mem_v, lw):
    B, T, _ = x.shape
    h = rms_norm(x, lw["ln_mix"])
    z = h @ lw["w_in"]
    dq, dk, dv, hq, hf, hi, hg = jnp.split(z, SPLIT_POINTS, axis=-1)
    k_rows = dk.reshape(B, T, DA_HEADS, 2 * DA_QK_DIM)
    v_rows = dv.reshape(B, T, DA_HEADS, DA_V_DIM)
    if past_k is None:
        k_all, v_all = k_rows, v_rows
    else:
        k_all = jnp.concatenate([past_k.astype(k_rows.dtype), k_rows], axis=1)
        v_all = jnp.concatenate([past_v.astype(v_rows.dtype), v_rows], axis=1)
    q_pos = past_len + jnp.arange(T, dtype=jnp.int32)
    k_pos = jnp.arange(past_len + T, dtype=jnp.int32)
    lp = lw["da_lambda"].astype(jnp.float32)
    lam = jnp.exp(jnp.sum(lp[0] * lp[1])) - jnp.exp(jnp.sum(lp[2] * lp[3])) + lw["lam_init"]
    o_da = diff_attention(dq.reshape(B, T, DA_HEADS, 2, DA_QK_DIM),
                          k_all.reshape(B, past_len + T, DA_HEADS, 2, DA_QK_DIM),
                          v_all, q_pos, k_pos, lam)
    o_da = rms_norm(o_da, lw["da_subln"]) * (1.0 - lw["lam_init"])
    lb = lw["hg_lb"]
    hf32 = hf.astype(jnp.float32)
    f = lb + (1.0 - lb) * jax.nn.sigmoid(hf32)
    kk = (1.0 - lb) * jax.nn.sigmoid(-hf32)
    log_f = jnp.log(f)
    qh = jax.nn.silu(hq.astype(jnp.float32))
    if s0 is None:
        s0 = jnp.zeros((B, HG_HEADS, HG_EXPAND, HG_V_DIM), jnp.float32)
    hshape = (B, T, HG_HEADS, HG_EXPAND)
    o_hg, s_new = hgrn2_chunkwise(qh.reshape(hshape), kk.reshape(hshape), log_f.reshape(hshape),
                                  hi.astype(jnp.float32).reshape(B, T, HG_HEADS, HG_V_DIM),
                                  s0.astype(jnp.float32), min(CHUNK, T))
    o_hg = rms_norm(o_hg, lw["hg_gnorm"]) * jax.nn.silu(hg.astype(jnp.float32)).reshape(B, T, HG_HEADS, HG_V_DIM)
    mixed = jnp.concatenate([o_da.reshape(B, T, DA_WIDTH), o_hg.reshape(B, T, HG_WIDTH)], axis=-1)
    x = x + mixed.astype(x.dtype) @ lw["w_out"]
    hm = rms_norm(x, lw["ln_mem"])
    qm = (hm @ lw["w_mem_q"]).reshape(B, T, MEM_HEADS, MEM_HEAD_DIM)
    sm = jnp.einsum("bthd,bmhd->bhtm", qm, mem_k.astype(qm.dtype),
                    preferred_element_type=jnp.float32) * (MEM_HEAD_DIM ** -0.5)
    pm = jax.nn.softmax(sm, axis=-1)
    om = jnp.einsum("bhtm,bmhd->bthd", pm, mem_v.astype(jnp.float32)).reshape(B, T, D_MODEL)
    x = x + om.astype(x.dtype) @ lw["w_mem_o"]
    x = x + hier_moe(rms_norm(x, lw["ln_ffn"]), lw["rg_w"], lw["rg_b"], lw["re_w"], lw["re_b"],
                     lw["e_gate"], lw["e_up"], lw["e_down"])
    return x, k_rows, v_rows, s_new.astype(x.dtype)


def setup_inputs(seed: int = 0) -> dict:
    key = jax.random.key(seed)
    ks = jax.random.split(key, 40)

    def nrm(k, shape, scale):
        return scale * jax.random.normal(k, shape, jnp.float32)

    return {
        "x_prompt": nrm(ks[0], (BATCH, SEQ, D_MODEL), 1.0),
        "x_sample": nrm(ks[1], (DEC_BATCH, DEC_SEQ, D_MODEL), 1.0),
        "mem_prompt": nrm(ks[2], (BATCH, N_MEM, D_MODEL), 1.0),
        "cache_diff_k": nrm(ks[3], (DEPTH, DEC_BATCH, PAST_LEN, DA_HEADS, 2 * DA_QK_DIM), 1.0),
        "cache_diff_v": nrm(ks[4], (DEPTH, DEC_BATCH, PAST_LEN, DA_HEADS, DA_V_DIM), 1.0),
        "state_hgrn": nrm(ks[5], (DEPTH, DEC_BATCH, HG_HEADS, HG_EXPAND, HG_V_DIM), 0.3),
        "cache_mem_k": nrm(ks[6], (DEPTH, DEC_BATCH, N_MEM, MEM_HEADS, MEM_HEAD_DIM), 1.0),
        "cache_mem_v": nrm(ks[7], (DEPTH, DEC_BATCH, N_MEM, MEM_HEADS, MEM_HEAD_DIM), 1.0),
        "ln_mix_g": 1.0 + nrm(ks[8], (DEPTH, D_MODEL), 0.02),
        "w_in": nrm(ks[9], (DEPTH, D_MODEL, IN_WIDTH), D_MODEL ** -0.5),
        "da_lambda": nrm(ks[10], (DEPTH, 4, DA_QK_DIM), 0.1),
        "da_subln_g": 1.0 + nrm(ks[11], (DEPTH, DA_V_DIM), 0.02),
        "hg_lb_logits": nrm(ks[12], (DEPTH + 1, HG_K_WIDTH), 0.5),
        "hg_gnorm_g": 1.0 + nrm(ks[13], (DEPTH, HG_V_DIM), 0.02),
        "w_out": nrm(ks[14], (DEPTH, MIX_WIDTH, D_MODEL), MIX_WIDTH ** -0.5),
        "ln_mem_g": 1.0 + nrm(ks[15], (DEPTH, D_MODEL), 0.02),
        "mem_norm_g": 1.0 + nrm(ks[16], (DEPTH, D_MODEL), 0.02),
        "w_mem_q": nrm(ks[17], (DEPTH, D_MODEL, D_MODEL), D_MODEL ** -0.5),
        "w_mem_k": nrm(ks[18], (DEPTH, D_MODEL, D_MODEL), D_MODEL ** -0.5),
        "w_mem_v": nrm(ks[19], (DEPTH, D_MODEL, D_MODEL), D_MODEL ** -0.5),
        "w_mem_o": nrm(ks[20], (DEPTH, D_MODEL, D_MODEL), D_MODEL ** -0.5),
        "ln_ffn_g": 1.0 + nrm(ks[21], (DEPTH, D_MODEL), 0.02),
        "router_group_w": nrm(ks[22], (DEPTH, D_MODEL, N_GROUPS), D_MODEL ** -0.5),
        "router_group_b": nrm(ks[23], (DEPTH, N_GROUPS), 0.01),
        "router_expert_w": nrm(ks[24], (DEPTH, D_MODEL, N_EXPERTS), D_MODEL ** -0.5),
        "router_expert_b": nrm(ks[25], (DEPTH, N_EXPERTS), 0.01),
        "exp_w_gate": nrm(ks[26], (DEPTH, N_EXPERTS, D_MODEL, EXPERT_FF), D_MODEL ** -0.5),
        "exp_w_up": nrm(ks[27], (DEPTH, N_EXPERTS, D_MODEL, EXPERT_FF), D_MODEL ** -0.5),
        "exp_w_down": nrm(ks[28], (DEPTH, N_EXPERTS, EXPERT_FF, D_MODEL), EXPERT_FF ** -0.5),
        "final_norm_g": 1.0 + nrm(ks[29], (D_MODEL,), 0.02),
    }


def reference(x_prompt, x_sample, mem_prompt, cache_diff_k, cache_diff_v, state_hgrn,
              cache_mem_k, cache_mem_v, ln_mix_g, w_in, da_lambda, da_subln_g, hg_lb_logits,
              hg_gnorm_g, w_out, ln_mem_g, mem_norm_g, w_mem_q, w_mem_k, w_mem_v, w_mem_o,
              ln_ffn_g, router_group_w, router_group_b, router_expert_w, router_expert_b,
              exp_w_gate, exp_w_up, exp_w_down, final_norm_g):
    lb_all = jnp.cumsum(jax.nn.softmax(hg_lb_logits.astype(jnp.float32), axis=0), axis=0)
    past_len = cache_diff_k.shape[2]
    yp, ys = x_prompt, x_sample
    kp_l, vp_l, sp_l, mkp_l, mvp_l, ks_l, vs_l, ss_l = [], [], [], [], [], [], [], []
    for l in range(DEPTH):
        lw = {
            "ln_mix": ln_mix_g[l], "w_in": w_in[l], "da_lambda": da_lambda[l],
            "da_subln": da_subln_g[l], "lam_init": lambda_init(l), "hg_lb": lb_all[l],
            "hg_gnorm": hg_gnorm_g[l], "w_out": w_out[l], "ln_mem": ln_mem_g[l],
            "w_mem_q": w_mem_q[l], "w_mem_o": w_mem_o[l], "ln_ffn": ln_ffn_g[l],
            "rg_w": router_group_w[l], "rg_b": router_group_b[l],
            "re_w": router_expert_w[l], "re_b": router_expert_b[l],
            "e_gate": exp_w_gate[l], "e_up": exp_w_up[l], "e_down": exp_w_down[l],
        }
        mk_p, mv_p = memory_kv(mem_prompt, mem_norm_g[l], w_mem_k[l], w_mem_v[l])
        yp, kp, vp, sp = encoder_layer(yp, 0, None, None, None, mk_p, mv_p, lw)
        ys, kss, vss, sss = encoder_layer(ys, past_len, cache_diff_k[l], cache_diff_v[l], state_hgrn[l],
                                          cache_mem_k[l], cache_mem_v[l], lw)
        kp_l.append(kp); vp_l.append(vp); sp_l.append(sp); mkp_l.append(mk_p); mvp_l.append(mv_p)
        ks_l.append(kss); vs_l.append(vss); ss_l.append(sss)
    y_prompt = rms_norm(yp, final_norm_g)
    y_sample = rms_norm(ys, final_norm_g)
    new_diff_k_prompt = jnp.stack(kp_l)
    new_diff_v_prompt = jnp.stack(vp_l)
    new_state_hgrn_prompt = jnp.stack(sp_l)
    new_mem_k_prompt = jnp.stack(mkp_l)
    new_mem_v_prompt = jnp.stack(mvp_l)
    new_diff_k_sample = jnp.stack(ks_l)
    new_diff_v_sample = jnp.stack(vs_l)
    new_state_hgrn_sample = jnp.stack(ss_l)
    return (y_prompt, y_sample, new_diff_k_prompt, new_diff_v_prompt, new_state_hgrn_prompt,
            new_mem_k_prompt, new_mem_v_prompt, new_diff_k_sample, new_diff_v_sample,
            new_state_hgrn_sample)
```

```python
import functools
import math

import jax
import jax.numpy as jnp
from jax import lax
from jax.experimental import pallas as pl
from jax.experimental.pallas import tpu as pltpu

F32 = jnp.float32
BF16 = jnp.bfloat16

D_MODEL = 1024
EPS = 1e-5
CHUNK = 64
N_HEADS = 4
HEAD_W = 128
QK_DIM = 64
GROUP_W = N_HEADS * HEAD_W
N_PROJ = 7
DA_SCALE = QK_DIM ** -0.5
LAM_INIT = 0.8 - 0.6 * math.exp(-0.3 * 0)
N_MEM = 256
MEM_HEADS = 4
MEM_HEAD_DIM = D_MODEL // MEM_HEADS
MEM_SCALE = MEM_HEAD_DIM ** -0.5
N_GROUPS = 4
EXPERTS_PER_GROUP = 8
N_EXPERTS = N_GROUPS * EXPERTS_PER_GROUP
TOP_K = 2
EXPERT_FF = 512
ROUTE_W = 8
SUB = 16
NEG_INF = float("-inf")

VMEM_LIMIT = 48 * 1024 * 1024


def _cparams(sem):
    return pltpu.CompilerParams(dimension_semantics=sem, vmem_limit_bytes=VMEM_LIMIT)


def _rms(x, g):
    return x * lax.rsqrt(jnp.mean(x * x, axis=-1, keepdims=True) + EPS) * g


def _dot(a, b):
    return jnp.dot(a, b, preferred_element_type=F32)


def _dot_nt(a, b):
    return lax.dot_general(a, b, (((1,), (1,)), ((), ())), preferred_element_type=F32)


def _dot_tn(a, b):
    return lax.dot_general(a, b, (((0,), (0,)), ((), ())), preferred_element_type=F32)


def _memkv_kernel(m_ref, g_ref, wk_ref, wv_ref, kf_ref, vf_ref, kb_ref, vb_ref):
    mn = _rms(m_ref[0], g_ref[...]).astype(BF16)
    k = _dot(mn, wk_ref[...])
    v = _dot(mn, wv_ref[...])
    kf_ref[0] = k
    vf_ref[0] = v
    kb_ref[0] = k.astype(BF16)
    vb_ref[0] = v.astype(BF16)


def _memory_kv(mem, g, wk, wv):
    B = mem.shape[0]
    blk = pl.BlockSpec((1, N_MEM, D_MODEL), lambda b: (b, 0, 0))
    wspec = pl.BlockSpec((D_MODEL, D_MODEL), lambda b: (0, 0))
    return pl.pallas_call(
        _memkv_kernel,
        grid=(B,),
        in_specs=[blk, pl.BlockSpec((1, D_MODEL), lambda b: (0, 0)), wspec, wspec],
        out_specs=[blk, blk, blk, blk],
        out_shape=[jax.ShapeDtypeStruct(mem.shape, F32)] * 2 + [jax.ShapeDtypeStruct(mem.shape, BF16)] * 2,
        compiler_params=_cparams(("parallel",)),
        name="memory_kv",
    )(mem, g.reshape(1, D_MODEL), wk, wv)


def _in_proj_kernel(x_ref, g_ref, w_ref, lb_ref, q_ref, kf_ref, vf_ref, kb_ref, vb_ref,
                    qh_ref, lf_ref, kk_ref, hi_ref, sg_ref):
    h = _rms(x_ref[...], g_ref[...]).astype(BF16)

    def proj(i):
        return _dot(h, w_ref[:, i * GROUP_W:(i + 1) * GROUP_W])

    q_ref[...] = (proj(0) * DA_SCALE).astype(BF16)
    dk = proj(1)
    kf_ref[...] = dk
    kb_ref[...] = dk.astype(BF16)
    dv = proj(2)
    vf_ref[...] = dv
    vb_ref[...] = dv.astype(BF16)
    hq = proj(3)
    qh_ref[...] = hq * jax.nn.sigmoid(hq)
    hf = proj(4)
    lb = lb_ref[...]
    lf_ref[...] = jnp.log(lb + (1.0 - lb) * jax.nn.sigmoid(hf))
    kk_ref[...] = (1.0 - lb) * jax.nn.sigmoid(-hf)
    hi_ref[...] = proj(5)
    hg = proj(6)
    sg_ref[...] = hg * jax.nn.sigmoid(hg)


def _in_proj(x2d, g, w_bf, lb, tm):
    n = x2d.shape[0]
    row = lambda i: (i, 0)
    fix = lambda i: (0, 0)
    o_spec = pl.BlockSpec((tm, GROUP_W), row)
    f32o = jax.ShapeDtypeStruct((n, GROUP_W), F32)
    bfo = jax.ShapeDtypeStruct((n, GROUP_W), BF16)
    return pl.pallas_call(
        _in_proj_kernel,
        grid=(n // tm,),
        in_specs=[pl.BlockSpec((tm, D_MODEL), row), pl.BlockSpec((1, D_MODEL), fix),
                  pl.BlockSpec((D_MODEL, N_PROJ * GROUP_W), fix), pl.BlockSpec((1, GROUP_W), fix)],
        out_specs=[o_spec] * 10,
        out_shape=[bfo, f32o, f32o, bfo, bfo, f32o, f32o, f32o, f32o, f32o],
        compiler_params=_cparams(("parallel",)),
        name="in_proj",
    )(x2d, g.reshape(1, D_MODEL), w_bf, lb.reshape(1, GROUP_W))


def _attn_kernel(*refs, tq, tk, past_len, tkp):
    if past_len:
        (slope_ref, lam_ref, q_ref, k_ref, v_ref, pk_ref, pv_ref, g_ref, o_ref,
         m_ref, l_ref, acc_ref) = refs
    else:
        slope_ref, lam_ref, q_ref, k_ref, v_ref, g_ref, o_ref, m_ref, l_ref, acc_ref = refs
    h = pl.program_id(1)
    qi = pl.program_id(2)
    slope = slope_ref[h]
    q = q_ref[0]
    lane = lax.broadcasted_iota(jnp.int32, (tq, HEAD_W), 1)
    q_maps = (jnp.where(lane < QK_DIM, q, jnp.zeros_like(q)),
              jnp.where(lane >= QK_DIM, q, jnp.zeros_like(q)))
    m_ref[...] = jnp.full(m_ref.shape, NEG_INF, F32)
    l_ref[...] = jnp.zeros(l_ref.shape, F32)
    acc_ref[...] = jnp.zeros(acc_ref.shape, F32)
    q0 = past_len + qi * tq

    def update(k, v, bias, visible):
        vb = v.astype(BF16)
        for c in range(2):
            s = _dot_nt(q_maps[c], k) + bias
            if visible is not None:
                s = jnp.where(visible, s, NEG_INF)
            m_old = m_ref[c]
            m_new = jnp.maximum(m_old, jnp.max(s, axis=-1, keepdims=True))
            p = jnp.exp(s - m_new)
            alpha = jnp.exp(m_old - m_new)
            l_ref[c] = alpha * l_ref[c] + jnp.sum(p, axis=-1, keepdims=True)
            acc_ref[c] = alpha * acc_ref[c] + _dot(p.astype(BF16), vb)
            m_ref[c] = m_new

    def past_bias(n_keys, k0):
        row = lax.broadcasted_iota(jnp.int32, (tq, n_keys), 0)
        col = lax.broadcasted_iota(jnp.int32, (tq, n_keys), 1)
        return (row - col + (q0 - k0)).astype(F32) * (-slope)

    if past_len:
        def past_body(j, carry):
            k0 = pl.multiple_of(j * tkp, tkp)
            k = pk_ref[0, pl.ds(k0, tkp), :].astype(BF16)
            update(k, pv_ref[0, pl.ds(k0, tkp), :], past_bias(tkp, k0), None)
            return carry
        lax.fori_loop(0, past_len // tkp, past_body, 0)

    def prev_body(j, carry):
        k0 = pl.multiple_of(j * tk, tk)
        update(k_ref[0, pl.ds(k0, tk), :], v_ref[0, pl.ds(k0, tk), :],
               past_bias(tk, past_len + k0), None)
        return carry
    lax.fori_loop(0, qi * (tq // tk), prev_body, 0)

    k0 = pl.multiple_of(qi * tq, tq)
    row = lax.broadcasted_iota(jnp.int32, (tq, tq), 0)
    col = lax.broadcasted_iota(jnp.int32, (tq, tq), 1)
    bias = jnp.abs(row - col).astype(F32) * (-slope)
    visible = (col // CHUNK) <= (row // CHUNK)
    update(k_ref[0, pl.ds(k0, tq), :], v_ref[0, pl.ds(k0, tq), :], bias, visible)

    lam = lam_ref[0]
    o = acc_ref[0] / l_ref[0] - lam * (acc_ref[1] / l_ref[1])
    o_ref[0] = (_rms(o, g_ref[...]) * (1.0 - LAM_INIT)).astype(o_ref.dtype)


def _diff_attn(q, k, v, past_k, past_v, slopes, lam, subln_g, tq):
    B, T, _ = q.shape
    past_len = 0 if past_k is None else past_k.shape[1]
    tkp = min(256, past_len) if past_len else 0
    smem = pl.BlockSpec(memory_space=pltpu.SMEM)
    qspec = pl.BlockSpec((1, tq, HEAD_W), lambda b, h, i: (b, i, h))
    kvspec = pl.BlockSpec((1, T, HEAD_W), lambda b, h, i: (b, 0, h))
    in_specs = [smem, smem, qspec, kvspec, kvspec]
    args = [slopes, lam, q, k, v]
    if past_len:
        pspec = pl.BlockSpec((1, past_len, HEAD_W), lambda b, h, i: (b, 0, h))
        in_specs += [pspec, pspec]
        args += [past_k, past_v]
    in_specs.append(pl.BlockSpec((1, HEAD_W), lambda b, h, i: (0, 0)))
    args.append(subln_g.reshape(1, HEAD_W))
    return pl.pallas_call(
        functools.partial(_attn_kernel, tq=tq, tk=tq, past_len=past_len, tkp=tkp),
        grid=(B, N_HEADS, T // tq),
        in_specs=in_specs,
        out_specs=qspec,
        out_shape=jax.ShapeDtypeStruct((B, T, GROUP_W), BF16),
        scratch_shapes=[pltpu.VMEM((2, tq, 1), F32), pltpu.VMEM((2, tq, 1), F32),
                        pltpu.VMEM((2, tq, HEAD_W), F32)],
        compiler_params=_cparams(("parallel", "parallel", "parallel")),
        name="diff_attn",
    )(*args)


def _split3(x):
    hi = x.astype(BF16)
    r = x - hi.astype(F32)
    mid = r.astype(BF16)
    lo = (r - mid.astype(F32)).astype(BF16)
    return hi, mid, lo


def _hgrn_kernel(q_ref, k_ref, lf_ref, v_ref, sg_ref, gn_ref, s0_ref, o_ref, sfin_ref,
                 st_ref, b_ref, *, tb):
    t = pl.program_id(1)
    n_chunks = tb // CHUNK

    @pl.when(t == 0)
    def _():
        for h in range(N_HEADS):
            st_ref[h] = s0_ref[0, h].T

    row = lax.broadcasted_iota(jnp.int32, (tb, tb), 0)
    col = lax.broadcasted_iota(jnp.int32, (tb, tb), 1)
    tri = jnp.where((col <= row) & (col // CHUNK == row // CHUNK), 1.0, 0.0).astype(BF16)
    hi, mid, lo = _split3(lf_ref[0])
    b_ref[...] = _dot(tri, hi) + _dot(tri, mid) + _dot(tri, lo)

    sub_row = lax.broadcasted_iota(jnp.int32, (8, HEAD_W), 0)
    zeros16 = jnp.zeros((SUB, HEAD_W), F32)

    def pad_rows(x, r0):
        parts = []
        if r0:
            parts.append(jnp.zeros((r0, HEAD_W), F32))
        parts.append(x)
        rest = CHUNK - r0 - x.shape[0]
        if rest:
            parts.append(jnp.zeros((rest, HEAD_W), F32))
        return jnp.concatenate(parts, axis=0) if len(parts) > 1 else x

    def chunk_body(c, carry):
        r0 = pl.multiple_of(c * CHUNK, CHUNK)
        for h in range(N_HEADS):
            hs = slice(h * HEAD_W, (h + 1) * HEAD_W)
            q = q_ref[0, pl.ds(r0, CHUNK), hs]
            k = k_ref[0, pl.ds(r0, CHUNK), hs]
            v = v_ref[0, pl.ds(r0, CHUNK), hs]
            b = b_ref[pl.ds(r0, CHUNK), hs]
            b_last = b[CHUNK - 1:CHUNK, :]
            st = st_ref[h]
            v_bf = v.astype(BF16)

            o = _dot_nt((q * jnp.exp(b)).astype(BF16), st.astype(BF16))

            q_segs, k_segs = [], []
            for i in range(1, CHUNK // SUB):
                lo_r, hi_r = i * SUB, (i + 1) * SUB
                ref_b = b[lo_r - 1:lo_r, :]
                q_segs.append(pad_rows(q[lo_r:hi_r] * jnp.exp(b[lo_r:hi_r] - ref_b), lo_r))
                k_segs.append(pad_rows(k[:lo_r] * jnp.exp(ref_b - b[:lo_r]), 0))
            a_off = _dot_nt(jnp.concatenate(q_segs, axis=1).astype(BF16),
                            jnp.concatenate(k_segs, axis=1).astype(BF16))
            o = o + _dot(a_off.astype(BF16), v_bf)

            diag = []
            for blk in range(CHUNK // SUB):
                base = blk * SUB
                qa, qb = q[base:base + 8], q[base + 8:base + SUB]
                ba, bb = b[base:base + 8], b[base + 8:base + SUB]
                oa = jnp.zeros((8, HEAD_W), F32)
                ob = jnp.zeros((8, HEAD_W), F32)
                for s in range(SUB):
                    ks = k[base + s:base + s + 1]
                    bs = b[base + s:base + s + 1]
                    vs = v[base + s:base + s + 1]
                    if s < 8:
                        e = jnp.exp(ba - bs)
                        if s:
                            e = jnp.where(sub_row >= s, e, 0.0)
                        oa = oa + jnp.sum(qa * ks * e, axis=-1, keepdims=True) * vs
                        ob = ob + jnp.sum(qb * ks * jnp.exp(bb - bs), axis=-1, keepdims=True) * vs
                    else:
                        e = jnp.exp(bb - bs)
                        if s > 8:
                            e = jnp.where(sub_row >= s - 8, e, 0.0)
                        ob = ob + jnp.sum(qb * ks * e, axis=-1, keepdims=True) * vs
                diag += [oa, ob]
            o = o + jnp.concatenate(diag, axis=0)

            k_dec = (k * jnp.exp(b_last - b)).astype(BF16)
            st_ref[h] = st * jnp.exp(b_last) + _dot_tn(v_bf, k_dec)

            out = _rms(o, gn_ref[...]) * sg_ref[0, pl.ds(r0, CHUNK), hs]
            o_ref[0, pl.ds(r0, CHUNK), hs] = out.astype(o_ref.dtype)
        return carry

    lax.fori_loop(0, n_chunks, chunk_body, 0)

    @pl.when(t == pl.num_programs(1) - 1)
    def _():
        for h in range(N_HEADS):
            sfin_ref[0, h] = st_ref[h].T


def _hgrn(qh, kk, lf, hi, sg, gnorm_g, s0, tb):
    B, T, _ = qh.shape
    tspec = pl.BlockSpec((1, tb, GROUP_W), lambda b, t: (b, t, 0))
    sspec = pl.BlockSpec((1, N_HEADS, HEAD_W, HEAD_W), lambda b, t: (b, 0, 0, 0))
    return pl.pallas_call(
        functools.partial(_hgrn_kernel, tb=tb),
        grid=(B, T // tb),
        in_specs=[tspec] * 5 + [pl.BlockSpec((1, HEAD_W), lambda b, t: (0, 0)), sspec],
        out_specs=[tspec, sspec],
        out_shape=[jax.ShapeDtypeStruct((B, T, GROUP_W), BF16),
                   jax.ShapeDtypeStruct((B, N_HEADS, HEAD_W, HEAD_W), F32)],
        scratch_shapes=[pltpu.VMEM((N_HEADS, HEAD_W, HEAD_W), F32), pltpu.VMEM((tb, GROUP_W), F32)],
        compiler_params=_cparams(("parallel", "arbitrary")),
        name="hgrn",
    )(qh, kk, lf, hi, sg, gnorm_g.reshape(1, HEAD_W), s0)


def _post_kernel(x_ref, oda_ref, ohg_ref, mk_ref, mv_ref, wout_ref, lnm_ref, wq_ref, wo_ref,
                 lnf_ref, wr_ref, br_ref, x2_ref, h3_ref, route_ref, om_ref, *, nb, tm):
    rows = nb * tm
    x = x_ref[...].reshape(rows, D_MODEL)
    mixed = (_dot(oda_ref[...].reshape(rows, GROUP_W), wout_ref[:GROUP_W, :])
             + _dot(ohg_ref[...].reshape(rows, GROUP_W), wout_ref[GROUP_W:, :]))
    x1 = x + mixed

    qm = (_dot(_rms(x1, lnm_ref[...]).astype(BF16), wq_ref[...]) * MEM_SCALE).astype(BF16)
    for b in range(nb):
        for h in range(MEM_HEADS):
            hs = slice(h * MEM_HEAD_DIM, (h + 1) * MEM_HEAD_DIM)
            s = _dot_nt(qm[b * tm:(b + 1) * tm, hs], mk_ref[b, :, hs].astype(BF16))
            e = jnp.exp(s - jnp.max(s, axis=-1, keepdims=True))
            p = e / jnp.sum(e, axis=-1, keepdims=True)
            om_ref[b * tm:(b + 1) * tm, hs] = _dot(
                p.astype(BF16), mv_ref[b, :, hs].astype(BF16)).astype(BF16)
    x2 = x1 + _dot(om_ref[...], wo_ref[...])
    x2_ref[...] = x2.reshape(nb, tm, D_MODEL)

    h3 = _rms(x2, lnf_ref[...])
    h3_ref[...] = h3.reshape(nb, tm, D_MODEL)

    r = _dot(h3.astype(BF16), wr_ref[...]) + br_ref[...]
    lane = lax.broadcasted_iota(jnp.int32, r.shape, 1).astype(F32)
    big = float(4 * HEAD_W)
    g_mask = lane < N_GROUPS
    gl = jnp.where(g_mask, r, NEG_INF)
    g_max = jnp.max(gl, axis=-1, keepdims=True)
    g_idx = jnp.min(jnp.where(gl == g_max, lane, big), axis=-1, keepdims=True)
    g_w = 1.0 / jnp.sum(jnp.where(g_mask, jnp.exp(r - g_max), 0.0), axis=-1, keepdims=True)
    e_lo = N_GROUPS + EXPERTS_PER_GROUP * g_idx
    el = jnp.where((lane >= e_lo) & (lane < e_lo + EXPERTS_PER_GROUP), r, NEG_INF)
    v1 = jnp.max(el, axis=-1, keepdims=True)
    i1 = jnp.min(jnp.where(el == v1, lane, big), axis=-1, keepdims=True)
    el2 = jnp.where(lane == i1, NEG_INF, el)
    v2 = jnp.max(el2, axis=-1, keepdims=True)
    i2 = jnp.min(jnp.where(el2 == v2, lane, big), axis=-1, keepdims=True)
    t = jnp.exp(v2 - v1)
    p1 = 1.0 / (1.0 + t)
    rec = jnp.where(lane == 0.0, i1 - N_GROUPS,
          jnp.where(lane == 1.0, i2 - N_GROUPS,
          jnp.where(lane == 2.0, p1 * g_w,
          jnp.where(lane == 3.0, t * p1 * g_w, 0.0))))
    route_ref[...] = rec[:, :ROUTE_W].reshape(nb, tm, ROUTE_W)


def _post_mix(x, oda, ohg, mem_k, mem_v, w_out, ln_mem, w_q, w_o, ln_ffn, w_r, b_r, nb, tm):
    B, T, _ = x.shape
    tok = lambda w: pl.BlockSpec((nb, tm, w), lambda b, t: (b, t, 0))
    memspec = pl.BlockSpec((nb, N_MEM, D_MODEL), lambda b, t: (b, 0, 0))
    fix = lambda s: pl.BlockSpec(s, lambda b, t: (0, 0))
    return pl.pallas_call(
        functools.partial(_post_kernel, nb=nb, tm=tm),
        grid=(B // nb, T // tm),
        in_specs=[tok(D_MODEL), tok(GROUP_W), tok(GROUP_W), memspec, memspec,
                  fix((D_MODEL, D_MODEL)), fix((1, D_MODEL)), fix((D_MODEL, D_MODEL)),
                  fix((D_MODEL, D_MODEL)), fix((1, D_MODEL)), fix((D_MODEL, HEAD_W)),
                  fix((1, HEAD_W))],
        out_specs=[tok(D_MODEL), tok(D_MODEL), tok(ROUTE_W)],
        out_shape=[jax.ShapeDtypeStruct((B, T, D_MODEL), F32), jax.ShapeDtypeStruct((B, T, D_MODEL), F32),
                   jax.ShapeDtypeStruct((B, T, ROUTE_W), F32)],
        scratch_shapes=[pltpu.VMEM((nb * tm, D_MODEL), BF16)],
        compiler_params=_cparams(("parallel", "parallel")),
        name="post_mix",
    )(x, oda, ohg, mem_k, mem_v, w_out, ln_mem.reshape(1, D_MODEL), w_q, w_o,
      ln_ffn.reshape(1, D_MODEL), w_r, b_r)


def _experts_kernel(blk_e_ref, n_real_ref, n_used_ref, src_ref, src_next_ref, dst_ref, h_hbm,
                    wg_ref, wu_ref, wd_ref, y_hbm, xbuf, ybuf, gsem, ssem, *, bm):
    del blk_e_ref
    i = pl.program_id(0)
    n_used = n_used_ref[0]
    slot = i % 2

    def start_gather(tab_ref, blk, s):
        def body(r, c):
            pltpu.make_async_copy(h_hbm.at[pl.ds(tab_ref[0, 0, r], 1), :],
                                  xbuf.at[s, pl.ds(r, 1), :], gsem.at[s]).start()
            return c
        lax.fori_loop(0, n_real_ref[blk], body, 0)

    def wait_rows(buf, sem, blk, s):
        n = n_real_ref[blk]
        n8 = pl.multiple_of(lax.shift_left(lax.shift_right_logical(n, 3), 3), 8)

        @pl.when(n8 > 0)
        def _():
            rows = pl.ds(0, n8)
            pltpu.make_async_copy(buf.at[s, rows], buf.at[s, rows], sem.at[s]).wait()

        def body(r, c):
            one = pl.ds(0, 1)
            pltpu.make_async_copy(buf.at[s, one], buf.at[s, one], sem.at[s]).wait()
            return c
        lax.fori_loop(0, n - n8, body, 0)

    @pl.when(i == 0)
    def _():
        xbuf[...] = jnp.zeros(xbuf.shape, F32)
        start_gather(src_ref, 0, 0)

    @pl.when(i + 1 < n_used)
    def _():
        start_gather(src_next_ref, i + 1, 1 - slot)

    @pl.when(i < n_used)
    def _():
        wait_rows(xbuf, gsem, i, slot)

        @pl.when(i >= 2)
        def _():
            wait_rows(ybuf, ssem, i - 2, slot)

        x = xbuf[slot].astype(BF16)
        hg = _dot(x, wg_ref[0])
        hu = _dot(x, wu_ref[0])
        hb = (hg * jax.nn.sigmoid(hg) * hu).astype(BF16)
        ybuf[slot] = _dot(hb, wd_ref[0])

        def body(r, c):
            pltpu.make_async_copy(ybuf.at[slot, pl.ds(r, 1), :],
                                  y_hbm.at[pl.ds(dst_ref[0, 0, r], 1), :], ssem.at[slot]).start()
            return c
        lax.fori_loop(0, n_real_ref[i], body, 0)

        @pl.when(i == n_used - 1)
        def _():
            wait_rows(ybuf, ssem, i, slot)

            @pl.when(i >= 1)
            def _():
                wait_rows(ybuf, ssem, i - 1, 1 - slot)


def _experts(h3, blk_e, n_real, n_used, src_tab, dst_tab, wg, wu, wd, n_rows_out, bm):
    n_blocks = src_tab.shape[0]
    tab = lambda f: pl.BlockSpec((1, 1, bm), f, memory_space=pltpu.SMEM)
    cur = lambda i, e, r, n: (i, 0, 0)
    nxt = lambda i, e, r, n: (jnp.minimum(i + 1, n_blocks - 1), 0, 0)
    wspec = lambda a, b: pl.BlockSpec((1, a, b), lambda i, e, r, n: (e[i], 0, 0))
    grid_spec = pltpu.PrefetchScalarGridSpec(
        num_scalar_prefetch=3,
        grid=(n_blocks,),
        in_specs=[tab(cur), tab(nxt), tab(cur), pl.BlockSpec(memory_space=pl.ANY),
                  wspec(D_MODEL, EXPERT_FF), wspec(D_MODEL, EXPERT_FF), wspec(EXPERT_FF, D_MODEL)],
        out_specs=pl.BlockSpec(memory_space=pl.ANY),
        scratch_shapes=[pltpu.VMEM((2, bm, D_MODEL), F32), pltpu.VMEM((2, bm, D_MODEL), F32),
                        pltpu.SemaphoreType.DMA((2,)), pltpu.SemaphoreType.DMA((2,))],
    )
    return pl.pallas_call(
        functools.partial(_experts_kernel, bm=bm),
        grid_spec=grid_spec,
        out_shape=jax.ShapeDtypeStruct((n_rows_out, D_MODEL), F32),
        compiler_params=_cparams(("arbitrary",)),
        name="experts",
    )(blk_e, n_real, n_used, src_tab, src_tab, dst_tab, h3, wg, wu, wd)


def _combine_kernel(x_ref, y0_ref, y1_ref, route_ref, g_ref, o_ref):
    rt = route_ref[...]
    y = x_ref[...] + rt[:, 2:3] * y0_ref[...] + rt[:, 3:4] * y1_ref[...]
    o_ref[...] = _rms(y, g_ref[...])


def _combine(x2, y, route, final_g, tm):
    n = x2.shape[0]
    nt = n // tm
    return pl.pallas_call(
        _combine_kernel,
        grid=(nt,),
        in_specs=[pl.BlockSpec((tm, D_MODEL), lambda i: (i, 0)),
                  pl.BlockSpec((tm, D_MODEL), lambda i: (i, 0)),
                  pl.BlockSpec((tm, D_MODEL), lambda i: (i + nt, 0)),
                  pl.BlockSpec((tm, ROUTE_W), lambda i: (i, 0)),
                  pl.BlockSpec((1, D_MODEL), lambda i: (0, 0))],
        out_specs=pl.BlockSpec((tm, D_MODEL), lambda i: (i, 0)),
        out_shape=jax.ShapeDtypeStruct((n, D_MODEL), F32),
        compiler_params=_cparams(("parallel",)),
        name="combine",
    )(x2, y, y, route, final_g.reshape(1, D_MODEL))


def _routing_tables(route, bm):
    n = route.shape[0]
    m_tot = TOP_K * n
    flat_e = jnp.concatenate([route[:, 0], route[:, 1]]).astype(jnp.int32)
    order = jnp.argsort(flat_e).astype(jnp.int32)
    counts = jnp.sum(flat_e[:, None] == jnp.arange(N_EXPERTS, dtype=jnp.int32)[None, :], axis=0,
                     dtype=jnp.int32)
    n_blk_e = (counts + bm - 1) // bm
    blk_end = jnp.cumsum(n_blk_e)
    start = jnp.cumsum(counts) - counts
    n_blocks = m_tot // bm + N_EXPERTS
    blk = jnp.arange(n_blocks, dtype=jnp.int32)
    blk_e = jnp.minimum(jnp.searchsorted(blk_end, blk, side="right"), N_EXPERTS - 1).astype(jnp.int32)
    blk_off = (blk - (blk_end[blk_e] - n_blk_e[blk_e])) * bm
    n_real = jnp.clip(counts[blk_e] - blk_off, 0, bm).astype(jnp.int32)
    p = jnp.arange(n_blocks * bm, dtype=jnp.int32)
    pe = blk_e[p // bm]
    off = blk_off[p // bm] + p % bm
    m = order[jnp.clip(start[pe] + off, 0, m_tot - 1)]
    shape = (n_blocks, 1, bm)
    return (blk_e, n_real, blk_end[N_EXPERTS - 1:].astype(jnp.int32), (m % n).reshape(shape),
            m.reshape(shape))


def _layer(x, past_k, past_v, s0, mem_k, mem_v, p, cfg):
    B, T, _ = x.shape
    n = B * T
    (q, kf, vf, kb, vb, qh, lf, kk, hi, sg) = _in_proj(
        x.reshape(n, D_MODEL), p["ln_mix"], p["w_in"], p["lb"], cfg["tm_in"])
    r3 = lambda a: a.reshape(B, T, GROUP_W)
    oda = _diff_attn(r3(q), r3(kb), r3(vb), past_k, past_v, p["slopes"], p["lam"], p["da_subln"],
                     cfg["tq"])
    ohg, s_new = _hgrn(r3(qh), r3(kk), r3(lf), r3(hi), r3(sg), p["hg_gnorm"], s0, cfg["tb"])
    x2, h3, route = _post_mix(x, oda, ohg, mem_k, mem_v, p["w_out"], p["ln_mem"], p["w_mem_q"],
                              p["w_mem_o"], p["ln_ffn"], p["w_r"], p["b_r"], cfg["nb"], cfg["tm_post"])
    route2 = route.reshape(n, ROUTE_W)
    blk_e, n_real, n_used, src_tab, dst_tab = _routing_tables(route2, cfg["bm"])
    y = _experts(h3.reshape(n, D_MODEL), blk_e, n_real, n_used, src_tab, dst_tab, p["e_gate"],
                 p["e_up"], p["e_down"], TOP_K * n, cfg["bm"])
    out = _combine(x2.reshape(n, D_MODEL), y, route2, p["final_g"], cfg["tm_in"])
    head = lambda a: a.reshape(1, B, T, N_HEADS, HEAD_W)
    return out.reshape(B, T, D_MODEL), head(kf), head(vf), s_new[None]


PROMPT_CFG = dict(tm_in=256, tq=256, tb=256, nb=1, tm_post=256, bm=512)
SAMPLE_CFG = dict(tm_in=256, tq=64, tb=64, nb=4, tm_post=64, bm=128)


def kernel(x_prompt, x_sample, mem_prompt, cache_diff_k, cache_diff_v, state_hgrn, cache_mem_k, cache_mem_v, ln_mix_g, w_in, da_lambda, da_subln_g, hg_lb_logits, hg_gnorm_g, w_out, ln_mem_g, mem_norm_g, w_mem_q, w_mem_k, w_mem_v, w_mem_o, ln_ffn_g, router_group_w, router_group_b, router_expert_w, router_expert_b, exp_w_gate, exp_w_up, exp_w_down, final_norm_g):
    assert w_in.shape[0] == 1, "single-layer configuration"
    lb_all = jnp.cumsum(jax.nn.softmax(hg_lb_logits.astype(F32), axis=0), axis=0)
    lp = da_lambda[0].astype(F32)
    lam = jnp.exp(jnp.sum(lp[0] * lp[1])) - jnp.exp(jnp.sum(lp[2] * lp[3])) + LAM_INIT
    w_r = jnp.zeros((D_MODEL, HEAD_W), F32)
    w_r = w_r.at[:, :N_GROUPS].set(router_group_w[0]).at[:, N_GROUPS:N_GROUPS + N_EXPERTS].set(router_expert_w[0])
    b_r = jnp.zeros((1, HEAD_W), F32)
    b_r = b_r.at[0, :N_GROUPS].set(router_group_b[0]).at[0, N_GROUPS:N_GROUPS + N_EXPERTS].set(router_expert_b[0])
    p = {
        "ln_mix": ln_mix_g[0], "w_in": w_in[0].astype(BF16), "lb": lb_all[0],
        "slopes": jnp.exp2(-8.0 * jnp.arange(1, N_HEADS + 1, dtype=F32) / N_HEADS),
        "lam": lam.reshape(1), "da_subln": da_subln_g[0], "hg_gnorm": hg_gnorm_g[0],
        "w_out": w_out[0].astype(BF16), "ln_mem": ln_mem_g[0], "w_mem_q": w_mem_q[0].astype(BF16),
        "w_mem_o": w_mem_o[0].astype(BF16), "ln_ffn": ln_ffn_g[0], "w_r": w_r.astype(BF16), "b_r": b_r,
        "e_gate": exp_w_gate[0].astype(BF16), "e_up": exp_w_up[0].astype(BF16),
        "e_down": exp_w_down[0].astype(BF16), "final_g": final_norm_g,
    }
    Bp, Tp, _ = x_prompt.shape
    Bs, Ts, _ = x_sample.shape
    past_len = cache_diff_k.shape[2]

    mkf, mvf, mkb, mvb = _memory_kv(mem_prompt, mem_norm_g[0], w_mem_k[0].astype(BF16),
                                    w_mem_v[0].astype(BF16))
    zero_state = jnp.zeros((Bp, N_HEADS, HEAD_W, HEAD_W), F32)
    yp, kp, vp, sp = _layer(x_prompt, None, None, zero_state, mkb, mvb, p, PROMPT_CFG)
    ys, ks, vs, ss = _layer(
        x_sample, cache_diff_k[0].reshape(Bs, past_len, GROUP_W), cache_diff_v[0].reshape(Bs, past_len, GROUP_W),
        state_hgrn[0], cache_mem_k[0].reshape(Bs, N_MEM, D_MODEL), cache_mem_v[0].reshape(Bs, N_MEM, D_MODEL),
        p, SAMPLE_CFG)
    mem_shape = (1, Bp, N_MEM, MEM_HEADS, MEM_HEAD_DIM)
    return (yp, ys, kp, vp, sp, mkf.reshape(mem_shape), mvf.reshape(mem_shape), ks, vs, ss)
```

```python
import functools
import math

import jax
import jax.numpy as jnp
from jax import lax
from jax.experimental import pallas as pl
from jax.experimental.pallas import tpu as pltpu

F32 = jnp.float32
BF16 = jnp.bfloat16

D_MODEL = 1024
EPS = 1e-5
CHUNK = 64
N_HEADS = 4
HEAD_W = 128
QK_DIM = 64
GROUP_W = N_HEADS * HEAD_W
N_PROJ = 7
DA_SCALE = QK_DIM ** -0.5
LOG2E = 1.4426950408889634
LAM_INIT = 0.8 - 0.6 * math.exp(-0.3 * 0)
N_MEM = 256
MEM_HEADS = 4
MEM_HEAD_DIM = D_MODEL // MEM_HEADS
MEM_SCALE = MEM_HEAD_DIM ** -0.5
N_GROUPS = 4
EXPERTS_PER_GROUP = 8
N_EXPERTS = N_GROUPS * EXPERTS_PER_GROUP
TOP_K = 2
EXPERT_FF = 512
ROUTE_W = 8
SUB = 16
SUB_KEYS = 128
NEG_INF = float("-inf")

VMEM_LIMIT = 48 * 1024 * 1024


def _cparams(sem):
    return pltpu.CompilerParams(dimension_semantics=sem, vmem_limit_bytes=VMEM_LIMIT)


def _rms(x, g):
    return x * lax.rsqrt(jnp.mean(x * x, axis=-1, keepdims=True) + EPS) * g


def _dot(a, b):
    return jnp.dot(a, b, preferred_element_type=F32)


def _dot_nt(a, b):
    return lax.dot_general(a, b, (((1,), (1,)), ((), ())), preferred_element_type=F32)


def _dot_tn(a, b):
    return lax.dot_general(a, b, (((0,), (0,)), ((), ())), preferred_element_type=F32)


def _memkv_kernel(m_ref, g_ref, wk_ref, wv_ref, kf_ref, vf_ref, kb_ref, vb_ref):
    mn = _rms(m_ref[0], g_ref[...]).astype(BF16)
    k = _dot(mn, wk_ref[...])
    v = _dot(mn, wv_ref[...])
    kf_ref[0] = k
    vf_ref[0] = v
    kb_ref[0] = k.astype(BF16)
    vb_ref[0] = v.astype(BF16)


def _memory_kv(mem, g, wk, wv):
    B = mem.shape[0]
    blk = pl.BlockSpec((1, N_MEM, D_MODEL), lambda b: (b, 0, 0))
    wspec = pl.BlockSpec((D_MODEL, D_MODEL), lambda b: (0, 0))
    return pl.pallas_call(
        _memkv_kernel,
        grid=(B,),
        in_specs=[blk, pl.BlockSpec((1, D_MODEL), lambda b: (0, 0)), wspec, wspec],
        out_specs=[blk, blk, blk, blk],
        out_shape=[jax.ShapeDtypeStruct(mem.shape, F32)] * 2 + [jax.ShapeDtypeStruct(mem.shape, BF16)] * 2,
        compiler_params=_cparams(("parallel",)),
        name="memory_kv",
    )(mem, g.reshape(1, D_MODEL), wk, wv)


def _in_proj_kernel(x_ref, g_ref, w_ref, lb_ref, q_ref, kf_ref, vf_ref, kb_ref, vt_ref,
                    qh_ref, lf_ref, kk_ref, hi_ref, sg_ref, *, nb, tm):
    rows = nb * tm
    h = _rms(x_ref[...].reshape(rows, D_MODEL), g_ref[...]).astype(BF16)

    def proj(i):
        return _dot(h, w_ref[:, i * GROUP_W:(i + 1) * GROUP_W])

    def put(ref, val):
        ref[...] = val.reshape(nb, tm, GROUP_W).astype(ref.dtype)

    put(q_ref, proj(0) * (DA_SCALE * LOG2E))
    dk = proj(1)
    put(kf_ref, dk)
    put(kb_ref, dk)
    dv = proj(2)
    put(vf_ref, dv)
    for b in range(nb):
        vt_ref[b] = dv[b * tm:(b + 1) * tm].T.astype(BF16)
    hq = proj(3)
    put(qh_ref, hq * jax.nn.sigmoid(hq))
    hf = proj(4)
    lb = lb_ref[...]
    put(lf_ref, jnp.log(lb + (1.0 - lb) * jax.nn.sigmoid(hf)))
    put(kk_ref, (1.0 - lb) * jax.nn.sigmoid(-hf))
    put(hi_ref, proj(5))
    hg = proj(6)
    put(sg_ref, hg * jax.nn.sigmoid(hg))


def _in_proj(x, g, w_bf, lb, nb, tm):
    B, T, _ = x.shape
    tok = lambda w: pl.BlockSpec((nb, tm, w), lambda b, t: (b, t, 0))
    fix = lambda s: pl.BlockSpec(s, lambda b, t: (0, 0))
    f32o = jax.ShapeDtypeStruct((B, T, GROUP_W), F32)
    bfo = jax.ShapeDtypeStruct((B, T, GROUP_W), BF16)
    o_spec = tok(GROUP_W)
    vt_spec = pl.BlockSpec((nb, GROUP_W, tm), lambda b, t: (b, 0, t))
    return pl.pallas_call(
        functools.partial(_in_proj_kernel, nb=nb, tm=tm),
        grid=(B // nb, T // tm),
        in_specs=[tok(D_MODEL), fix((1, D_MODEL)), fix((D_MODEL, N_PROJ * GROUP_W)), fix((1, GROUP_W))],
        out_specs=[o_spec] * 4 + [vt_spec] + [o_spec] * 5,
        out_shape=[bfo, f32o, f32o, bfo, jax.ShapeDtypeStruct((B, GROUP_W, T), BF16),
                   f32o, f32o, f32o, f32o, f32o],
        compiler_params=_cparams(("parallel", "parallel")),
        name="in_proj",
    )(x, g.reshape(1, D_MODEL), w_bf, lb.reshape(1, GROUP_W))


def _attn_kernel(*refs, tq, tk, past_len, tkp):
    if past_len:
        (slope_ref, lam_ref, q_ref, k_ref, vt_ref, pk_ref, pv_ref, g_ref, o_ref,
         m_ref, l_ref, acc_ref, base_ref, t_ref) = refs
    else:
        (slope_ref, lam_ref, q_ref, k_ref, vt_ref, g_ref, o_ref,
         m_ref, l_ref, acc_ref, base_ref, t_ref) = refs
    h = pl.program_id(1)
    qi = pl.program_id(2)
    slope2 = slope_ref[h] * LOG2E
    qt = q_ref[0].astype(F32).T.astype(BF16)
    zeros = jnp.zeros((QK_DIM, tq), BF16)
    qt_maps = (jnp.concatenate([qt[:QK_DIM], zeros], axis=0),
               jnp.concatenate([zeros, qt[QK_DIM:]], axis=0))
    m_ref[...] = jnp.full(m_ref.shape, NEG_INF, F32)
    l_ref[...] = jnp.zeros(l_ref.shape, F32)
    acc_ref[...] = jnp.zeros(acc_ref.shape, F32)
    q0 = past_len + qi * tq

    def update(n_keys, get_k, get_vt, get_bias, shift):
        sub = min(n_keys, SUB_KEYS)
        n_sub = n_keys // sub
        mx = [None, None]
        for u in range(n_sub):
            bias, visible = get_bias(u)
            k = get_k(u)
            for c in range(2):
                t = _dot(k, qt_maps[c]) + bias
                if visible is not None:
                    t = jnp.where(visible, t, NEG_INF)
                t_ref[c, u * sub:(u + 1) * sub, :] = t
                mu = jnp.max(t, axis=0, keepdims=True)
                mx[c] = mu if mx[c] is None else jnp.maximum(mx[c], mu)
        off, alpha = [], []
        for c in range(2):
            m_old = m_ref[c]
            m_new = jnp.maximum(m_old, mx[c] + shift)
            off.append(m_new - shift)
            alpha.append(jnp.exp2(m_old - m_new))
            m_ref[c] = m_new
        lsum = [None, None]
        pv = [None, None]
        for u in range(n_sub):
            vt = get_vt(u)
            for c in range(2):
                p = jnp.exp2(t_ref[c, u * sub:(u + 1) * sub, :] - off[c])
                ls = jnp.sum(p, axis=0, keepdims=True)
                pu = _dot(vt, p.astype(BF16))
                lsum[c] = ls if lsum[c] is None else lsum[c] + ls
                pv[c] = pu if pv[c] is None else pv[c] + pu
        for c in range(2):
            l_ref[c] = alpha[c] * l_ref[c] + lsum[c]
            acc_ref[c] = alpha[c] * acc_ref[c] + pv[c]

    def rel_bias(n_keys, key0):
        key = lax.broadcasted_iota(jnp.int32, (n_keys, tq), 0) + key0
        qry = lax.broadcasted_iota(jnp.int32, (n_keys, tq), 1)
        return (qry - key).astype(F32) * (-slope2)

    def block_shift(k0):
        return (q0 - k0).astype(F32) * (-slope2)

    if past_len:
        sub_p = min(tkp, SUB_KEYS)

        def past_body(j, carry):
            k0 = pl.multiple_of(j * tkp, tkp)
            update(tkp,
                   lambda u: pk_ref[0, pl.ds(k0 + u * sub_p, sub_p), :].astype(BF16),
                   lambda u: pv_ref[0, pl.ds(k0 + u * sub_p, sub_p), :].T.astype(BF16),
                   lambda u: (rel_bias(sub_p, u * sub_p), None), block_shift(k0))
            return carry
        lax.fori_loop(0, past_len // tkp, past_body, 0)

    n_q = k_ref.shape[1] // tq
    if n_q > 1:
        sub_k = min(tk, SUB_KEYS)
        base_ref[...] = rel_bias(tk, 0)

        def prev_body(j, carry):
            k0 = pl.multiple_of(j * tk, tk)
            update(tk,
                   lambda u: k_ref[0, pl.ds(k0 + u * sub_k, sub_k), :],
                   lambda u: vt_ref[0, :, pl.ds(k0 + u * sub_k, sub_k)],
                   lambda u: (base_ref[u * sub_k:(u + 1) * sub_k, :], None),
                   block_shift(past_len + k0))
            return carry
        lax.fori_loop(0, qi * (tq // tk), prev_body, 0)

    kd = 0 if n_q == 1 else pl.multiple_of(qi * tq, tq)
    sub_d = min(tq, SUB_KEYS)

    def diag_bias(u):
        key = lax.broadcasted_iota(jnp.int32, (sub_d, tq), 0) + u * sub_d
        qry = lax.broadcasted_iota(jnp.int32, (sub_d, tq), 1)
        return (jnp.abs(qry - key).astype(F32) * (-slope2), (key // CHUNK) <= (qry // CHUNK))

    update(tq,
           lambda u: k_ref[0, pl.ds(kd + u * sub_d, sub_d), :],
           lambda u: vt_ref[0, :, pl.ds(kd + u * sub_d, sub_d)],
           diag_bias, 0.0)

    lam = lam_ref[0]
    ot = acc_ref[0] / l_ref[0] - lam * (acc_ref[1] / l_ref[1])
    ot = ot * lax.rsqrt(jnp.mean(ot * ot, axis=0, keepdims=True) + EPS)
    o_ref[0] = (ot.T * g_ref[...] * (1.0 - LAM_INIT)).astype(o_ref.dtype)


def _diff_attn(q, k, vt, past_k, past_v, slopes, lam, subln_g, tq, tk):
    B, T, _ = q.shape
    past_len = 0 if past_k is None else past_k.shape[1]
    tkp = min(512, past_len) if past_len else 0
    smem = pl.BlockSpec(memory_space=pltpu.SMEM)
    qspec = pl.BlockSpec((1, tq, HEAD_W), lambda b, h, i: (b, i, h))
    in_specs = [smem, smem, qspec,
                pl.BlockSpec((1, T, HEAD_W), lambda b, h, i: (b, 0, h)),
                pl.BlockSpec((1, HEAD_W, T), lambda b, h, i: (b, h, 0))]
    args = [slopes, lam, q, k, vt]
    if past_len:
        pspec = pl.BlockSpec((1, past_len, HEAD_W), lambda b, h, i: (b, 0, h))
        in_specs += [pspec, pspec]
        args += [past_k, past_v]
    in_specs.append(pl.BlockSpec((1, HEAD_W), lambda b, h, i: (0, 0)))
    args.append(subln_g.reshape(1, HEAD_W))
    return pl.pallas_call(
        functools.partial(_attn_kernel, tq=tq, tk=tk, past_len=past_len, tkp=tkp),
        grid=(B, N_HEADS, T // tq),
        in_specs=in_specs,
        out_specs=qspec,
        out_shape=jax.ShapeDtypeStruct((B, T, GROUP_W), BF16),
        scratch_shapes=[pltpu.VMEM((2, 1, tq), F32), pltpu.VMEM((2, 1, tq), F32),
                        pltpu.VMEM((2, HEAD_W, tq), F32), pltpu.VMEM((tk, tq), F32),
                        pltpu.VMEM((2, max(tk, tq, tkp), tq), F32)],
        compiler_params=_cparams(("parallel", "parallel", "parallel")),
        name="diff_attn",
    )(*args)


def _split3(x):
    hi = x.astype(BF16)
    r = x - hi.astype(F32)
    mid = r.astype(BF16)
    lo = (r - mid.astype(F32)).astype(BF16)
    return hi, mid, lo


def _hgrn_kernel(q_ref, k_ref, lf_ref, v_ref, sg_ref, gn_ref, s0_ref, o_ref, sfin_ref,
                 st_ref, b_ref, *, tb):
    t = pl.program_id(1)
    n_chunks = tb // CHUNK

    @pl.when(t == 0)
    def _():
        for h in range(N_HEADS):
            st_ref[h] = s0_ref[0, h].T

    row = lax.broadcasted_iota(jnp.int32, (tb, tb), 0)
    col = lax.broadcasted_iota(jnp.int32, (tb, tb), 1)
    tri = jnp.where((col <= row) & (col // CHUNK == row // CHUNK), 1.0, 0.0).astype(BF16)
    hi, mid, lo = _split3(lf_ref[0])
    b_ref[...] = _dot(tri, hi) + _dot(tri, mid) + _dot(tri, lo)

    sub_row = lax.broadcasted_iota(jnp.int32, (8, HEAD_W), 0)
    zeros16 = jnp.zeros((SUB, HEAD_W), F32)

    def pad_rows(x, r0):
        parts = []
        if r0:
            parts.append(jnp.zeros((r0, HEAD_W), F32))
        parts.append(x)
        rest = CHUNK - r0 - x.shape[0]
        if rest:
            parts.append(jnp.zeros((rest, HEAD_W), F32))
        return jnp.concatenate(parts, axis=0) if len(parts) > 1 else x

    def chunk_body(c, carry):
        r0 = pl.multiple_of(c * CHUNK, CHUNK)
        for h in range(N_HEADS):
            hs = slice(h * HEAD_W, (h + 1) * HEAD_W)
            q = q_ref[0, pl.ds(r0, CHUNK), hs]
            k = k_ref[0, pl.ds(r0, CHUNK), hs]
            v = v_ref[0, pl.ds(r0, CHUNK), hs]
            b = b_ref[pl.ds(r0, CHUNK), hs]
            b_last = b[CHUNK - 1:CHUNK, :]
            st = st_ref[h]
            v_bf = v.astype(BF16)

            o = _dot_nt((q * jnp.exp(b)).astype(BF16), st.astype(BF16))

            q_segs, k_segs = [], []
            for i in range(1, CHUNK // SUB):
                lo_r, hi_r = i * SUB, (i + 1) * SUB
                ref_b = b[lo_r - 1:lo_r, :]
                q_segs.append(pad_rows(q[lo_r:hi_r] * jnp.exp(b[lo_r:hi_r] - ref_b), lo_r))
                k_segs.append(pad_rows(k[:lo_r] * jnp.exp(ref_b - b[:lo_r]), 0))
            a_off = _dot_nt(jnp.concatenate(q_segs, axis=1).astype(BF16),
                            jnp.concatenate(k_segs, axis=1).astype(BF16))
            o = o + _dot(a_off.astype(BF16), v_bf)

            diag = []
            for blk in range(CHUNK // SUB):
                base = blk * SUB
                qa, qb = q[base:base + 8], q[base + 8:base + SUB]
                ba, bb = b[base:base + 8], b[base + 8:base + SUB]
                oa = jnp.zeros((8, HEAD_W), F32)
                ob = jnp.zeros((8, HEAD_W), F32)
                for s in range(SUB):
                    ks = k[base + s:base + s + 1]
                    bs = b[base + s:base + s + 1]
                    vs = v[base + s:base + s + 1]
                    if s < 8:
                        e = jnp.exp(ba - bs)
                        if s:
                            e = jnp.where(sub_row >= s, e, 0.0)
                        oa = oa + jnp.sum(qa * ks * e, axis=-1, keepdims=True) * vs
                        ob = ob + jnp.sum(qb * ks * jnp.exp(bb - bs), axis=-1, keepdims=True) * vs
                    else:
                        e = jnp.exp(bb - bs)
                        if s > 8:
                            e = jnp.where(sub_row >= s - 8, e, 0.0)
                        ob = ob + jnp.sum(qb * ks * e, axis=-1, keepdims=True) * vs
                diag += [oa, ob]
            o = o + jnp.concatenate(diag, axis=0)

            k_dec = (k * jnp.exp(b_last - b)).astype(BF16)
            st_ref[h] = st * jnp.exp(b_last) + _dot_tn(v_bf, k_dec)

            out = _rms(o, gn_ref[...]) * sg_ref[0, pl.ds(r0, CHUNK), hs]
            o_ref[0, pl.ds(r0, CHUNK), hs] = out.astype(o_ref.dtype)
        return carry

    lax.fori_loop(0, n_chunks, chunk_body, 0)

    @pl.when(t == pl.num_programs(1) - 1)
    def _():
        for h in range(N_HEADS):
            sfin_ref[0, h] = st_ref[h].T


def _hgrn(qh, kk, lf, hi, sg, gnorm_g, s0, tb):
    B, T, _ = qh.shape
    tspec = pl.BlockSpec((1, tb, GROUP_W), lambda b, t: (b, t, 0))
    sspec = pl.BlockSpec((1, N_HEADS, HEAD_W, HEAD_W), lambda b, t: (b, 0, 0, 0))
    return pl.pallas_call(
        functools.partial(_hgrn_kernel, tb=tb),
        grid=(B, T // tb),
        in_specs=[tspec] * 5 + [pl.BlockSpec((1, HEAD_W), lambda b, t: (0, 0)), sspec],
        out_specs=[tspec, sspec],
        out_shape=[jax.ShapeDtypeStruct((B, T, GROUP_W), BF16),
                   jax.ShapeDtypeStruct((B, N_HEADS, HEAD_W, HEAD_W), F32)],
        scratch_shapes=[pltpu.VMEM((N_HEADS, HEAD_W, HEAD_W), F32), pltpu.VMEM((tb, GROUP_W), F32)],
        compiler_params=_cparams(("parallel", "arbitrary")),
        name="hgrn",
    )(qh, kk, lf, hi, sg, gnorm_g.reshape(1, HEAD_W), s0)


def _post_kernel(x_ref, oda_ref, ohg_ref, mk_ref, mv_ref, wout_ref, lnm_ref, wq_ref, wo_ref,
                 lnf_ref, wr_ref, br_ref, x2_ref, h3_ref, route_ref, om_ref, *, nb, tm):
    rows = nb * tm
    x = x_ref[...].reshape(rows, D_MODEL)
    mixed = (_dot(oda_ref[...].reshape(rows, GROUP_W), wout_ref[:GROUP_W, :])
             + _dot(ohg_ref[...].reshape(rows, GROUP_W), wout_ref[GROUP_W:, :]))
    x1 = x + mixed

    qm = (_dot(_rms(x1, lnm_ref[...]).astype(BF16), wq_ref[...]) * MEM_SCALE).astype(BF16)
    for b in range(nb):
        for h in range(MEM_HEADS):
            hs = slice(h * MEM_HEAD_DIM, (h + 1) * MEM_HEAD_DIM)
            s = _dot_nt(qm[b * tm:(b + 1) * tm, hs], mk_ref[b, :, hs].astype(BF16))
            e = jnp.exp(s - jnp.max(s, axis=-1, keepdims=True))
            p = e / jnp.sum(e, axis=-1, keepdims=True)
            om_ref[b * tm:(b + 1) * tm, hs] = _dot(
                p.astype(BF16), mv_ref[b, :, hs].astype(BF16)).astype(BF16)
    x2 = x1 + _dot(om_ref[...], wo_ref[...])
    x2_ref[...] = x2.reshape(nb, tm, D_MODEL)

    h3 = _rms(x2, lnf_ref[...])
    h3_ref[...] = h3.reshape(nb, tm, D_MODEL)

    r = _dot(h3.astype(BF16), wr_ref[...]) + br_ref[...]
    lane = lax.broadcasted_iota(jnp.int32, r.shape, 1).astype(F32)
    big = float(4 * HEAD_W)
    g_mask = lane < N_GROUPS
    gl = jnp.where(g_mask, r, NEG_INF)
    g_max = jnp.max(gl, axis=-1, keepdims=True)
    g_idx = jnp.min(jnp.where(gl == g_max, lane, big), axis=-1, keepdims=True)
    g_w = 1.0 / jnp.sum(jnp.where(g_mask, jnp.exp(r - g_max), 0.0), axis=-1, keepdims=True)
    e_lo = N_GROUPS + EXPERTS_PER_GROUP * g_idx
    el = jnp.where((lane >= e_lo) & (lane < e_lo + EXPERTS_PER_GROUP), r, NEG_INF)
    v1 = jnp.max(el, axis=-1, keepdims=True)
    i1 = jnp.min(jnp.where(el == v1, lane, big), axis=-1, keepdims=True)
    el2 = jnp.where(lane == i1, NEG_INF, el)
    v2 = jnp.max(el2, axis=-1, keepdims=True)
    i2 = jnp.min(jnp.where(el2 == v2, lane, big), axis=-1, keepdims=True)
    t = jnp.exp(v2 - v1)
    p1 = 1.0 / (1.0 + t)
    rec = jnp.where(lane == 0.0, i1 - N_GROUPS,
          jnp.where(lane == 1.0, i2 - N_GROUPS,
          jnp.where(lane == 2.0, p1 * g_w,
          jnp.where(lane == 3.0, t * p1 * g_w, 0.0))))
    route_ref[...] = rec[:, :ROUTE_W].reshape(nb, tm, ROUTE_W)


def _post_mix(x, oda, ohg, mem_k, mem_v, w_out, ln_mem, w_q, w_o, ln_ffn, w_r, b_r, nb, tm):
    B, T, _ = x.shape
    tok = lambda w: pl.BlockSpec((nb, tm, w), lambda b, t: (b, t, 0))
    memspec = pl.BlockSpec((nb, N_MEM, D_MODEL), lambda b, t: (b, 0, 0))
    fix = lambda s: pl.BlockSpec(s, lambda b, t: (0, 0))
    return pl.pallas_call(
        functools.partial(_post_kernel, nb=nb, tm=tm),
        grid=(B // nb, T // tm),
        in_specs=[tok(D_MODEL), tok(GROUP_W), tok(GROUP_W), memspec, memspec,
                  fix((D_MODEL, D_MODEL)), fix((1, D_MODEL)), fix((D_MODEL, D_MODEL)),
                  fix((D_MODEL, D_MODEL)), fix((1, D_MODEL)), fix((D_MODEL, HEAD_W)),
                  fix((1, HEAD_W))],
        out_specs=[tok(D_MODEL), tok(D_MODEL), tok(ROUTE_W)],
        out_shape=[jax.ShapeDtypeStruct((B, T, D_MODEL), F32), jax.ShapeDtypeStruct((B, T, D_MODEL), F32),
                   jax.ShapeDtypeStruct((B, T, ROUTE_W), F32)],
        scratch_shapes=[pltpu.VMEM((nb * tm, D_MODEL), BF16)],
        compiler_params=_cparams(("parallel", "parallel")),
        name="post_mix",
    )(x, oda, ohg, mem_k, mem_v, w_out, ln_mem.reshape(1, D_MODEL), w_q, w_o,
      ln_ffn.reshape(1, D_MODEL), w_r, b_r)


def _experts_kernel(blk_e_ref, n_real_ref, n_used_ref, src_ref, src_next_ref, dst_ref, h_hbm,
                    wg_ref, wu_ref, wd_ref, y_hbm, xbuf, ybuf, gsem, ssem, *, bm):
    del blk_e_ref
    i = pl.program_id(0)
    n_used = n_used_ref[0]
    slot = i % 2

    def start_gather(tab_ref, blk, s):
        def body(r, c):
            pltpu.make_async_copy(h_hbm.at[pl.ds(tab_ref[0, 0, r], 1), :],
                                  xbuf.at[s, pl.ds(r, 1), :], gsem.at[s]).start()
            return c
        lax.fori_loop(0, n_real_ref[blk], body, 0)

    def wait_rows(buf, sem, blk, s):
        n = n_real_ref[blk]
        n8 = pl.multiple_of(lax.shift_left(lax.shift_right_logical(n, 3), 3), 8)

        @pl.when(n8 > 0)
        def _():
            rows = pl.ds(0, n8)
            pltpu.make_async_copy(buf.at[s, rows], buf.at[s, rows], sem.at[s]).wait()

        def body(r, c):
            one = pl.ds(0, 1)
            pltpu.make_async_copy(buf.at[s, one], buf.at[s, one], sem.at[s]).wait()
            return c
        lax.fori_loop(0, n - n8, body, 0)

    @pl.when(i == 0)
    def _():
        xbuf[...] = jnp.zeros(xbuf.shape, F32)
        start_gather(src_ref, 0, 0)

    @pl.when(i + 1 < n_used)
    def _():
        start_gather(src_next_ref, i + 1, 1 - slot)

    @pl.when(i < n_used)
    def _():
        wait_rows(xbuf, gsem, i, slot)

        @pl.when(i >= 2)
        def _():
            wait_rows(ybuf, ssem, i - 2, slot)

        x = xbuf[slot].astype(BF16)
        hg = _dot(x, wg_ref[0])
        hu = _dot(x, wu_ref[0])
        hb = (hg * jax.nn.sigmoid(hg) * hu).astype(BF16)
        ybuf[slot] = _dot(hb, wd_ref[0])

        def body(r, c):
            pltpu.make_async_copy(ybuf.at[slot, pl.ds(r, 1), :],
                                  y_hbm.at[pl.ds(dst_ref[0, 0, r], 1), :], ssem.at[slot]).start()
            return c
        lax.fori_loop(0, n_real_ref[i], body, 0)

        @pl.when(i == n_used - 1)
        def _():
            wait_rows(ybuf, ssem, i, slot)

            @pl.when(i >= 1)
            def _():
                wait_rows(ybuf, ssem, i - 1, 1 - slot)


def _experts(h3, blk_e, n_real, n_used, src_tab, dst_tab, wg, wu, wd, n_rows_out, bm):
    n_blocks = src_tab.shape[0]
    tab = lambda f: pl.BlockSpec((1, 1, bm), f, memory_space=pltpu.SMEM)
    cur = lambda i, e, r, n: (i, 0, 0)
    nxt = lambda i, e, r, n: (jnp.minimum(i + 1, n_blocks - 1), 0, 0)
    wspec = lambda a, b: pl.BlockSpec((1, a, b), lambda i, e, r, n: (e[i], 0, 0))
    grid_spec = pltpu.PrefetchScalarGridSpec(
        num_scalar_prefetch=3,
        grid=(n_blocks,),
        in_specs=[tab(cur), tab(nxt), tab(cur), pl.BlockSpec(memory_space=pl.ANY),
                  wspec(D_MODEL, EXPERT_FF), wspec(D_MODEL, EXPERT_FF), wspec(EXPERT_FF, D_MODEL)],
        out_specs=pl.BlockSpec(memory_space=pl.ANY),
        scratch_shapes=[pltpu.VMEM((2, bm, D_MODEL), F32), pltpu.VMEM((2, bm, D_MODEL), F32),
                        pltpu.SemaphoreType.DMA((2,)), pltpu.SemaphoreType.DMA((2,))],
    )
    return pl.pallas_call(
        functools.partial(_experts_kernel, bm=bm),
        grid_spec=grid_spec,
        out_shape=jax.ShapeDtypeStruct((n_rows_out, D_MODEL), F32),
        compiler_params=_cparams(("arbitrary",)),
        name="experts",
    )(blk_e, n_real, n_used, src_tab, src_tab, dst_tab, h3, wg, wu, wd)


def _combine_kernel(x_ref, y0_ref, y1_ref, route_ref, g_ref, o_ref):
    rt = route_ref[...]
    y = x_ref[...] + rt[:, 2:3] * y0_ref[...] + rt[:, 3:4] * y1_ref[...]
    o_ref[...] = _rms(y, g_ref[...])


def _combine(x2, y, route, final_g, tm):
    n = x2.shape[0]
    nt = n // tm
    return pl.pallas_call(
        _combine_kernel,
        grid=(nt,),
        in_specs=[pl.BlockSpec((tm, D_MODEL), lambda i: (i, 0)),
                  pl.BlockSpec((tm, D_MODEL), lambda i: (i, 0)),
                  pl.BlockSpec((tm, D_MODEL), lambda i: (i + nt, 0)),
                  pl.BlockSpec((tm, ROUTE_W), lambda i: (i, 0)),
                  pl.BlockSpec((1, D_MODEL), lambda i: (0, 0))],
        out_specs=pl.BlockSpec((tm, D_MODEL), lambda i: (i, 0)),
        out_shape=jax.ShapeDtypeStruct((n, D_MODEL), F32),
        compiler_params=_cparams(("parallel",)),
        name="combine",
    )(x2, y, y, route, final_g.reshape(1, D_MODEL))


def _routing_tables(route, bm):
    n = route.shape[0]
    m_tot = TOP_K * n
    flat_e = jnp.concatenate([route[:, 0], route[:, 1]]).astype(jnp.int32)
    order = jnp.argsort(flat_e).astype(jnp.int32)
    counts = jnp.sum(flat_e[:, None] == jnp.arange(N_EXPERTS, dtype=jnp.int32)[None, :], axis=0,
                     dtype=jnp.int32)
    n_blk_e = (counts + bm - 1) // bm
    blk_end = jnp.cumsum(n_blk_e)
    start = jnp.cumsum(counts) - counts
    n_blocks = m_tot // bm + N_EXPERTS
    blk = jnp.arange(n_blocks, dtype=jnp.int32)
    blk_e = jnp.minimum(jnp.sum(blk[:, None] >= blk_end[None, :], axis=1, dtype=jnp.int32),
                        N_EXPERTS - 1)
    blk_off = (blk - (blk_end[blk_e] - n_blk_e[blk_e])) * bm
    n_real = jnp.clip(counts[blk_e] - blk_off, 0, bm).astype(jnp.int32)
    row0 = jnp.minimum(start[blk_e] + blk_off, m_tot)
    order_pad = jnp.concatenate([order, jnp.zeros((bm,), jnp.int32)])
    m = jax.vmap(lambda s: lax.dynamic_slice(order_pad, (s,), (bm,)))(row0)
    shape = (n_blocks, 1, bm)
    return (blk_e, n_real, blk_end[N_EXPERTS - 1:].astype(jnp.int32), (m % n).reshape(shape),
            m.reshape(shape))


def _layer(x, past_k, past_v, s0, mem_k, mem_v, p, cfg):
    B, T, _ = x.shape
    n = B * T
    (q, kf, vf, kb, vt, qh, lf, kk, hi, sg) = _in_proj(
        x, p["ln_mix"], p["w_in"], p["lb"], cfg["nb"], cfg["tm_post"])
    oda = _diff_attn(q, kb, vt, past_k, past_v, p["slopes"], p["lam"], p["da_subln"],
                     cfg["tq"], cfg["tk"])
    ohg, s_new = _hgrn(qh, kk, lf, hi, sg, p["hg_gnorm"], s0, cfg["tb"])
    x2, h3, route = _post_mix(x, oda, ohg, mem_k, mem_v, p["w_out"], p["ln_mem"], p["w_mem_q"],
                              p["w_mem_o"], p["ln_ffn"], p["w_r"], p["b_r"], cfg["nb"], cfg["tm_post"])
    route2 = route.reshape(n, ROUTE_W)
    blk_e, n_real, n_used, src_tab, dst_tab = _routing_tables(route2, cfg["bm"])
    y = _experts(h3.reshape(n, D_MODEL), blk_e, n_real, n_used, src_tab, dst_tab, p["e_gate"],
                 p["e_up"], p["e_down"], TOP_K * n, cfg["bm"])
    out = _combine(x2.reshape(n, D_MODEL), y, route2, p["final_g"], cfg["tm_in"])
    head = lambda a: a.reshape(1, B, T, N_HEADS, HEAD_W)
    return out.reshape(B, T, D_MODEL), head(kf), head(vf), s_new[None]


PROMPT_CFG = dict(tm_in=256, tq=512, tk=512, tb=256, nb=1, tm_post=256, bm=512)
SAMPLE_CFG = dict(tm_in=256, tq=64, tk=64, tb=64, nb=4, tm_post=64, bm=128)


def kernel(x_prompt, x_sample, mem_prompt, cache_diff_k, cache_diff_v, state_hgrn, cache_mem_k, cache_mem_v, ln_mix_g, w_in, da_lambda, da_subln_g, hg_lb_logits, hg_gnorm_g, w_out, ln_mem_g, mem_norm_g, w_mem_q, w_mem_k, w_mem_v, w_mem_o, ln_ffn_g, router_group_w, router_group_b, router_expert_w, router_expert_b, exp_w_gate, exp_w_up, exp_w_down, final_norm_g):
    assert w_in.shape[0] == 1, "single-layer configuration"
    lb_all = jnp.cumsum(jax.nn.softmax(hg_lb_logits.astype(F32), axis=0), axis=0)
    lp = da_lambda[0].astype(F32)
    lam = jnp.exp(jnp.sum(lp[0] * lp[1])) - jnp.exp(jnp.sum(lp[2] * lp[3])) + LAM_INIT
    w_r = jnp.zeros((D_MODEL, HEAD_W), F32)
    w_r = w_r.at[:, :N_GROUPS].set(router_group_w[0]).at[:, N_GROUPS:N_GROUPS + N_EXPERTS].set(router_expert_w[0])
    b_r = jnp.zeros((1, HEAD_W), F32)
    b_r = b_r.at[0, :N_GROUPS].set(router_group_b[0]).at[0, N_GROUPS:N_GROUPS + N_EXPERTS].set(router_expert_b[0])
    p = {
        "ln_mix": ln_mix_g[0], "w_in": w_in[0].astype(BF16), "lb": lb_all[0],
        "slopes": jnp.exp2(-8.0 * jnp.arange(1, N_HEADS + 1, dtype=F32) / N_HEADS),
        "lam": lam.reshape(1), "da_subln": da_subln_g[0], "hg_gnorm": hg_gnorm_g[0],
        "w_out": w_out[0].astype(BF16), "ln_mem": ln_mem_g[0], "w_mem_q": w_mem_q[0].astype(BF16),
        "w_mem_o": w_mem_o[0].astype(BF16), "ln_ffn": ln_ffn_g[0], "w_r": w_r.astype(BF16), "b_r": b_r,
        "e_gate": exp_w_gate[0].astype(BF16), "e_up": exp_w_up[0].astype(BF16),
        "e_down": exp_w_down[0].astype(BF16), "final_g": final_norm_g,
    }
    Bp, Tp, _ = x_prompt.shape
    Bs, Ts, _ = x_sample.shape
    past_len = cache_diff_k.shape[2]

    mkf, mvf, mkb, mvb = _memory_kv(mem_prompt, mem_norm_g[0], w_mem_k[0].astype(BF16),
                                    w_mem_v[0].astype(BF16))
    zero_state = jnp.zeros((Bp, N_HEADS, HEAD_W, HEAD_W), F32)
    yp, kp, vp, sp = _layer(x_prompt, None, None, zero_state, mkb, mvb, p, PROMPT_CFG)
    ys, ks, vs, ss = _layer(
        x_sample, cache_diff_k[0].reshape(Bs, past_len, GROUP_W), cache_diff_v[0].reshape(Bs, past_len, GROUP_W),
        state_hgrn[0], cache_mem_k[0].reshape(Bs, N_MEM, D_MODEL), cache_mem_v[0].reshape(Bs, N_MEM, D_MODEL),
        p, SAMPLE_CFG)
    mem_shape = (1, Bp, N_MEM, MEM_HEADS, MEM_HEAD_DIM)
    return (yp, ys, kp, vp, sp, mkf.reshape(mem_shape), mvf.reshape(mem_shape), ks, vs, ss)
```

```python
import functools
import math

import jax
import jax.numpy as jnp
from jax import lax
from jax.experimental import pallas as pl
from jax.experimental.pallas import tpu as pltpu

F32 = jnp.float32
BF16 = jnp.bfloat16

D_MODEL = 1024
EPS = 1e-5
CHUNK = 64
N_HEADS = 4
HEAD_W = 128
QK_DIM = 64
GROUP_W = N_HEADS * HEAD_W
N_PROJ = 7
DA_SCALE = QK_DIM ** -0.5
LOG2E = 1.4426950408889634
LAM_INIT = 0.8 - 0.6 * math.exp(-0.3 * 0)
N_MEM = 256
MEM_HEADS = 4
MEM_HEAD_DIM = D_MODEL // MEM_HEADS
MEM_SCALE = MEM_HEAD_DIM ** -0.5
N_GROUPS = 4
EXPERTS_PER_GROUP = 8
N_EXPERTS = N_GROUPS * EXPERTS_PER_GROUP
TOP_K = 2
EXPERT_FF = 512
ROUTE_W = 8
SUB = 16
SUB_KEYS = 128
NEG_INF = float("-inf")

VMEM_LIMIT = 48 * 1024 * 1024


def _cparams(sem):
    return pltpu.CompilerParams(dimension_semantics=sem, vmem_limit_bytes=VMEM_LIMIT)


def _rms(x, g):
    return x * lax.rsqrt(jnp.mean(x * x, axis=-1, keepdims=True) + EPS) * g


def _dot(a, b):
    return jnp.dot(a, b, preferred_element_type=F32)


def _dot_nt(a, b):
    return lax.dot_general(a, b, (((1,), (1,)), ((), ())), preferred_element_type=F32)


def _dot_tn(a, b):
    return lax.dot_general(a, b, (((0,), (0,)), ((), ())), preferred_element_type=F32)


def _memkv_kernel(m_ref, g_ref, wk_ref, wv_ref, kf_ref, vf_ref, kb_ref, vb_ref):
    mn = _rms(m_ref[0], g_ref[...]).astype(BF16)
    k = _dot(mn, wk_ref[...])
    v = _dot(mn, wv_ref[...])
    for h in range(MEM_HEADS):
        hs = slice(h * MEM_HEAD_DIM, (h + 1) * MEM_HEAD_DIM)
        kf_ref[0, :, h, :] = k[:, hs]
        vf_ref[0, :, h, :] = v[:, hs]
    kb_ref[0] = k.astype(BF16)
    vb_ref[0] = v.astype(BF16)


def _memory_kv(mem, g, wk, wv):
    B = mem.shape[0]
    blk = pl.BlockSpec((1, N_MEM, D_MODEL), lambda b: (b, 0, 0))
    hblk = pl.BlockSpec((1, N_MEM, MEM_HEADS, MEM_HEAD_DIM), lambda b: (b, 0, 0, 0))
    wspec = pl.BlockSpec((D_MODEL, D_MODEL), lambda b: (0, 0))
    return pl.pallas_call(
        _memkv_kernel,
        grid=(B,),
        in_specs=[blk, pl.BlockSpec((1, D_MODEL), lambda b: (0, 0)), wspec, wspec],
        out_specs=[hblk, hblk, blk, blk],
        out_shape=[jax.ShapeDtypeStruct((B, N_MEM, MEM_HEADS, MEM_HEAD_DIM), F32)] * 2
                  + [jax.ShapeDtypeStruct(mem.shape, BF16)] * 2,
        compiler_params=_cparams(("parallel",)),
        name="memory_kv",
    )(mem, g.reshape(1, D_MODEL), wk, wv)


def _in_proj_kernel(x_ref, g_ref, w_ref, lb_ref, q_ref, kf_ref, vf_ref, kb_ref, vt_ref,
                    qh_ref, lf_ref, kk_ref, hi_ref, sg_ref, *, nb, tm):
    rows = nb * tm
    h = _rms(x_ref[...].reshape(rows, D_MODEL), g_ref[...]).astype(BF16)

    def proj(i):
        return _dot(h, w_ref[:, i * GROUP_W:(i + 1) * GROUP_W])

    def put(ref, val):
        ref[...] = val.reshape(nb, tm, GROUP_W).astype(ref.dtype)

    def put_heads(ref, val):
        for b in range(nb):
            for hd in range(N_HEADS):
                ref[b, :, hd, :] = val[b * tm:(b + 1) * tm, hd * HEAD_W:(hd + 1) * HEAD_W]

    put(q_ref, proj(0) * (DA_SCALE * LOG2E))
    dk = proj(1)
    put_heads(kf_ref, dk)
    put(kb_ref, dk)
    dv = proj(2)
    put_heads(vf_ref, dv)
    for b in range(nb):
        vt_ref[b] = dv[b * tm:(b + 1) * tm].T.astype(BF16)
    hq = proj(3)
    put(qh_ref, hq * jax.nn.sigmoid(hq))
    hf = proj(4)
    lb = lb_ref[...]
    put(lf_ref, jnp.log(lb + (1.0 - lb) * jax.nn.sigmoid(hf)))
    put(kk_ref, (1.0 - lb) * jax.nn.sigmoid(-hf))
    put(hi_ref, proj(5))
    hg = proj(6)
    put(sg_ref, hg * jax.nn.sigmoid(hg))


def _in_proj(x, g, w_bf, lb, nb, tm):
    B, T, _ = x.shape
    tok = lambda w: pl.BlockSpec((nb, tm, w), lambda b, t: (b, t, 0))
    fix = lambda s: pl.BlockSpec(s, lambda b, t: (0, 0))
    f32o = jax.ShapeDtypeStruct((B, T, GROUP_W), F32)
    bfo = jax.ShapeDtypeStruct((B, T, GROUP_W), BF16)
    o_spec = tok(GROUP_W)
    ho = jax.ShapeDtypeStruct((B, T, N_HEADS, HEAD_W), F32)
    h_spec = pl.BlockSpec((nb, tm, N_HEADS, HEAD_W), lambda b, t: (b, t, 0, 0))
    vt_spec = pl.BlockSpec((nb, GROUP_W, tm), lambda b, t: (b, 0, t))
    return pl.pallas_call(
        functools.partial(_in_proj_kernel, nb=nb, tm=tm),
        grid=(B // nb, T // tm),
        in_specs=[tok(D_MODEL), fix((1, D_MODEL)), fix((D_MODEL, N_PROJ * GROUP_W)), fix((1, GROUP_W))],
        out_specs=[o_spec, h_spec, h_spec, o_spec, vt_spec] + [o_spec] * 5,
        out_shape=[bfo, ho, ho, bfo, jax.ShapeDtypeStruct((B, GROUP_W, T), BF16),
                   f32o, f32o, f32o, f32o, f32o],
        compiler_params=_cparams(("parallel", "parallel")),
        name="in_proj",
    )(x, g.reshape(1, D_MODEL), w_bf, lb.reshape(1, GROUP_W))


def _attn_kernel(*refs, tq, tk, past_len, tkp):
    if past_len:
        (slope_ref, lam_ref, q_ref, k_ref, vt_ref, pk_ref, pv_ref, g_ref, o_ref,
         m_ref, l_ref, acc_ref, base_ref, t_ref) = refs
    else:
        (slope_ref, lam_ref, q_ref, k_ref, vt_ref, g_ref, o_ref,
         m_ref, l_ref, acc_ref, base_ref, t_ref) = refs
    h = pl.program_id(1)
    qi = pl.program_id(2)
    slope2 = slope_ref[h] * LOG2E
    qt = q_ref[0].astype(F32).T.astype(BF16)
    zeros = jnp.zeros((QK_DIM, tq), BF16)
    qt_maps = (jnp.concatenate([qt[:QK_DIM], zeros], axis=0),
               jnp.concatenate([zeros, qt[QK_DIM:]], axis=0))
    m_ref[...] = jnp.full(m_ref.shape, NEG_INF, F32)
    l_ref[...] = jnp.zeros(l_ref.shape, F32)
    acc_ref[...] = jnp.zeros(acc_ref.shape, F32)
    q0 = past_len + qi * tq

    def lane_pad(x, lo, fill):
        if lo == 0:
            return x
        return jnp.concatenate([jnp.full((x.shape[0], lo), fill, x.dtype), x], axis=1)

    def update(n_keys, get_k, get_vt, get_bias, shift, q_lo=lambda u: 0):
        sub = min(n_keys, SUB_KEYS)
        n_sub = n_keys // sub
        mx = [None, None]
        for u in range(n_sub):
            lo = q_lo(u)
            bias, visible = get_bias(u)
            k = get_k(u)
            for c in range(2):
                t = _dot(k, qt_maps[c][:, lo:]) + bias
                if visible is not None:
                    w = visible.shape[1]
                    head = jnp.where(visible, t[:, :w], NEG_INF)
                    t = head if w == t.shape[1] else jnp.concatenate([head, t[:, w:]], axis=1)
                t_ref[c, u * sub:(u + 1) * sub, lo:] = t
                mu = lane_pad(jnp.max(t, axis=0, keepdims=True), lo, NEG_INF)
                mx[c] = mu if mx[c] is None else jnp.maximum(mx[c], mu)
        off, alpha = [], []
        for c in range(2):
            m_old = m_ref[c]
            m_new = jnp.maximum(m_old, mx[c] + shift)
            off.append(m_new - shift)
            alpha.append(jnp.exp2(m_old - m_new))
            m_ref[c] = m_new
        lsum = [None, None]
        pv = [None, None]
        for u in range(n_sub):
            lo = q_lo(u)
            vt = get_vt(u)
            for c in range(2):
                p = jnp.exp2(t_ref[c, u * sub:(u + 1) * sub, lo:] - off[c][:, lo:])
                ls = lane_pad(jnp.sum(p, axis=0, keepdims=True), lo, 0.0)
                pu = lane_pad(_dot(vt, p.astype(BF16)), lo, 0.0)
                lsum[c] = ls if lsum[c] is None else lsum[c] + ls
                pv[c] = pu if pv[c] is None else pv[c] + pu
        for c in range(2):
            l_ref[c] = alpha[c] * l_ref[c] + lsum[c]
            acc_ref[c] = alpha[c] * acc_ref[c] + pv[c]

    def rel_bias(n_keys, key0):
        key = lax.broadcasted_iota(jnp.int32, (n_keys, tq), 0) + key0
        qry = lax.broadcasted_iota(jnp.int32, (n_keys, tq), 1)
        return (qry - key).astype(F32) * (-slope2)

    def block_shift(k0):
        return (q0 - k0).astype(F32) * (-slope2)

    if past_len:
        sub_p = min(tkp, SUB_KEYS)

        def past_body(j, carry):
            k0 = pl.multiple_of(j * tkp, tkp)
            update(tkp,
                   lambda u: pk_ref[0, pl.ds(k0 + u * sub_p, sub_p), :].astype(BF16),
                   lambda u: pv_ref[0, pl.ds(k0 + u * sub_p, sub_p), :].T.astype(BF16),
                   lambda u: (rel_bias(sub_p, u * sub_p), None), block_shift(k0))
            return carry
        lax.fori_loop(0, past_len // tkp, past_body, 0)

    n_q = k_ref.shape[1] // tq
    if n_q > 1 or tq > SUB_KEYS:
        base_ref[...] = rel_bias(base_ref.shape[0], 0)
    if n_q > 1:
        sub_k = min(tk, SUB_KEYS)

        def prev_body(j, carry):
            k0 = pl.multiple_of(j * tk, tk)
            update(tk,
                   lambda u: k_ref[0, pl.ds(k0 + u * sub_k, sub_k), :],
                   lambda u: vt_ref[0, :, pl.ds(k0 + u * sub_k, sub_k)],
                   lambda u: (base_ref[u * sub_k:(u + 1) * sub_k, :], None),
                   block_shift(past_len + k0))
            return carry
        lax.fori_loop(0, qi * (tq // tk), prev_body, 0)

    kd = 0 if n_q == 1 else pl.multiple_of(qi * tq, tq)
    sub_d = min(tq, SUB_KEYS)
    key = lax.broadcasted_iota(jnp.int32, (sub_d, sub_d), 0)
    qry = lax.broadcasted_iota(jnp.int32, (sub_d, sub_d), 1)
    strip_bias = jnp.abs(qry - key).astype(F32) * (-slope2)
    strip_visible = (key // CHUNK) <= (qry // CHUNK)

    def diag_bias(u):
        lo = (u + 1) * sub_d
        if lo == tq:
            return strip_bias, strip_visible
        later = base_ref[u * sub_d:(u + 1) * sub_d, lo:]
        return jnp.concatenate([strip_bias, later], axis=1), strip_visible

    update(tq,
           lambda u: k_ref[0, pl.ds(kd + u * sub_d, sub_d), :],
           lambda u: vt_ref[0, :, pl.ds(kd + u * sub_d, sub_d)],
           diag_bias, 0.0, q_lo=lambda u: u * sub_d)

    lam = lam_ref[0]
    ot = acc_ref[0] / l_ref[0] - lam * (acc_ref[1] / l_ref[1])
    ot = ot * lax.rsqrt(jnp.mean(ot * ot, axis=0, keepdims=True) + EPS)
    o_ref[0] = (ot.T * g_ref[...] * (1.0 - LAM_INIT)).astype(o_ref.dtype)


def _diff_attn(q, k, vt, past_k, past_v, slopes, lam, subln_g, tq, tk):
    B, T, _ = q.shape
    past_len = 0 if past_k is None else past_k.shape[1]
    tkp = min(512, past_len) if past_len else 0
    smem = pl.BlockSpec(memory_space=pltpu.SMEM)
    qspec = pl.BlockSpec((1, tq, HEAD_W), lambda b, h, i: (b, i, h))
    in_specs = [smem, smem, qspec,
                pl.BlockSpec((1, T, HEAD_W), lambda b, h, i: (b, 0, h)),
                pl.BlockSpec((1, HEAD_W, T), lambda b, h, i: (b, h, 0))]
    args = [slopes, lam, q, k, vt]
    if past_len:
        pspec = pl.BlockSpec((1, past_len, HEAD_W), lambda b, h, i: (b, 0, h))
        in_specs += [pspec, pspec]
        args += [past_k, past_v]
    in_specs.append(pl.BlockSpec((1, HEAD_W), lambda b, h, i: (0, 0)))
    args.append(subln_g.reshape(1, HEAD_W))
    return pl.pallas_call(
        functools.partial(_attn_kernel, tq=tq, tk=tk, past_len=past_len, tkp=tkp),
        grid=(B, N_HEADS, T // tq),
        in_specs=in_specs,
        out_specs=qspec,
        out_shape=jax.ShapeDtypeStruct((B, T, GROUP_W), BF16),
        scratch_shapes=[pltpu.VMEM((2, 1, tq), F32), pltpu.VMEM((2, 1, tq), F32),
                        pltpu.VMEM((2, HEAD_W, tq), F32), pltpu.VMEM((max(tk, tq), tq), F32),
                        pltpu.VMEM((2, max(tk, tq, tkp), tq), F32)],
        compiler_params=_cparams(("parallel", "parallel", "parallel")),
        name="diff_attn",
    )(*args)


def _split3(x):
    hi = x.astype(BF16)
    r = x - hi.astype(F32)
    mid = r.astype(BF16)
    lo = (r - mid.astype(F32)).astype(BF16)
    return hi, mid, lo


def _hgrn_kernel(q_ref, k_ref, lf_ref, v_ref, sg_ref, gn_ref, s0_ref, o_ref, sfin_ref,
                 st_ref, b_ref, *, tb):
    t = pl.program_id(1)
    n_chunks = tb // CHUNK

    @pl.when(t == 0)
    def _():
        for h in range(N_HEADS):
            st_ref[h] = s0_ref[0, h].T

    row = lax.broadcasted_iota(jnp.int32, (tb, tb), 0)
    col = lax.broadcasted_iota(jnp.int32, (tb, tb), 1)
    tri = jnp.where((col <= row) & (col // CHUNK == row // CHUNK), 1.0, 0.0).astype(BF16)
    hi, mid, lo = _split3(lf_ref[0])
    b_ref[...] = _dot(tri, hi) + _dot(tri, mid) + _dot(tri, lo)

    sub_row = lax.broadcasted_iota(jnp.int32, (8, HEAD_W), 0)
    zeros16 = jnp.zeros((SUB, HEAD_W), F32)

    def pad_rows(x, r0):
        parts = []
        if r0:
            parts.append(jnp.zeros((r0, HEAD_W), F32))
        parts.append(x)
        rest = CHUNK - r0 - x.shape[0]
        if rest:
            parts.append(jnp.zeros((rest, HEAD_W), F32))
        return jnp.concatenate(parts, axis=0) if len(parts) > 1 else x

    def chunk_body(c, carry):
        r0 = pl.multiple_of(c * CHUNK, CHUNK)
        for h in range(N_HEADS):
            hs = slice(h * HEAD_W, (h + 1) * HEAD_W)
            q = q_ref[0, pl.ds(r0, CHUNK), hs]
            k = k_ref[0, pl.ds(r0, CHUNK), hs]
            v = v_ref[0, pl.ds(r0, CHUNK), hs]
            b = b_ref[pl.ds(r0, CHUNK), hs]
            b_last = b[CHUNK - 1:CHUNK, :]
            st = st_ref[h]
            v_bf = v.astype(BF16)

            o = _dot_nt((q * jnp.exp(b)).astype(BF16), st.astype(BF16))

            q_segs, k_segs = [], []
            for i in range(1, CHUNK // SUB):
                lo_r, hi_r = i * SUB, (i + 1) * SUB
                ref_b = b[lo_r - 1:lo_r, :]
                q_segs.append(pad_rows(q[lo_r:hi_r] * jnp.exp(b[lo_r:hi_r] - ref_b), lo_r))
                k_segs.append(pad_rows(k[:lo_r] * jnp.exp(ref_b - b[:lo_r]), 0))
            a_off = _dot_nt(jnp.concatenate(q_segs, axis=1).astype(BF16),
                            jnp.concatenate(k_segs, axis=1).astype(BF16))
            o = o + _dot(a_off.astype(BF16), v_bf)

            diag = []
            for blk in range(CHUNK // SUB):
                base = blk * SUB
                qa, qb = q[base:base + 8], q[base + 8:base + SUB]
                ba, bb = b[base:base + 8], b[base + 8:base + SUB]
                oa = jnp.zeros((8, HEAD_W), F32)
                ob = jnp.zeros((8, HEAD_W), F32)
                for s in range(SUB):
                    ks = k[base + s:base + s + 1]
                    bs = b[base + s:base + s + 1]
                    vs = v[base + s:base + s + 1]
                    if s < 8:
                        e = jnp.exp(ba - bs)
                        if s:
                            e = jnp.where(sub_row >= s, e, 0.0)
                        oa = oa + jnp.sum(qa * ks * e, axis=-1, keepdims=True) * vs
                        ob = ob + jnp.sum(qb * ks * jnp.exp(bb - bs), axis=-1, keepdims=True) * vs
                    else:
                        e = jnp.exp(bb - bs)
                        if s > 8:
                            e = jnp.where(sub_row >= s - 8, e, 0.0)
                        ob = ob + jnp.sum(qb * ks * e, axis=-1, keepdims=True) * vs
                diag += [oa, ob]
            o = o + jnp.concatenate(diag, axis=0)

            k_dec = (k * jnp.exp(b_last - b)).astype(BF16)
            st_ref[h] = st * jnp.exp(b_last) + _dot_tn(v_bf, k_dec)

            out = _rms(o, gn_ref[...]) * sg_ref[0, pl.ds(r0, CHUNK), hs]
            o_ref[0, pl.ds(r0, CHUNK), hs] = out.astype(o_ref.dtype)
        return carry

    lax.fori_loop(0, n_chunks, chunk_body, 0)

    @pl.when(t == pl.num_programs(1) - 1)
    def _():
        for h in range(N_HEADS):
            sfin_ref[0, h] = st_ref[h].T


def _hgrn(qh, kk, lf, hi, sg, gnorm_g, s0, tb):
    B, T, _ = qh.shape
    tspec = pl.BlockSpec((1, tb, GROUP_W), lambda b, t: (b, t, 0))
    sspec = pl.BlockSpec((1, N_HEADS, HEAD_W, HEAD_W), lambda b, t: (b, 0, 0, 0))
    return pl.pallas_call(
        functools.partial(_hgrn_kernel, tb=tb),
        grid=(B, T // tb),
        in_specs=[tspec] * 5 + [pl.BlockSpec((1, HEAD_W), lambda b, t: (0, 0)), sspec],
        out_specs=[tspec, sspec],
        out_shape=[jax.ShapeDtypeStruct((B, T, GROUP_W), BF16),
                   jax.ShapeDtypeStruct((B, N_HEADS, HEAD_W, HEAD_W), F32)],
        scratch_shapes=[pltpu.VMEM((N_HEADS, HEAD_W, HEAD_W), F32), pltpu.VMEM((tb, GROUP_W), F32)],
        compiler_params=_cparams(("parallel", "arbitrary")),
        name="hgrn",
    )(qh, kk, lf, hi, sg, gnorm_g.reshape(1, HEAD_W), s0)


def _post_kernel(x_ref, oda_ref, ohg_ref, mk_ref, mv_ref, wout_ref, lnm_ref, wq_ref, wo_ref,
                 lnf_ref, wr_ref, br_ref, x2_ref, h3_ref, route_ref, om_ref, *, nb, tm):
    rows = nb * tm
    x = x_ref[...].reshape(rows, D_MODEL)
    mixed = (_dot(oda_ref[...].reshape(rows, GROUP_W), wout_ref[:GROUP_W, :])
             + _dot(ohg_ref[...].reshape(rows, GROUP_W), wout_ref[GROUP_W:, :]))
    x1 = x + mixed

    qm = (_dot(_rms(x1, lnm_ref[...]).astype(BF16), wq_ref[...]) * MEM_SCALE).astype(BF16)
    for b in range(nb):
        for h in range(MEM_HEADS):
            hs = slice(h * MEM_HEAD_DIM, (h + 1) * MEM_HEAD_DIM)
            if len(mk_ref.shape) == 4:
                mk, mv = mk_ref[b, :, h, :], mv_ref[b, :, h, :]
            else:
                mk, mv = mk_ref[b, :, hs], mv_ref[b, :, hs]
            s = _dot_nt(qm[b * tm:(b + 1) * tm, hs], mk.astype(BF16))
            e = jnp.exp(s - jnp.max(s, axis=-1, keepdims=True))
            p = e / jnp.sum(e, axis=-1, keepdims=True)
            om_ref[b * tm:(b + 1) * tm, hs] = _dot(p.astype(BF16), mv.astype(BF16)).astype(BF16)
    x2 = x1 + _dot(om_ref[...], wo_ref[...])
    x2_ref[...] = x2.reshape(nb, tm, D_MODEL)

    h3 = _rms(x2, lnf_ref[...])
    h3_ref[...] = h3.reshape(nb, tm, D_MODEL)

    r = _dot(h3.astype(BF16), wr_ref[...]) + br_ref[...]
    lane = lax.broadcasted_iota(jnp.int32, r.shape, 1).astype(F32)
    big = float(4 * HEAD_W)
    g_mask = lane < N_GROUPS
    gl = jnp.where(g_mask, r, NEG_INF)
    g_max = jnp.max(gl, axis=-1, keepdims=True)
    g_idx = jnp.min(jnp.where(gl == g_max, lane, big), axis=-1, keepdims=True)
    g_w = 1.0 / jnp.sum(jnp.where(g_mask, jnp.exp(r - g_max), 0.0), axis=-1, keepdims=True)
    e_lo = N_GROUPS + EXPERTS_PER_GROUP * g_idx
    el = jnp.where((lane >= e_lo) & (lane < e_lo + EXPERTS_PER_GROUP), r, NEG_INF)
    v1 = jnp.max(el, axis=-1, keepdims=True)
    i1 = jnp.min(jnp.where(el == v1, lane, big), axis=-1, keepdims=True)
    el2 = jnp.where(lane == i1, NEG_INF, el)
    v2 = jnp.max(el2, axis=-1, keepdims=True)
    i2 = jnp.min(jnp.where(el2 == v2, lane, big), axis=-1, keepdims=True)
    t = jnp.exp(v2 - v1)
    p1 = 1.0 / (1.0 + t)
    rec = jnp.where(lane == 0.0, i1 - N_GROUPS,
          jnp.where(lane == 1.0, i2 - N_GROUPS,
          jnp.where(lane == 2.0, p1 * g_w,
          jnp.where(lane == 3.0, t * p1 * g_w, 0.0))))
    route_ref[...] = rec[:, :ROUTE_W].reshape(nb, tm, ROUTE_W)


def _post_mix(x, oda, ohg, mem_k, mem_v, w_out, ln_mem, w_q, w_o, ln_ffn, w_r, b_r, nb, tm):
    B, T, _ = x.shape
    tok = lambda w: pl.BlockSpec((nb, tm, w), lambda b, t: (b, t, 0))
    if mem_k.ndim == 4:
        memspec = pl.BlockSpec((nb, N_MEM, MEM_HEADS, MEM_HEAD_DIM), lambda b, t: (b, 0, 0, 0))
    else:
        memspec = pl.BlockSpec((nb, N_MEM, D_MODEL), lambda b, t: (b, 0, 0))
    fix = lambda s: pl.BlockSpec(s, lambda b, t: (0, 0))
    return pl.pallas_call(
        functools.partial(_post_kernel, nb=nb, tm=tm),
        grid=(B // nb, T // tm),
        in_specs=[tok(D_MODEL), tok(GROUP_W), tok(GROUP_W), memspec, memspec,
                  fix((D_MODEL, D_MODEL)), fix((1, D_MODEL)), fix((D_MODEL, D_MODEL)),
                  fix((D_MODEL, D_MODEL)), fix((1, D_MODEL)), fix((D_MODEL, HEAD_W)),
                  fix((1, HEAD_W))],
        out_specs=[tok(D_MODEL), tok(D_MODEL), tok(ROUTE_W)],
        out_shape=[jax.ShapeDtypeStruct((B, T, D_MODEL), F32), jax.ShapeDtypeStruct((B, T, D_MODEL), F32),
                   jax.ShapeDtypeStruct((B, T, ROUTE_W), F32)],
        scratch_shapes=[pltpu.VMEM((nb * tm, D_MODEL), BF16)],
        compiler_params=_cparams(("parallel", "parallel")),
        name="post_mix",
    )(x, oda, ohg, mem_k, mem_v, w_out, ln_mem.reshape(1, D_MODEL), w_q, w_o,
      ln_ffn.reshape(1, D_MODEL), w_r, b_r)


def _experts_kernel(blk_e_ref, n_used_ref, src_next_ref, dst_prev_ref, src0_ref, dst_ref, h_hbm,
                    wg_ref, wu_ref, wd_ref, y_hbm, xbuf, ybuf, gsem, ssem, *, bm, m_tot):
    del blk_e_ref
    i = pl.program_id(0)
    n_used = n_used_ref[0]
    slot = i % 2

    def gather_row(tab_ref, r, s, priority=0):
        pltpu.make_async_copy(h_hbm.at[pl.ds(tab_ref[0, 0, r], 1), :],
                              xbuf.at[s, pl.ds(r, 1), :], gsem.at[s]).start(priority=priority)

    def scatter_row(tab_ref, r, s, priority=0):
        pltpu.make_async_copy(ybuf.at[s, pl.ds(r, 1), :],
                              y_hbm.at[pl.ds(tab_ref[0, 0, r], 1), :], ssem.at[s]).start(priority=priority)

    def wait_block(buf, sem, s):
        pltpu.make_async_copy(buf.at[s], buf.at[s], sem.at[s]).wait()

    def spare_fill(s):
        return pltpu.make_async_copy(ybuf.at[s], y_hbm.at[pl.ds(m_tot + s * bm, bm), :], ssem.at[s])

    @pl.when(i == 0)
    def _():
        ybuf[...] = jnp.zeros(ybuf.shape, F32)
        spare_fill(0).start()
        spare_fill(1).start()
        spare_fill(1).wait()

        def body(r, c):
            gather_row(src0_ref, r, 0)
            return c
        lax.fori_loop(0, bm, body, 0)

    @pl.when(i < n_used)
    def _():
        wait_block(xbuf, gsem, slot)
        wait_block(ybuf, ssem, slot)
        x = xbuf[slot].astype(BF16)
        hg = _dot(x, wg_ref[0])
        hu = _dot(x, wu_ref[0])
        hb = (hg * jax.nn.sigmoid(hg) * hu).astype(BF16)
        ybuf[slot] = _dot(hb, wd_ref[0])
        for r in range(bm):
            gather_row(src_next_ref, r, 1 - slot, priority=r % 2)
            scatter_row(dst_prev_ref, r, 1 - slot, priority=(r + 1) % 2)

    @pl.when(i == n_used - 1)
    def _():
        def body(r, c):
            scatter_row(dst_ref, r, slot)
            return c
        lax.fori_loop(0, bm, body, 0)
        wait_block(xbuf, gsem, 1 - slot)
        wait_block(ybuf, ssem, 1 - slot)
        wait_block(ybuf, ssem, slot)


def _experts(h3, blk_e, n_used, src_tab, dst_tab, dst_prev_tab, wg, wu, wd, bm):
    n_blocks = src_tab.shape[0]
    m_tot = TOP_K * h3.shape[0]
    tab = lambda f: pl.BlockSpec((1, 1, bm), f, memory_space=pltpu.SMEM)
    cur = lambda i, e, n: (i, 0, 0)
    nxt = lambda i, e, n: (jnp.minimum(i + 1, n_blocks - 1), 0, 0)
    first = lambda i, e, n: (0, 0, 0)
    wspec = lambda a, b: pl.BlockSpec((1, a, b), lambda i, e, n: (e[i], 0, 0))
    grid_spec = pltpu.PrefetchScalarGridSpec(
        num_scalar_prefetch=2,
        grid=(n_blocks,),
        in_specs=[tab(nxt), tab(cur), tab(first), tab(cur), pl.BlockSpec(memory_space=pl.ANY),
                  wspec(D_MODEL, EXPERT_FF), wspec(D_MODEL, EXPERT_FF), wspec(EXPERT_FF, D_MODEL)],
        out_specs=pl.BlockSpec(memory_space=pl.ANY),
        scratch_shapes=[pltpu.VMEM((2, bm, D_MODEL), F32), pltpu.VMEM((2, bm, D_MODEL), F32),
                        pltpu.SemaphoreType.DMA((2,)), pltpu.SemaphoreType.DMA((2,))],
    )
    return pl.pallas_call(
        functools.partial(_experts_kernel, bm=bm, m_tot=m_tot),
        grid_spec=grid_spec,
        out_shape=jax.ShapeDtypeStruct((m_tot + 2 * bm, D_MODEL), F32),
        compiler_params=_cparams(("arbitrary",)),
        name="experts",
    )(blk_e, n_used, src_tab, dst_prev_tab, src_tab, dst_tab, h3, wg, wu, wd)


def _combine_kernel(x_ref, y0_ref, y1_ref, route_ref, g_ref, o_ref):
    rt = route_ref[...]
    y = x_ref[...] + rt[:, 2:3] * y0_ref[...] + rt[:, 3:4] * y1_ref[...]
    o_ref[...] = _rms(y, g_ref[...])


def _combine(x2, y, route, final_g, tm):
    n = x2.shape[0]
    nt = n // tm
    return pl.pallas_call(
        _combine_kernel,
        grid=(nt,),
        in_specs=[pl.BlockSpec((tm, D_MODEL), lambda i: (i, 0)),
                  pl.BlockSpec((tm, D_MODEL), lambda i: (i, 0)),
                  pl.BlockSpec((tm, D_MODEL), lambda i: (i + nt, 0)),
                  pl.BlockSpec((tm, ROUTE_W), lambda i: (i, 0)),
                  pl.BlockSpec((1, D_MODEL), lambda i: (0, 0))],
        out_specs=pl.BlockSpec((tm, D_MODEL), lambda i: (i, 0)),
        out_shape=jax.ShapeDtypeStruct((n, D_MODEL), F32),
        compiler_params=_cparams(("parallel",)),
        name="combine",
    )(x2, y, y, route, final_g.reshape(1, D_MODEL))


def _routing_tables(route, bm):
    n = route.shape[0]
    m_tot = TOP_K * n
    flat_e = jnp.concatenate([route[:, 0], route[:, 1]]).astype(jnp.int32)
    order = jnp.argsort(flat_e).astype(jnp.int32)
    counts = jnp.sum(flat_e[:, None] == jnp.arange(N_EXPERTS, dtype=jnp.int32)[None, :], axis=0,
                     dtype=jnp.int32)
    n_blk_e = (counts + bm - 1) // bm
    blk_end = jnp.cumsum(n_blk_e)
    start = jnp.cumsum(counts) - counts
    n_blocks = m_tot // bm + N_EXPERTS
    blk = jnp.arange(n_blocks, dtype=jnp.int32)
    blk_e = jnp.minimum(jnp.sum(blk[:, None] >= blk_end[None, :], axis=1, dtype=jnp.int32),
                        N_EXPERTS - 1)
    blk_off = (blk - (blk_end[blk_e] - n_blk_e[blk_e])) * bm
    n_real = jnp.clip(counts[blk_e] - blk_off, 0, bm)
    row0 = jnp.minimum(start[blk_e] + blk_off, m_tot)
    order_pad = jnp.concatenate([order, jnp.zeros((bm,), jnp.int32)])
    dnums = lax.GatherDimensionNumbers(offset_dims=(1,), collapsed_slice_dims=(), start_index_map=(0,))
    m = lax.gather(order_pad, row0[:, None], dnums, slice_sizes=(bm,),
                   mode=lax.GatherScatterMode.PROMISE_IN_BOUNDS)
    r = jnp.arange(bm, dtype=jnp.int32)[None, :]
    real = r < n_real[:, None]
    spare = m_tot + (blk[:, None] % 2) * bm + r
    src = jnp.where(real, m % n, 0)
    dst = jnp.where(real, m, spare)
    dst_prev = jnp.concatenate([m_tot + bm + r, dst[:-1]], axis=0)
    shape = (n_blocks, 1, bm)
    return (blk_e, blk_end[N_EXPERTS - 1:].astype(jnp.int32), src.reshape(shape), dst.reshape(shape),
            dst_prev.reshape(shape))


def _layer(x, past_k, past_v, s0, mem_k, mem_v, p, cfg):
    B, T, _ = x.shape
    n = B * T
    (q, kf, vf, kb, vt, qh, lf, kk, hi, sg) = _in_proj(
        x, p["ln_mix"], p["w_in"], p["lb"], cfg["nb"], cfg["tm_post"])
    oda = _diff_attn(q, kb, vt, past_k, past_v, p["slopes"], p["lam"], p["da_subln"],
                     cfg["tq"], cfg["tk"])
    ohg, s_new = _hgrn(qh, kk, lf, hi, sg, p["hg_gnorm"], s0, cfg["tb"])
    x2, h3, route = _post_mix(x, oda, ohg, mem_k, mem_v, p["w_out"], p["ln_mem"], p["w_mem_q"],
                              p["w_mem_o"], p["ln_ffn"], p["w_r"], p["b_r"], cfg["nb"], cfg["tm_post"])
    route2 = route.reshape(n, ROUTE_W)
    blk_e, n_used, src_tab, dst_tab, dst_prev_tab = _routing_tables(route2, cfg["bm"])
    y = _experts(h3.reshape(n, D_MODEL), blk_e, n_used, src_tab, dst_tab, dst_prev_tab,
                 p["e_gate"], p["e_up"], p["e_down"], cfg["bm"])
    out = _combine(x2.reshape(n, D_MODEL), y, route2, p["final_g"], cfg["tm_in"])
    return out.reshape(B, T, D_MODEL), kf[None], vf[None], s_new[None]


PROMPT_CFG = dict(tm_in=256, tq=512, tk=512, tb=256, nb=1, tm_post=256, bm=512)
SAMPLE_CFG = dict(tm_in=256, tq=64, tk=64, tb=64, nb=4, tm_post=64, bm=128)


def kernel(x_prompt, x_sample, mem_prompt, cache_diff_k, cache_diff_v, state_hgrn, cache_mem_k, cache_mem_v, ln_mix_g, w_in, da_lambda, da_subln_g, hg_lb_logits, hg_gnorm_g, w_out, ln_mem_g, mem_norm_g, w_mem_q, w_mem_k, w_mem_v, w_mem_o, ln_ffn_g, router_group_w, router_group_b, router_expert_w, router_expert_b, exp_w_gate, exp_w_up, exp_w_down, final_norm_g):
    assert w_in.shape[0] == 1, "single-layer configuration"
    lb_all = jnp.cumsum(jax.nn.softmax(hg_lb_logits.astype(F32), axis=0), axis=0)
    lp = da_lambda[0].astype(F32)
    lam = jnp.exp(jnp.sum(lp[0] * lp[1])) - jnp.exp(jnp.sum(lp[2] * lp[3])) + LAM_INIT
    w_r = jnp.zeros((D_MODEL, HEAD_W), F32)
    w_r = w_r.at[:, :N_GROUPS].set(router_group_w[0]).at[:, N_GROUPS:N_GROUPS + N_EXPERTS].set(router_expert_w[0])
    b_r = jnp.zeros((1, HEAD_W), F32)
    b_r = b_r.at[0, :N_GROUPS].set(router_group_b[0]).at[0, N_GROUPS:N_GROUPS + N_EXPERTS].set(router_expert_b[0])
    p = {
        "ln_mix": ln_mix_g[0], "w_in": w_in[0].astype(BF16), "lb": lb_all[0],
        "slopes": jnp.exp2(-8.0 * jnp.arange(1, N_HEADS + 1, dtype=F32) / N_HEADS),
        "lam": lam.reshape(1), "da_subln": da_subln_g[0], "hg_gnorm": hg_gnorm_g[0],
        "w_out": w_out[0].astype(BF16), "ln_mem": ln_mem_g[0], "w_mem_q": w_mem_q[0].astype(BF16),
        "w_mem_o": w_mem_o[0].astype(BF16), "ln_ffn": ln_ffn_g[0], "w_r": w_r.astype(BF16), "b_r": b_r,
        "e_gate": exp_w_gate[0].astype(BF16), "e_up": exp_w_up[0].astype(BF16),
        "e_down": exp_w_down[0].astype(BF16), "final_g": final_norm_g,
    }
    Bp, Tp, _ = x_prompt.shape
    Bs, Ts, _ = x_sample.shape
    past_len = cache_diff_k.shape[2]

    mkf, mvf, mkb, mvb = _memory_kv(mem_prompt, mem_norm_g[0], w_mem_k[0].astype(BF16),
                                    w_mem_v[0].astype(BF16))
    zero_state = jnp.zeros((Bp, N_HEADS, HEAD_W, HEAD_W), F32)
    yp, kp, vp, sp = _layer(x_prompt, None, None, zero_state, mkb, mvb, p, PROMPT_CFG)
    ys, ks, vs, ss = _layer(
        x_sample, cache_diff_k[0].reshape(Bs, past_len, GROUP_W), cache_diff_v[0].reshape(Bs, past_len, GROUP_W),
        state_hgrn[0], cache_mem_k[0], cache_mem_v[0], p, SAMPLE_CFG)
    return (yp, ys, kp, vp, sp, mkf[None], mvf[None], ks, vs, ss)
```

```python
import functools
import math

import jax
import jax.numpy as jnp
from jax import lax
from jax.experimental import pallas as pl
from jax.experimental.pallas import tpu as pltpu

F32 = jnp.float32
BF16 = jnp.bfloat16

D_MODEL = 1024
EPS = 1e-5
CHUNK = 64
N_HEADS = 4
HEAD_W = 128
QK_DIM = 64
GROUP_W = N_HEADS * HEAD_W
N_PROJ = 7
DA_SCALE = QK_DIM ** -0.5
LOG2E = 1.4426950408889634
LAM_INIT = 0.8 - 0.6 * math.exp(-0.3 * 0)
N_MEM = 256
MEM_HEADS = 4
MEM_HEAD_DIM = D_MODEL // MEM_HEADS
MEM_SCALE = MEM_HEAD_DIM ** -0.5
N_GROUPS = 4
EXPERTS_PER_GROUP = 8
N_EXPERTS = N_GROUPS * EXPERTS_PER_GROUP
TOP_K = 2
EXPERT_FF = 512
ROUTE_W = 8
SUB = 16
SUB_KEYS = 128
NEG_INF = float("-inf")

VMEM_LIMIT = 48 * 1024 * 1024


def _cparams(sem):
    return pltpu.CompilerParams(dimension_semantics=sem, vmem_limit_bytes=VMEM_LIMIT)


def _rms(x, g):
    return x * lax.rsqrt(jnp.mean(x * x, axis=-1, keepdims=True) + EPS) * g


def _dot(a, b):
    return jnp.dot(a, b, preferred_element_type=F32)


def _dot_nt(a, b):
    return lax.dot_general(a, b, (((1,), (1,)), ((), ())), preferred_element_type=F32)


def _dot_tn(a, b):
    return lax.dot_general(a, b, (((0,), (0,)), ((), ())), preferred_element_type=F32)


def _memkv_kernel(m_ref, g_ref, wk_ref, wv_ref, kf_ref, vf_ref, kb_ref, vb_ref):
    mn = _rms(m_ref[0], g_ref[...]).astype(BF16)
    k = _dot(mn, wk_ref[...])
    v = _dot(mn, wv_ref[...])
    for h in range(MEM_HEADS):
        hs = slice(h * MEM_HEAD_DIM, (h + 1) * MEM_HEAD_DIM)
        kf_ref[0, :, h, :] = k[:, hs]
        vf_ref[0, :, h, :] = v[:, hs]
    kb_ref[0] = k.astype(BF16)
    vb_ref[0] = v.astype(BF16)


def _memory_kv(mem, g, wk, wv):
    B = mem.shape[0]
    blk = pl.BlockSpec((1, N_MEM, D_MODEL), lambda b: (b, 0, 0))
    hblk = pl.BlockSpec((1, N_MEM, MEM_HEADS, MEM_HEAD_DIM), lambda b: (b, 0, 0, 0))
    wspec = pl.BlockSpec((D_MODEL, D_MODEL), lambda b: (0, 0))
    return pl.pallas_call(
        _memkv_kernel,
        grid=(B,),
        in_specs=[blk, pl.BlockSpec((1, D_MODEL), lambda b: (0, 0)), wspec, wspec],
        out_specs=[hblk, hblk, blk, blk],
        out_shape=[jax.ShapeDtypeStruct((B, N_MEM, MEM_HEADS, MEM_HEAD_DIM), F32)] * 2
                  + [jax.ShapeDtypeStruct(mem.shape, BF16)] * 2,
        compiler_params=_cparams(("parallel",)),
        name="memory_kv",
    )(mem, g.reshape(1, D_MODEL), wk, wv)


def _in_proj_kernel(x_ref, g_ref, w_ref, lb_ref, q_ref, kf_ref, vf_ref, kb_ref, vt_ref,
                    qh_ref, lf_ref, kk_ref, hi_ref, sg_ref, *, nb, tm):
    rows = nb * tm
    h = _rms(x_ref[...].reshape(rows, D_MODEL), g_ref[...]).astype(BF16)

    def proj(i):
        return _dot(h, w_ref[:, i * GROUP_W:(i + 1) * GROUP_W])

    def put(ref, val):
        ref[...] = val.reshape(nb, tm, GROUP_W).astype(ref.dtype)

    def put_heads(ref, val):
        for b in range(nb):
            for hd in range(N_HEADS):
                ref[b, :, hd, :] = val[b * tm:(b + 1) * tm, hd * HEAD_W:(hd + 1) * HEAD_W]

    put(q_ref, proj(0) * (DA_SCALE * LOG2E))
    dk = proj(1)
    put_heads(kf_ref, dk)
    put(kb_ref, dk)
    dv = proj(2)
    put_heads(vf_ref, dv)
    for b in range(nb):
        vt_ref[b] = dv[b * tm:(b + 1) * tm].T.astype(BF16)
    hq = proj(3)
    put(qh_ref, hq * jax.nn.sigmoid(hq))
    hf = proj(4)
    lb = lb_ref[...]
    put(lf_ref, jnp.log(lb + (1.0 - lb) * jax.nn.sigmoid(hf)))
    put(kk_ref, (1.0 - lb) * jax.nn.sigmoid(-hf))
    put(hi_ref, proj(5))
    hg = proj(6)
    put(sg_ref, hg * jax.nn.sigmoid(hg))


def _in_proj(x, g, w_bf, lb, nb, tm):
    B, T, _ = x.shape
    tok = lambda w: pl.BlockSpec((nb, tm, w), lambda b, t: (b, t, 0))
    fix = lambda s: pl.BlockSpec(s, lambda b, t: (0, 0))
    f32o = jax.ShapeDtypeStruct((B, T, GROUP_W), F32)
    bfo = jax.ShapeDtypeStruct((B, T, GROUP_W), BF16)
    o_spec = tok(GROUP_W)
    ho = jax.ShapeDtypeStruct((B, T, N_HEADS, HEAD_W), F32)
    h_spec = pl.BlockSpec((nb, tm, N_HEADS, HEAD_W), lambda b, t: (b, t, 0, 0))
    vt_spec = pl.BlockSpec((nb, GROUP_W, tm), lambda b, t: (b, 0, t))
    return pl.pallas_call(
        functools.partial(_in_proj_kernel, nb=nb, tm=tm),
        grid=(B // nb, T // tm),
        in_specs=[tok(D_MODEL), fix((1, D_MODEL)), fix((D_MODEL, N_PROJ * GROUP_W)), fix((1, GROUP_W))],
        out_specs=[o_spec, h_spec, h_spec, o_spec, vt_spec] + [o_spec] * 5,
        out_shape=[bfo, ho, ho, bfo, jax.ShapeDtypeStruct((B, GROUP_W, T), BF16),
                   bfo, f32o, bfo, bfo, bfo],
        compiler_params=_cparams(("parallel", "parallel")),
        name="in_proj",
    )(x, g.reshape(1, D_MODEL), w_bf, lb.reshape(1, GROUP_W))


def _attn_kernel(*refs, tq, tk, past_len, tkp):
    if past_len:
        (slope_ref, lam_ref, q_ref, k_ref, vt_ref, pk_ref, pv_ref, g_ref, o_ref,
         m_ref, l_ref, acc_ref, base_ref, t_ref) = refs
    else:
        (slope_ref, lam_ref, q_ref, k_ref, vt_ref, g_ref, o_ref,
         m_ref, l_ref, acc_ref, base_ref, t_ref) = refs
    h = pl.program_id(1)
    qi = pl.program_id(2)
    slope2 = slope_ref[h] * LOG2E
    qt = q_ref[0].astype(F32).T.astype(BF16)
    zeros = jnp.zeros((QK_DIM, tq), BF16)
    qt_maps = (jnp.concatenate([qt[:QK_DIM], zeros], axis=0),
               jnp.concatenate([zeros, qt[QK_DIM:]], axis=0))
    m_ref[...] = jnp.full(m_ref.shape, NEG_INF, F32)
    l_ref[...] = jnp.zeros(l_ref.shape, F32)
    acc_ref[...] = jnp.zeros(acc_ref.shape, F32)
    q0 = past_len + qi * tq

    def lane_pad(x, lo, fill):
        if lo == 0:
            return x
        return jnp.concatenate([jnp.full((x.shape[0], lo), fill, x.dtype), x], axis=1)

    def update(n_keys, get_k, get_vt, get_bias, shift, q_lo=lambda u: 0):
        sub = min(n_keys, SUB_KEYS)
        n_sub = n_keys // sub
        mx = [None, None]
        for u in range(n_sub):
            lo = q_lo(u)
            bias, visible = get_bias(u)
            k = get_k(u)
            for c in range(2):
                t = _dot(k, qt_maps[c][:, lo:]) + bias
                if visible is not None:
                    w = visible.shape[1]
                    head = jnp.where(visible, t[:, :w], NEG_INF)
                    t = head if w == t.shape[1] else jnp.concatenate([head, t[:, w:]], axis=1)
                t_ref[c, u * sub:(u + 1) * sub, lo:] = t
                mu = lane_pad(jnp.max(t, axis=0, keepdims=True), lo, NEG_INF)
                mx[c] = mu if mx[c] is None else jnp.maximum(mx[c], mu)
        off, alpha = [], []
        for c in range(2):
            m_old = m_ref[c]
            m_new = jnp.maximum(m_old, mx[c] + shift)
            off.append(m_new - shift)
            alpha.append(jnp.exp2(m_old - m_new))
            m_ref[c] = m_new
        lsum = [None, None]
        pv = [None, None]
        for u in range(n_sub):
            lo = q_lo(u)
            vt = get_vt(u)
            for c in range(2):
                p = jnp.exp2(t_ref[c, u * sub:(u + 1) * sub, lo:] - off[c][:, lo:])
                ls = lane_pad(jnp.sum(p, axis=0, keepdims=True), lo, 0.0)
                pu = lane_pad(_dot(vt, p.astype(BF16)), lo, 0.0)
                lsum[c] = ls if lsum[c] is None else lsum[c] + ls
                pv[c] = pu if pv[c] is None else pv[c] + pu
        for c in range(2):
            l_ref[c] = alpha[c] * l_ref[c] + lsum[c]
            acc_ref[c] = alpha[c] * acc_ref[c] + pv[c]

    def rel_bias(n_keys, key0):
        key = lax.broadcasted_iota(jnp.int32, (n_keys, tq), 0) + key0
        qry = lax.broadcasted_iota(jnp.int32, (n_keys, tq), 1)
        return (qry - key).astype(F32) * (-slope2)

    def block_shift(k0):
        return (q0 - k0).astype(F32) * (-slope2)

    if past_len:
        sub_p = min(tkp, SUB_KEYS)

        def past_body(j, carry):
            k0 = pl.multiple_of(j * tkp, tkp)
            update(tkp,
                   lambda u: pk_ref[0, pl.ds(k0 + u * sub_p, sub_p), :].astype(BF16),
                   lambda u: pv_ref[0, pl.ds(k0 + u * sub_p, sub_p), :].T.astype(BF16),
                   lambda u: (rel_bias(sub_p, u * sub_p), None), block_shift(k0))
            return carry
        lax.fori_loop(0, past_len // tkp, past_body, 0)

    n_q = k_ref.shape[1] // tq
    if n_q > 1 or tq > SUB_KEYS:
        base_ref[...] = rel_bias(base_ref.shape[0], 0)
    if n_q > 1:
        sub_k = min(tk, SUB_KEYS)

        def prev_body(j, carry):
            k0 = pl.multiple_of(j * tk, tk)
            update(tk,
                   lambda u: k_ref[0, pl.ds(k0 + u * sub_k, sub_k), :],
                   lambda u: vt_ref[0, :, pl.ds(k0 + u * sub_k, sub_k)],
                   lambda u: (base_ref[u * sub_k:(u + 1) * sub_k, :], None),
                   block_shift(past_len + k0))
            return carry
        lax.fori_loop(0, qi * (tq // tk), prev_body, 0)

    kd = 0 if n_q == 1 else pl.multiple_of(qi * tq, tq)
    sub_d = min(tq, SUB_KEYS)
    key = lax.broadcasted_iota(jnp.int32, (sub_d, sub_d), 0)
    qry = lax.broadcasted_iota(jnp.int32, (sub_d, sub_d), 1)
    strip_bias = jnp.abs(qry - key).astype(F32) * (-slope2)
    strip_visible = (key // CHUNK) <= (qry // CHUNK)

    def diag_bias(u):
        lo = (u + 1) * sub_d
        if lo == tq:
            return strip_bias, strip_visible
        later = base_ref[u * sub_d:(u + 1) * sub_d, lo:]
        return jnp.concatenate([strip_bias, later], axis=1), strip_visible

    update(tq,
           lambda u: k_ref[0, pl.ds(kd + u * sub_d, sub_d), :],
           lambda u: vt_ref[0, :, pl.ds(kd + u * sub_d, sub_d)],
           diag_bias, 0.0, q_lo=lambda u: u * sub_d)

    lam = lam_ref[0]
    ot = acc_ref[0] / l_ref[0] - lam * (acc_ref[1] / l_ref[1])
    ot = ot * lax.rsqrt(jnp.mean(ot * ot, axis=0, keepdims=True) + EPS)
    o_ref[0] = (ot.T * g_ref[...] * (1.0 - LAM_INIT)).astype(o_ref.dtype)


def _diff_attn(q, k, vt, past_k, past_v, slopes, lam, subln_g, tq, tk):
    B, T, _ = q.shape
    past_len = 0 if past_k is None else past_k.shape[1]
    tkp = min(512, past_len) if past_len else 0
    smem = pl.BlockSpec(memory_space=pltpu.SMEM)
    qspec = pl.BlockSpec((1, tq, HEAD_W), lambda b, h, i: (b, i, h))
    in_specs = [smem, smem, qspec,
                pl.BlockSpec((1, T, HEAD_W), lambda b, h, i: (b, 0, h)),
                pl.BlockSpec((1, HEAD_W, T), lambda b, h, i: (b, h, 0))]
    args = [slopes, lam, q, k, vt]
    if past_len:
        pspec = pl.BlockSpec((1, past_len, HEAD_W), lambda b, h, i: (b, 0, h))
        in_specs += [pspec, pspec]
        args += [past_k, past_v]
    in_specs.append(pl.BlockSpec((1, HEAD_W), lambda b, h, i: (0, 0)))
    args.append(subln_g.reshape(1, HEAD_W))
    return pl.pallas_call(
        functools.partial(_attn_kernel, tq=tq, tk=tk, past_len=past_len, tkp=tkp),
        grid=(B, N_HEADS, T // tq),
        in_specs=in_specs,
        out_specs=qspec,
        out_shape=jax.ShapeDtypeStruct((B, T, GROUP_W), BF16),
        scratch_shapes=[pltpu.VMEM((2, 1, tq), F32), pltpu.VMEM((2, 1, tq), F32),
                        pltpu.VMEM((2, HEAD_W, tq), F32), pltpu.VMEM((max(tk, tq), tq), F32),
                        pltpu.VMEM((2, max(tk, tq, tkp), tq), F32)],
        compiler_params=_cparams(("parallel", "parallel", "parallel")),
        name="diff_attn",
    )(*args)


def _split3(x):
    hi = x.astype(BF16)
    r = x - hi.astype(F32)
    mid = r.astype(BF16)
    lo = (r - mid.astype(F32)).astype(BF16)
    return hi, mid, lo


def _hgrn_kernel(q_ref, k_ref, lf_ref, v_ref, sg_ref, gn_ref, s0_ref, o_ref, sfin_ref,
                 st_ref, b_ref, *, tb):
    t = pl.program_id(1)
    n_chunks = tb // CHUNK

    @pl.when(t == 0)
    def _():
        for h in range(N_HEADS):
            st_ref[h] = s0_ref[0, h].T

    row = lax.broadcasted_iota(jnp.int32, (tb, tb), 0)
    col = lax.broadcasted_iota(jnp.int32, (tb, tb), 1)
    tri = jnp.where((col <= row) & (col // CHUNK == row // CHUNK), 1.0, 0.0).astype(BF16)
    hi, mid, lo = _split3(lf_ref[0])
    b_ref[...] = _dot(tri, hi) + _dot(tri, mid) + _dot(tri, lo)

    sub_row = lax.broadcasted_iota(jnp.int32, (8, HEAD_W), 0)
    zeros16 = jnp.zeros((SUB, HEAD_W), F32)

    def pad_rows(x, r0):
        parts = []
        if r0:
            parts.append(jnp.zeros((r0, HEAD_W), F32))
        parts.append(x)
        rest = CHUNK - r0 - x.shape[0]
        if rest:
            parts.append(jnp.zeros((rest, HEAD_W), F32))
        return jnp.concatenate(parts, axis=0) if len(parts) > 1 else x

    def chunk_body(c, carry):
        r0 = pl.multiple_of(c * CHUNK, CHUNK)
        for h in range(N_HEADS):
            hs = slice(h * HEAD_W, (h + 1) * HEAD_W)
            q = q_ref[0, pl.ds(r0, CHUNK), hs].astype(F32)
            k = k_ref[0, pl.ds(r0, CHUNK), hs].astype(F32)
            v_bf = v_ref[0, pl.ds(r0, CHUNK), hs]
            v = v_bf.astype(F32)
            b = b_ref[pl.ds(r0, CHUNK), hs]
            b_last = b[CHUNK - 1:CHUNK, :]
            st = st_ref[h]

            o = _dot_nt((q * jnp.exp(b)).astype(BF16), st.astype(BF16))

            q_segs, k_segs = [], []
            for i in range(1, CHUNK // SUB):
                lo_r, hi_r = i * SUB, (i + 1) * SUB
                ref_b = b[lo_r - 1:lo_r, :]
                q_segs.append(pad_rows(q[lo_r:hi_r] * jnp.exp(b[lo_r:hi_r] - ref_b), lo_r))
                k_segs.append(pad_rows(k[:lo_r] * jnp.exp(ref_b - b[:lo_r]), 0))
            a_off = _dot_nt(jnp.concatenate(q_segs, axis=1).astype(BF16),
                            jnp.concatenate(k_segs, axis=1).astype(BF16))
            o = o + _dot(a_off.astype(BF16), v_bf)

            diag = []
            for blk in range(CHUNK // SUB):
                base = blk * SUB
                qa, qb = q[base:base + 8], q[base + 8:base + SUB]
                ba, bb = b[base:base + 8], b[base + 8:base + SUB]
                oa = jnp.zeros((8, HEAD_W), F32)
                ob = jnp.zeros((8, HEAD_W), F32)
                for s in range(SUB):
                    ks = k[base + s:base + s + 1]
                    bs = b[base + s:base + s + 1]
                    vs = v[base + s:base + s + 1]
                    if s < 8:
                        e = jnp.exp(ba - bs)
                        if s:
                            e = jnp.where(sub_row >= s, e, 0.0)
                        oa = oa + jnp.sum(qa * ks * e, axis=-1, keepdims=True) * vs
                        ob = ob + jnp.sum(qb * ks * jnp.exp(bb - bs), axis=-1, keepdims=True) * vs
                    else:
                        e = jnp.exp(bb - bs)
                        if s > 8:
                            e = jnp.where(sub_row >= s - 8, e, 0.0)
                        ob = ob + jnp.sum(qb * ks * e, axis=-1, keepdims=True) * vs
                diag += [oa, ob]
            o = o + jnp.concatenate(diag, axis=0)

            k_dec = (k * jnp.exp(b_last - b)).astype(BF16)
            st_ref[h] = st * jnp.exp(b_last) + _dot_tn(v_bf, k_dec)

            out = _rms(o, gn_ref[...]) * sg_ref[0, pl.ds(r0, CHUNK), hs]
            o_ref[0, pl.ds(r0, CHUNK), hs] = out.astype(o_ref.dtype)
        return carry

    lax.fori_loop(0, n_chunks, chunk_body, 0)

    @pl.when(t == pl.num_programs(1) - 1)
    def _():
        for h in range(N_HEADS):
            sfin_ref[0, h] = st_ref[h].T


def _hgrn(qh, kk, lf, hi, sg, gnorm_g, s0, tb):
    B, T, _ = qh.shape
    tspec = pl.BlockSpec((1, tb, GROUP_W), lambda b, t: (b, t, 0))
    sspec = pl.BlockSpec((1, N_HEADS, HEAD_W, HEAD_W), lambda b, t: (b, 0, 0, 0))
    return pl.pallas_call(
        functools.partial(_hgrn_kernel, tb=tb),
        grid=(B, T // tb),
        in_specs=[tspec] * 5 + [pl.BlockSpec((1, HEAD_W), lambda b, t: (0, 0)), sspec],
        out_specs=[tspec, sspec],
        out_shape=[jax.ShapeDtypeStruct((B, T, GROUP_W), BF16),
                   jax.ShapeDtypeStruct((B, N_HEADS, HEAD_W, HEAD_W), F32)],
        scratch_shapes=[pltpu.VMEM((N_HEADS, HEAD_W, HEAD_W), F32), pltpu.VMEM((tb, GROUP_W), F32)],
        compiler_params=_cparams(("parallel", "arbitrary")),
        name="hgrn",
    )(qh, kk, lf, hi, sg, gnorm_g.reshape(1, HEAD_W), s0)


def _post_kernel(x_ref, oda_ref, ohg_ref, mk_ref, mv_ref, wout_ref, lnm_ref, wq_ref, wo_ref,
                 lnf_ref, wr_ref, br_ref, x2_ref, h3_ref, route_ref, om_ref, *, nb, tm, split):
    gt = tm // split
    groups = [_post_group(slice(g * gt, (g + 1) * gt), g * nb * gt, x_ref, oda_ref, ohg_ref,
                          mk_ref, mv_ref, wout_ref, lnm_ref, wq_ref, wo_ref, lnf_ref, wr_ref,
                          br_ref, x2_ref, h3_ref, route_ref, om_ref, nb=nb, gt=gt)
              for g in range(split)]
    live = []
    while groups or live:
        if groups:
            live.append(groups.pop(0))
        for gen in list(live):
            if next(gen, "done") == "done":
                live.remove(gen)


def _post_group(ts, om0, x_ref, oda_ref, ohg_ref, mk_ref, mv_ref, wout_ref, lnm_ref, wq_ref, wo_ref,
                lnf_ref, wr_ref, br_ref, x2_ref, h3_ref, route_ref, om_ref, *, nb, gt):
    rows = nb * gt
    x = x_ref[:, ts, :].reshape(rows, D_MODEL)
    mixed = (_dot(oda_ref[:, ts, :].reshape(rows, GROUP_W), wout_ref[:GROUP_W, :])
             + _dot(ohg_ref[:, ts, :].reshape(rows, GROUP_W), wout_ref[GROUP_W:, :]))
    x1 = x + mixed
    yield

    hm = _rms(x1, lnm_ref[...]).astype(BF16)
    yield
    qm = (_dot(hm, wq_ref[...]) * MEM_SCALE).astype(BF16)
    yield
    for b in range(nb):
        for h in range(MEM_HEADS):
            hs = slice(h * MEM_HEAD_DIM, (h + 1) * MEM_HEAD_DIM)
            if len(mk_ref.shape) == 4:
                mk, mv = mk_ref[b, :, h, :], mv_ref[b, :, h, :]
            else:
                mk, mv = mk_ref[b, :, hs], mv_ref[b, :, hs]
            s = _dot_nt(qm[b * gt:(b + 1) * gt, hs], mk.astype(BF16))
            e = jnp.exp(s - jnp.max(s, axis=-1, keepdims=True))
            p = e / jnp.sum(e, axis=-1, keepdims=True)
            om_ref[om0 + b * gt:om0 + (b + 1) * gt, hs] = _dot(
                p.astype(BF16), mv.astype(BF16)).astype(BF16)
    yield
    x2 = x1 + _dot(om_ref[om0:om0 + rows, :], wo_ref[...])
    x2_ref[:, ts, :] = x2.reshape(nb, gt, D_MODEL)
    yield

    h3 = _rms(x2, lnf_ref[...])
    h3_ref[:, ts, :] = h3.reshape(nb, gt, D_MODEL)
    yield

    r = _dot(h3.astype(BF16), wr_ref[...]) + br_ref[...]
    lane = lax.broadcasted_iota(jnp.int32, r.shape, 1).astype(F32)
    big = float(4 * HEAD_W)
    g_mask = lane < N_GROUPS
    gl = jnp.where(g_mask, r, NEG_INF)
    g_max = jnp.max(gl, axis=-1, keepdims=True)
    g_idx = jnp.min(jnp.where(gl == g_max, lane, big), axis=-1, keepdims=True)
    g_w = 1.0 / jnp.sum(jnp.where(g_mask, jnp.exp(r - g_max), 0.0), axis=-1, keepdims=True)
    e_lo = N_GROUPS + EXPERTS_PER_GROUP * g_idx
    el = jnp.where((lane >= e_lo) & (lane < e_lo + EXPERTS_PER_GROUP), r, NEG_INF)
    v1 = jnp.max(el, axis=-1, keepdims=True)
    i1 = jnp.min(jnp.where(el == v1, lane, big), axis=-1, keepdims=True)
    el2 = jnp.where(lane == i1, NEG_INF, el)
    v2 = jnp.max(el2, axis=-1, keepdims=True)
    i2 = jnp.min(jnp.where(el2 == v2, lane, big), axis=-1, keepdims=True)
    t = jnp.exp(v2 - v1)
    p1 = 1.0 / (1.0 + t)
    rec = jnp.where(lane == 0.0, i1 - N_GROUPS,
          jnp.where(lane == 1.0, i2 - N_GROUPS,
          jnp.where(lane == 2.0, p1 * g_w,
          jnp.where(lane == 3.0, t * p1 * g_w, 0.0))))
    route_ref[:, ts, :] = rec[:, :ROUTE_W].reshape(nb, gt, ROUTE_W)


def _post_mix(x, oda, ohg, mem_k, mem_v, w_out, ln_mem, w_q, w_o, ln_ffn, w_r, b_r, nb, tm, split):
    B, T, _ = x.shape
    tok = lambda w: pl.BlockSpec((nb, tm, w), lambda b, t: (b, t, 0))
    if mem_k.ndim == 4:
        memspec = pl.BlockSpec((nb, N_MEM, MEM_HEADS, MEM_HEAD_DIM), lambda b, t: (b, 0, 0, 0))
    else:
        memspec = pl.BlockSpec((nb, N_MEM, D_MODEL), lambda b, t: (b, 0, 0))
    fix = lambda s: pl.BlockSpec(s, lambda b, t: (0, 0))
    return pl.pallas_call(
        functools.partial(_post_kernel, nb=nb, tm=tm, split=split),
        grid=(B // nb, T // tm),
        in_specs=[tok(D_MODEL), tok(GROUP_W), tok(GROUP_W), memspec, memspec,
                  fix((D_MODEL, D_MODEL)), fix((1, D_MODEL)), fix((D_MODEL, D_MODEL)),
                  fix((D_MODEL, D_MODEL)), fix((1, D_MODEL)), fix((D_MODEL, HEAD_W)),
                  fix((1, HEAD_W))],
        out_specs=[tok(D_MODEL), tok(D_MODEL), tok(ROUTE_W)],
        out_shape=[jax.ShapeDtypeStruct((B, T, D_MODEL), F32), jax.ShapeDtypeStruct((B, T, D_MODEL), F32),
                   jax.ShapeDtypeStruct((B, T, ROUTE_W), F32)],
        scratch_shapes=[pltpu.VMEM((nb * tm, D_MODEL), BF16)],
        compiler_params=_cparams(("parallel", "parallel")),
        name="post_mix",
    )(x, oda, ohg, mem_k, mem_v, w_out, ln_mem.reshape(1, D_MODEL), w_q, w_o,
      ln_ffn.reshape(1, D_MODEL), w_r, b_r)


def _experts_kernel(blk_e_ref, n_used_ref, src_next_ref, dst_prev_ref, src0_ref, dst_ref, h_hbm,
                    wg_ref, wu_ref, wd_ref, y_hbm, xbuf, ybuf, gsem, ssem, *, bm, m_tot):
    del blk_e_ref
    i = pl.program_id(0)
    n_used = n_used_ref[0]
    slot = i % 2

    def gather_row(tab_ref, r, s, priority=0):
        pltpu.make_async_copy(h_hbm.at[pl.ds(tab_ref[0, 0, r], 1), :],
                              xbuf.at[s, pl.ds(r, 1), :], gsem.at[s]).start(priority=priority)

    def scatter_row(tab_ref, r, s, priority=0):
        pltpu.make_async_copy(ybuf.at[s, pl.ds(r, 1), :],
                              y_hbm.at[pl.ds(tab_ref[0, 0, r], 1), :], ssem.at[s]).start(priority=priority)

    def wait_block(buf, sem, s):
        pltpu.make_async_copy(buf.at[s], buf.at[s], sem.at[s]).wait()

    def spare_fill(s):
        return pltpu.make_async_copy(ybuf.at[s], y_hbm.at[pl.ds(m_tot + s * bm, bm), :], ssem.at[s])

    @pl.when(i == 0)
    def _():
        ybuf[...] = jnp.zeros(ybuf.shape, F32)
        spare_fill(0).start()
        spare_fill(1).start()
        spare_fill(1).wait()

        def body(r, c):
            gather_row(src0_ref, r, 0)
            return c
        lax.fori_loop(0, bm, body, 0)

    @pl.when(i < n_used)
    def _():
        wait_block(xbuf, gsem, slot)
        wait_block(ybuf, ssem, slot)
        x = xbuf[slot].astype(BF16)
        hg = _dot(x, wg_ref[0])
        hu = _dot(x, wu_ref[0])
        hb = (hg * jax.nn.sigmoid(hg) * hu).astype(BF16)
        ybuf[slot] = _dot(hb, wd_ref[0])
        for r in range(bm):
            gather_row(src_next_ref, r, 1 - slot, priority=r % 2)
            scatter_row(dst_prev_ref, r, 1 - slot, priority=(r + 1) % 2)

    @pl.when(i == n_used - 1)
    def _():
        def body(r, c):
            scatter_row(dst_ref, r, slot)
            return c
        lax.fori_loop(0, bm, body, 0)
        wait_block(xbuf, gsem, 1 - slot)
        wait_block(ybuf, ssem, 1 - slot)
        wait_block(ybuf, ssem, slot)


def _experts(h3, blk_e, n_used, src_tab, dst_tab, dst_prev_tab, wg, wu, wd, bm):
    n_blocks = src_tab.shape[0]
    m_tot = TOP_K * h3.shape[0]
    tab = lambda f: pl.BlockSpec((1, 1, bm), f, memory_space=pltpu.SMEM)
    cur = lambda i, e, n: (i, 0, 0)
    nxt = lambda i, e, n: (jnp.minimum(i + 1, n_blocks - 1), 0, 0)
    first = lambda i, e, n: (0, 0, 0)
    wspec = lambda a, b: pl.BlockSpec((1, a, b), lambda i, e, n: (e[i], 0, 0))
    grid_spec = pltpu.PrefetchScalarGridSpec(
        num_scalar_prefetch=2,
        grid=(n_blocks,),
        in_specs=[tab(nxt), tab(cur), tab(first), tab(cur), pl.BlockSpec(memory_space=pl.ANY),
                  wspec(D_MODEL, EXPERT_FF), wspec(D_MODEL, EXPERT_FF), wspec(EXPERT_FF, D_MODEL)],
        out_specs=pl.BlockSpec(memory_space=pl.ANY),
        scratch_shapes=[pltpu.VMEM((2, bm, D_MODEL), F32), pltpu.VMEM((2, bm, D_MODEL), F32),
                        pltpu.SemaphoreType.DMA((2,)), pltpu.SemaphoreType.DMA((2,))],
    )
    return pl.pallas_call(
        functools.partial(_experts_kernel, bm=bm, m_tot=m_tot),
        grid_spec=grid_spec,
        out_shape=jax.ShapeDtypeStruct((m_tot + 2 * bm, D_MODEL), F32),
        compiler_params=_cparams(("arbitrary",)),
        name="experts",
    )(blk_e, n_used, src_tab, dst_prev_tab, src_tab, dst_tab, h3, wg, wu, wd)


def _combine_kernel(x_ref, y0_ref, y1_ref, route_ref, g_ref, o_ref):
    rt = route_ref[...]
    y = x_ref[...] + rt[:, 2:3] * y0_ref[...] + rt[:, 3:4] * y1_ref[...]
    o_ref[...] = _rms(y, g_ref[...])


def _combine(x2, y, route, final_g, tm):
    n = x2.shape[0]
    nt = n // tm
    return pl.pallas_call(
        _combine_kernel,
        grid=(nt,),
        in_specs=[pl.BlockSpec((tm, D_MODEL), lambda i: (i, 0)),
                  pl.BlockSpec((tm, D_MODEL), lambda i: (i, 0)),
                  pl.BlockSpec((tm, D_MODEL), lambda i: (i + nt, 0)),
                  pl.BlockSpec((tm, ROUTE_W), lambda i: (i, 0)),
                  pl.BlockSpec((1, D_MODEL), lambda i: (0, 0))],
        out_specs=pl.BlockSpec((tm, D_MODEL), lambda i: (i, 0)),
        out_shape=jax.ShapeDtypeStruct((n, D_MODEL), F32),
        compiler_params=_cparams(("parallel",)),
        name="combine",
    )(x2, y, y, route, final_g.reshape(1, D_MODEL))


def _routing_tables(route, bm):
    n = route.shape[0]
    m_tot = TOP_K * n
    flat_e = jnp.concatenate([route[:, 0], route[:, 1]]).astype(jnp.int32)
    experts = jnp.arange(N_EXPERTS, dtype=jnp.int32)
    counts = jnp.sum(flat_e[:, None] == experts[None, :], axis=0, dtype=jnp.int32)
    n_blk_e = (counts + bm - 1) // bm
    blk_end = jnp.cumsum(n_blk_e)
    n_blocks = m_tot // bm + N_EXPERTS
    blk = jnp.arange(n_blocks, dtype=jnp.int32)
    blk_e = jnp.minimum(jnp.sum(blk[:, None] >= blk_end[None, :], axis=1, dtype=jnp.int32),
                        N_EXPERTS - 1)
    r = jnp.arange(bm, dtype=jnp.int32)[None, :]
    n_pad = (n_blk_e * bm - counts)[:, None]
    pad_keys = jnp.where(r < n_pad, 2 * experts[:, None] + 1, 2 * N_EXPERTS + 1)
    keys = jnp.concatenate([2 * flat_e, pad_keys.reshape(-1)])
    vals = jnp.concatenate([jnp.arange(m_tot, dtype=jnp.int32),
                            jnp.full((N_EXPERTS * bm,), -1, jnp.int32)])
    m = lax.sort((keys, vals), num_keys=1)[1].reshape(n_blocks, bm)
    real = m >= 0
    spare = m_tot + (blk[:, None] % 2) * bm + r
    src = jnp.where(real, m % n, 0)
    dst = jnp.where(real, m, spare)
    dst_prev = jnp.concatenate([m_tot + bm + r, dst[:-1]], axis=0)
    shape = (n_blocks, 1, bm)
    return (blk_e, blk_end[N_EXPERTS - 1:].astype(jnp.int32), src.reshape(shape), dst.reshape(shape),
            dst_prev.reshape(shape))


def _layer(x, past_k, past_v, s0, mem_k, mem_v, p, cfg):
    B, T, _ = x.shape
    n = B * T
    (q, kf, vf, kb, vt, qh, lf, kk, hi, sg) = _in_proj(
        x, p["ln_mix"], p["w_in"], p["lb"], cfg["nb"], cfg["tm_proj"])
    oda = _diff_attn(q, kb, vt, past_k, past_v, p["slopes"], p["lam"], p["da_subln"],
                     cfg["tq"], cfg["tk"])
    ohg, s_new = _hgrn(qh, kk, lf, hi, sg, p["hg_gnorm"], s0, cfg["tb"])
    x2, h3, route = _post_mix(x, oda, ohg, mem_k, mem_v, p["w_out"], p["ln_mem"], p["w_mem_q"],
                              p["w_mem_o"], p["ln_ffn"], p["w_r"], p["b_r"], cfg["nb"], cfg["tm_post"],
                              cfg["split"])
    route2 = route.reshape(n, ROUTE_W)
    blk_e, n_used, src_tab, dst_tab, dst_prev_tab = _routing_tables(route2, cfg["bm"])
    y = _experts(h3.reshape(n, D_MODEL), blk_e, n_used, src_tab, dst_tab, dst_prev_tab,
                 p["e_gate"], p["e_up"], p["e_down"], cfg["bm"])
    out = _combine(x2.reshape(n, D_MODEL), y, route2, p["final_g"], cfg["tm_in"])
    return out.reshape(B, T, D_MODEL), kf[None], vf[None], s_new[None]


PROMPT_CFG = dict(tm_in=256, tq=512, tk=512, tb=256, nb=1, tm_proj=256, tm_post=512, split=2, bm=512)
SAMPLE_CFG = dict(tm_in=256, tq=64, tk=64, tb=64, nb=4, tm_proj=64, tm_post=64, split=1, bm=128)


def kernel(x_prompt, x_sample, mem_prompt, cache_diff_k, cache_diff_v, state_hgrn, cache_mem_k, cache_mem_v, ln_mix_g, w_in, da_lambda, da_subln_g, hg_lb_logits, hg_gnorm_g, w_out, ln_mem_g, mem_norm_g, w_mem_q, w_mem_k, w_mem_v, w_mem_o, ln_ffn_g, router_group_w, router_group_b, router_expert_w, router_expert_b, exp_w_gate, exp_w_up, exp_w_down, final_norm_g):
    assert w_in.shape[0] == 1, "single-layer configuration"
    lb_all = jnp.cumsum(jax.nn.softmax(hg_lb_logits.astype(F32), axis=0), axis=0)
    lp = da_lambda[0].astype(F32)
    lam = jnp.exp(jnp.sum(lp[0] * lp[1])) - jnp.exp(jnp.sum(lp[2] * lp[3])) + LAM_INIT
    w_r = jnp.zeros((D_MODEL, HEAD_W), F32)
    w_r = w_r.at[:, :N_GROUPS].set(router_group_w[0]).at[:, N_GROUPS:N_GROUPS + N_EXPERTS].set(router_expert_w[0])
    b_r = jnp.zeros((1, HEAD_W), F32)
    b_r = b_r.at[0, :N_GROUPS].set(router_group_b[0]).at[0, N_GROUPS:N_GROUPS + N_EXPERTS].set(router_expert_b[0])
    p = {
        "ln_mix": ln_mix_g[0], "w_in": w_in[0].astype(BF16), "lb": lb_all[0],
        "slopes": jnp.exp2(-8.0 * jnp.arange(1, N_HEADS + 1, dtype=F32) / N_HEADS),
        "lam": lam.reshape(1), "da_subln": da_subln_g[0], "hg_gnorm": hg_gnorm_g[0],
        "w_out": w_out[0].astype(BF16), "ln_mem": ln_mem_g[0], "w_mem_q": w_mem_q[0].astype(BF16),
        "w_mem_o": w_mem_o[0].astype(BF16), "ln_ffn": ln_ffn_g[0], "w_r": w_r.astype(BF16), "b_r": b_r,
        "e_gate": exp_w_gate[0].astype(BF16), "e_up": exp_w_up[0].astype(BF16),
        "e_down": exp_w_down[0].astype(BF16), "final_g": final_norm_g,
    }
    Bp, Tp, _ = x_prompt.shape
    Bs, Ts, _ = x_sample.shape
    past_len = cache_diff_k.shape[2]

    mkf, mvf, mkb, mvb = _memory_kv(mem_prompt, mem_norm_g[0], w_mem_k[0].astype(BF16),
                                    w_mem_v[0].astype(BF16))
    zero_state = jnp.zeros((Bp, N_HEADS, HEAD_W, HEAD_W), F32)
    yp, kp, vp, sp = _layer(x_prompt, None, None, zero_state, mkb, mvb, p, PROMPT_CFG)
    ys, ks, vs, ss = _layer(
        x_sample, cache_diff_k[0].reshape(Bs, past_len, GROUP_W), cache_diff_v[0].reshape(Bs, past_len, GROUP_W),
        state_hgrn[0], cache_mem_k[0], cache_mem_v[0], p, SAMPLE_CFG)
    return (yp, ys, kp, vp, sp, mkf[None], mvf[None], ks, vs, ss)
```

```python
import functools
import math

import jax
import jax.numpy as jnp
from jax import lax
from jax.experimental import pallas as pl
from jax.experimental.pallas import tpu as pltpu

F32 = jnp.float32
BF16 = jnp.bfloat16

D_MODEL = 1024
EPS = 1e-5
CHUNK = 64
N_HEADS = 4
HEAD_W = 128
QK_DIM = 64
GROUP_W = N_HEADS * HEAD_W
N_PROJ = 7
DA_SCALE = QK_DIM ** -0.5
LOG2E = 1.4426950408889634
LAM_INIT = 0.8 - 0.6 * math.exp(-0.3 * 0)
N_MEM = 256
MEM_HEADS = 4
MEM_HEAD_DIM = D_MODEL // MEM_HEADS
MEM_SCALE = MEM_HEAD_DIM ** -0.5
N_GROUPS = 4
EXPERTS_PER_GROUP = 8
N_EXPERTS = N_GROUPS * EXPERTS_PER_GROUP
TOP_K = 2
EXPERT_FF = 512
ROUTE_W = 8
SUB = 16
SUB_KEYS = 128
N_SLAB = D_MODEL // HEAD_W
NEG_INF = float("-inf")

VMEM_LIMIT = 48 * 1024 * 1024


def _cparams(sem):
    return pltpu.CompilerParams(dimension_semantics=sem, vmem_limit_bytes=VMEM_LIMIT)


def _rms(x, g):
    return x * lax.rsqrt(jnp.mean(x * x, axis=-1, keepdims=True) + EPS) * g


def _dot(a, b):
    return jnp.dot(a, b, preferred_element_type=F32)


def _dot_nt(a, b):
    return lax.dot_general(a, b, (((1,), (1,)), ((), ())), preferred_element_type=F32)


def _dot_tn(a, b):
    return lax.dot_general(a, b, (((0,), (0,)), ((), ())), preferred_element_type=F32)


def _memkv_kernel(m_ref, g_ref, wk_ref, wv_ref, kf_ref, vf_ref, kb_ref, vb_ref):
    mn = _rms(m_ref[0], g_ref[...]).astype(BF16)
    k = _dot(mn, wk_ref[...])
    v = _dot(mn, wv_ref[...])
    kf_ref[0] = k.reshape(N_MEM, MEM_HEADS, MEM_HEAD_DIM)
    vf_ref[0] = v.reshape(N_MEM, MEM_HEADS, MEM_HEAD_DIM)
    kb_ref[0] = k.astype(BF16)
    vb_ref[0] = v.astype(BF16)


def _memory_kv(mem, g, wk, wv):
    B = mem.shape[0]
    blk = pl.BlockSpec((1, N_MEM, D_MODEL), lambda b: (b, 0, 0))
    hblk = pl.BlockSpec((1, N_MEM, MEM_HEADS, MEM_HEAD_DIM), lambda b: (b, 0, 0, 0))
    wspec = pl.BlockSpec((D_MODEL, D_MODEL), lambda b: (0, 0))
    return pl.pallas_call(
        _memkv_kernel,
        grid=(B,),
        in_specs=[blk, pl.BlockSpec((1, D_MODEL), lambda b: (0, 0)), wspec, wspec],
        out_specs=[hblk, hblk, blk, blk],
        out_shape=[jax.ShapeDtypeStruct((B, N_MEM, MEM_HEADS, MEM_HEAD_DIM), F32)] * 2
                  + [jax.ShapeDtypeStruct(mem.shape, BF16)] * 2,
        compiler_params=_cparams(("parallel",)),
        name="memory_kv",
    )(mem, g.reshape(1, D_MODEL), wk, wv)


def _in_proj_kernel(x_ref, g_ref, w_ref, lb_ref, q_ref, kf_ref, vf_ref, kb_ref, vt_ref,
                    qh_ref, lf_ref, kk_ref, hi_ref, sg_ref, *, nb, tm):
    rows = nb * tm
    h = _rms(x_ref[...].reshape(rows, D_MODEL), g_ref[...]).astype(BF16)

    def proj(i):
        return _dot(h, w_ref[:, i * GROUP_W:(i + 1) * GROUP_W])

    def put(ref, val):
        ref[...] = val.reshape(nb, tm, GROUP_W).astype(ref.dtype)

    def put_heads(ref, val):
        ref[...] = val.reshape(nb, tm, N_HEADS, HEAD_W)

    put(q_ref, proj(0) * (DA_SCALE * LOG2E))
    dk = proj(1)
    put_heads(kf_ref, dk)
    put(kb_ref, dk)
    dv = proj(2)
    put_heads(vf_ref, dv)
    for b in range(nb):
        vt_ref[b] = dv[b * tm:(b + 1) * tm].T.astype(BF16)
    hq = proj(3)
    put(qh_ref, hq * jax.nn.sigmoid(hq))
    hf = proj(4)
    lb = lb_ref[...]
    put(lf_ref, jnp.log(lb + (1.0 - lb) * jax.nn.sigmoid(hf)))
    put(kk_ref, (1.0 - lb) * jax.nn.sigmoid(-hf))
    put(hi_ref, proj(5))
    hg = proj(6)
    put(sg_ref, hg * jax.nn.sigmoid(hg))


def _in_proj(x, g, w_bf, lb, nb, tm):
    B, T, _ = x.shape
    tok = lambda w: pl.BlockSpec((nb, tm, w), lambda b, t: (b, t, 0))
    fix = lambda s: pl.BlockSpec(s, lambda b, t: (0, 0))
    f32o = jax.ShapeDtypeStruct((B, T, GROUP_W), F32)
    bfo = jax.ShapeDtypeStruct((B, T, GROUP_W), BF16)
    o_spec = tok(GROUP_W)
    ho = jax.ShapeDtypeStruct((B, T, N_HEADS, HEAD_W), F32)
    h_spec = pl.BlockSpec((nb, tm, N_HEADS, HEAD_W), lambda b, t: (b, t, 0, 0))
    vt_spec = pl.BlockSpec((nb, GROUP_W, tm), lambda b, t: (b, 0, t))
    return pl.pallas_call(
        functools.partial(_in_proj_kernel, nb=nb, tm=tm),
        grid=(B // nb, T // tm),
        in_specs=[tok(D_MODEL), fix((1, D_MODEL)), fix((D_MODEL, N_PROJ * GROUP_W)), fix((1, GROUP_W))],
        out_specs=[o_spec, h_spec, h_spec, o_spec, vt_spec] + [o_spec] * 5,
        out_shape=[bfo, ho, ho, bfo, jax.ShapeDtypeStruct((B, GROUP_W, T), BF16),
                   bfo, f32o, bfo, bfo, bfo],
        compiler_params=_cparams(("parallel", "parallel")),
        name="in_proj",
    )(x, g.reshape(1, D_MODEL), w_bf, lb.reshape(1, GROUP_W))


def _attn_kernel(*refs, tq, tk, past_len, tkp):
    if past_len:
        (slope_ref, lam_ref, q_ref, k_ref, vt_ref, pk_ref, pv_ref, g_ref, o_ref,
         m_ref, l_ref, acc_ref, base_ref, t_ref) = refs
    else:
        (slope_ref, lam_ref, q_ref, k_ref, vt_ref, g_ref, o_ref,
         m_ref, l_ref, acc_ref, base_ref, t_ref) = refs
    h = pl.program_id(1)
    qi = pl.program_id(2)
    slope2 = slope_ref[h] * LOG2E
    qt = q_ref[0].astype(F32).T.astype(BF16)
    zeros = jnp.zeros((QK_DIM, tq), BF16)
    qt_maps = (jnp.concatenate([qt[:QK_DIM], zeros], axis=0),
               jnp.concatenate([zeros, qt[QK_DIM:]], axis=0))
    m_ref[...] = jnp.full(m_ref.shape, NEG_INF, F32)
    l_ref[...] = jnp.zeros(l_ref.shape, F32)
    acc_ref[...] = jnp.zeros(acc_ref.shape, F32)
    q0 = past_len + qi * tq

    def lane_pad(x, lo, fill):
        if lo == 0:
            return x
        return jnp.concatenate([jnp.full((x.shape[0], lo), fill, x.dtype), x], axis=1)

    def update(n_keys, get_k, get_vt, get_bias, shift, q_lo=lambda u: 0):
        sub = min(n_keys, SUB_KEYS)
        n_sub = n_keys // sub
        mx = [None, None]
        for u in range(n_sub):
            lo = q_lo(u)
            bias, visible = get_bias(u)
            k = get_k(u)
            for c in range(2):
                t = _dot(k, qt_maps[c][:, lo:]) + bias
                if visible is not None:
                    w = visible.shape[1]
                    head = jnp.where(visible, t[:, :w], NEG_INF)
                    t = head if w == t.shape[1] else jnp.concatenate([head, t[:, w:]], axis=1)
                t_ref[c, u * sub:(u + 1) * sub, lo:] = t
                mu = lane_pad(jnp.max(t, axis=0, keepdims=True), lo, NEG_INF)
                mx[c] = mu if mx[c] is None else jnp.maximum(mx[c], mu)
        off, alpha = [], []
        for c in range(2):
            m_old = m_ref[c]
            m_new = jnp.maximum(m_old, mx[c] + shift)
            off.append(m_new - shift)
            alpha.append(jnp.exp2(m_old - m_new))
            m_ref[c] = m_new
        lsum = [None, None]
        pv = [None, None]
        for u in range(n_sub):
            lo = q_lo(u)
            vt = get_vt(u)
            for c in range(2):
                p = jnp.exp2(t_ref[c, u * sub:(u + 1) * sub, lo:] - off[c][:, lo:])
                ls = lane_pad(jnp.sum(p, axis=0, keepdims=True), lo, 0.0)
                pu = lane_pad(_dot(vt, p.astype(BF16)), lo, 0.0)
                lsum[c] = ls if lsum[c] is None else lsum[c] + ls
                pv[c] = pu if pv[c] is None else pv[c] + pu
        for c in range(2):
            l_ref[c] = alpha[c] * l_ref[c] + lsum[c]
            acc_ref[c] = alpha[c] * acc_ref[c] + pv[c]

    def rel_bias(n_keys, key0):
        key = lax.broadcasted_iota(jnp.int32, (n_keys, tq), 0) + key0
        qry = lax.broadcasted_iota(jnp.int32, (n_keys, tq), 1)
        return (qry - key).astype(F32) * (-slope2)

    def block_shift(k0):
        return (q0 - k0).astype(F32) * (-slope2)

    if past_len:
        sub_p = min(tkp, SUB_KEYS)

        def past_body(j, carry):
            k0 = pl.multiple_of(j * tkp, tkp)
            update(tkp,
                   lambda u: pk_ref[0, pl.ds(k0 + u * sub_p, sub_p), :].astype(BF16),
                   lambda u: pv_ref[0, pl.ds(k0 + u * sub_p, sub_p), :].T.astype(BF16),
                   lambda u: (rel_bias(sub_p, u * sub_p), None), block_shift(k0))
            return carry
        lax.fori_loop(0, past_len // tkp, past_body, 0)

    n_q = k_ref.shape[1] // tq
    if n_q > 1 or tq > SUB_KEYS:
        base_ref[...] = rel_bias(base_ref.shape[0], 0)
    if n_q > 1:
        sub_k = min(tk, SUB_KEYS)

        def prev_body(j, carry):
            k0 = pl.multiple_of(j * tk, tk)
            update(tk,
                   lambda u: k_ref[0, pl.ds(k0 + u * sub_k, sub_k), :],
                   lambda u: vt_ref[0, :, pl.ds(k0 + u * sub_k, sub_k)],
                   lambda u: (base_ref[u * sub_k:(u + 1) * sub_k, :], None),
                   block_shift(past_len + k0))
            return carry
        lax.fori_loop(0, qi * (tq // tk), prev_body, 0)

    kd = 0 if n_q == 1 else pl.multiple_of(qi * tq, tq)
    sub_d = min(tq, SUB_KEYS)
    key = lax.broadcasted_iota(jnp.int32, (sub_d, sub_d), 0)
    qry = lax.broadcasted_iota(jnp.int32, (sub_d, sub_d), 1)
    strip_bias = jnp.abs(qry - key).astype(F32) * (-slope2)
    strip_visible = (key // CHUNK) <= (qry // CHUNK)

    def diag_bias(u):
        lo = (u + 1) * sub_d
        if lo == tq:
            return strip_bias, strip_visible
        later = base_ref[u * sub_d:(u + 1) * sub_d, lo:]
        return jnp.concatenate([strip_bias, later], axis=1), strip_visible

    update(tq,
           lambda u: k_ref[0, pl.ds(kd + u * sub_d, sub_d), :],
           lambda u: vt_ref[0, :, pl.ds(kd + u * sub_d, sub_d)],
           diag_bias, 0.0, q_lo=lambda u: u * sub_d)

    lam = lam_ref[0]
    ot = acc_ref[0] / l_ref[0] - lam * (acc_ref[1] / l_ref[1])
    ot = ot * lax.rsqrt(jnp.mean(ot * ot, axis=0, keepdims=True) + EPS)
    o_ref[0] = (ot.T * g_ref[...] * (1.0 - LAM_INIT)).astype(o_ref.dtype)


def _diff_attn(q, k, vt, past_k, past_v, slopes, lam, subln_g, tq, tk):
    B, T, _ = q.shape
    past_len = 0 if past_k is None else past_k.shape[1]
    tkp = min(512, past_len) if past_len else 0
    smem = pl.BlockSpec(memory_space=pltpu.SMEM)
    qspec = pl.BlockSpec((1, tq, HEAD_W), lambda b, h, i: (b, i, h))
    in_specs = [smem, smem, qspec,
                pl.BlockSpec((1, T, HEAD_W), lambda b, h, i: (b, 0, h)),
                pl.BlockSpec((1, HEAD_W, T), lambda b, h, i: (b, h, 0))]
    args = [slopes, lam, q, k, vt]
    if past_len:
        pspec = pl.BlockSpec((1, past_len, HEAD_W), lambda b, h, i: (b, 0, h))
        in_specs += [pspec, pspec]
        args += [past_k, past_v]
    in_specs.append(pl.BlockSpec((1, HEAD_W), lambda b, h, i: (0, 0)))
    args.append(subln_g.reshape(1, HEAD_W))
    return pl.pallas_call(
        functools.partial(_attn_kernel, tq=tq, tk=tk, past_len=past_len, tkp=tkp),
        grid=(B, N_HEADS, T // tq),
        in_specs=in_specs,
        out_specs=qspec,
        out_shape=jax.ShapeDtypeStruct((B, T, GROUP_W), BF16),
        scratch_shapes=[pltpu.VMEM((2, 1, tq), F32), pltpu.VMEM((2, 1, tq), F32),
                        pltpu.VMEM((2, HEAD_W, tq), F32), pltpu.VMEM((max(tk, tq), tq), F32),
                        pltpu.VMEM((2, max(tk, tq, tkp), tq), F32)],
        compiler_params=_cparams(("parallel", "parallel", "parallel")),
        name="diff_attn",
    )(*args)


def _split3(x):
    hi = x.astype(BF16)
    r = x - hi.astype(F32)
    mid = r.astype(BF16)
    lo = (r - mid.astype(F32)).astype(BF16)
    return hi, mid, lo


def _hgrn_kernel(q_ref, k_ref, lf_ref, v_ref, sg_ref, gn_ref, s0_ref, o_ref, sfin_ref,
                 st_ref, b_ref, *, tb):
    t = pl.program_id(1)
    n_chunks = tb // CHUNK

    @pl.when(t == 0)
    def _():
        for h in range(N_HEADS):
            st_ref[h] = s0_ref[0, h].T

    row = lax.broadcasted_iota(jnp.int32, (tb, tb), 0)
    col = lax.broadcasted_iota(jnp.int32, (tb, tb), 1)
    tri = jnp.where((col <= row) & (col // CHUNK == row // CHUNK), 1.0, 0.0).astype(BF16)
    hi, mid, lo = _split3(lf_ref[0])
    b_ref[...] = _dot(tri, hi) + _dot(tri, mid) + _dot(tri, lo)

    sub_row = lax.broadcasted_iota(jnp.int32, (8, HEAD_W), 0)
    zeros16 = jnp.zeros((SUB, HEAD_W), F32)

    def pad_rows(x, r0):
        parts = []
        if r0:
            parts.append(jnp.zeros((r0, HEAD_W), F32))
        parts.append(x)
        rest = CHUNK - r0 - x.shape[0]
        if rest:
            parts.append(jnp.zeros((rest, HEAD_W), F32))
        return jnp.concatenate(parts, axis=0) if len(parts) > 1 else x

    def chunk_body(c, carry):
        r0 = pl.multiple_of(c * CHUNK, CHUNK)
        for h in range(N_HEADS):
            hs = slice(h * HEAD_W, (h + 1) * HEAD_W)
            q = q_ref[0, pl.ds(r0, CHUNK), hs].astype(F32)
            k = k_ref[0, pl.ds(r0, CHUNK), hs].astype(F32)
            v_bf = v_ref[0, pl.ds(r0, CHUNK), hs]
            v = v_bf.astype(F32)
            b = b_ref[pl.ds(r0, CHUNK), hs]
            b_last = b[CHUNK - 1:CHUNK, :]
            st = st_ref[h]

            o = _dot_nt((q * jnp.exp(b)).astype(BF16), st.astype(BF16))

            q_segs, k_segs = [], []
            for i in range(1, CHUNK // SUB):
                lo_r, hi_r = i * SUB, (i + 1) * SUB
                ref_b = b[lo_r - 1:lo_r, :]
                q_segs.append(pad_rows(q[lo_r:hi_r] * jnp.exp(b[lo_r:hi_r] - ref_b), lo_r))
                k_segs.append(pad_rows(k[:lo_r] * jnp.exp(ref_b - b[:lo_r]), 0))
            a_off = _dot_nt(jnp.concatenate(q_segs, axis=1).astype(BF16),
                            jnp.concatenate(k_segs, axis=1).astype(BF16))
            o = o + _dot(a_off.astype(BF16), v_bf)

            diag = []
            for blk in range(CHUNK // SUB):
                base = blk * SUB
                qa, qb = q[base:base + 8], q[base + 8:base + SUB]
                ba, bb = b[base:base + 8], b[base + 8:base + SUB]
                oa = jnp.zeros((8, HEAD_W), F32)
                ob = jnp.zeros((8, HEAD_W), F32)
                for s in range(SUB):
                    ks = k[base + s:base + s + 1]
                    bs = b[base + s:base + s + 1]
                    vs = v[base + s:base + s + 1]
                    if s < 8:
                        e = jnp.exp(ba - bs)
                        if s:
                            e = jnp.where(sub_row >= s, e, 0.0)
                        oa = oa + jnp.sum(qa * ks * e, axis=-1, keepdims=True) * vs
                        ob = ob + jnp.sum(qb * ks * jnp.exp(bb - bs), axis=-1, keepdims=True) * vs
                    else:
                        e = jnp.exp(bb - bs)
                        if s > 8:
                            e = jnp.where(sub_row >= s - 8, e, 0.0)
                        ob = ob + jnp.sum(qb * ks * e, axis=-1, keepdims=True) * vs
                diag += [oa, ob]
            o = o + jnp.concatenate(diag, axis=0)

            k_dec = (k * jnp.exp(b_last - b)).astype(BF16)
            st_ref[h] = st * jnp.exp(b_last) + _dot_tn(v_bf, k_dec)

            out = _rms(o, gn_ref[...]) * sg_ref[0, pl.ds(r0, CHUNK), hs]
            o_ref[0, pl.ds(r0, CHUNK), hs] = out.astype(o_ref.dtype)
        return carry

    lax.fori_loop(0, n_chunks, chunk_body, 0)

    @pl.when(t == pl.num_programs(1) - 1)
    def _():
        for h in range(N_HEADS):
            sfin_ref[0, h] = st_ref[h].T


def _hgrn(qh, kk, lf, hi, sg, gnorm_g, s0, tb):
    B, T, _ = qh.shape
    tspec = pl.BlockSpec((1, tb, GROUP_W), lambda b, t: (b, t, 0))
    sspec = pl.BlockSpec((1, N_HEADS, HEAD_W, HEAD_W), lambda b, t: (b, 0, 0, 0))
    return pl.pallas_call(
        functools.partial(_hgrn_kernel, tb=tb),
        grid=(B, T // tb),
        in_specs=[tspec] * 5 + [pl.BlockSpec((1, HEAD_W), lambda b, t: (0, 0)), sspec],
        out_specs=[tspec, sspec],
        out_shape=[jax.ShapeDtypeStruct((B, T, GROUP_W), BF16),
                   jax.ShapeDtypeStruct((B, N_HEADS, HEAD_W, HEAD_W), F32)],
        scratch_shapes=[pltpu.VMEM((N_HEADS, HEAD_W, HEAD_W), F32), pltpu.VMEM((tb, GROUP_W), F32)],
        compiler_params=_cparams(("parallel", "arbitrary")),
        name="hgrn",
    )(qh, kk, lf, hi, sg, gnorm_g.reshape(1, HEAD_W), s0)


def _post_kernel(x_ref, oda_ref, ohg_ref, mk_ref, mv_ref, wout_ref, lnm_ref, wq_ref, wo_ref,
                 lnf_ref, wr_ref, br_ref, x2_ref, h3_ref, route_ref, om_ref, *, nb, tm, split):
    gt = tm // split
    groups = [_post_group(slice(g * gt, (g + 1) * gt), g * nb * gt, x_ref, oda_ref, ohg_ref,
                          mk_ref, mv_ref, wout_ref, lnm_ref, wq_ref, wo_ref, lnf_ref, wr_ref,
                          br_ref, x2_ref, h3_ref, route_ref, om_ref, nb=nb, gt=gt)
              for g in range(split)]
    live = []
    while groups or live:
        if groups:
            live.append(groups.pop(0))
        for gen in list(live):
            if next(gen, "done") == "done":
                live.remove(gen)


def _post_group(ts, om0, x_ref, oda_ref, ohg_ref, mk_ref, mv_ref, wout_ref, lnm_ref, wq_ref, wo_ref,
                lnf_ref, wr_ref, br_ref, x2_ref, h3_ref, route_ref, om_ref, *, nb, gt):
    rows = nb * gt
    x = x_ref[:, ts, :].reshape(rows, D_MODEL)
    mixed = (_dot(oda_ref[:, ts, :].reshape(rows, GROUP_W), wout_ref[:GROUP_W, :])
             + _dot(ohg_ref[:, ts, :].reshape(rows, GROUP_W), wout_ref[GROUP_W:, :]))
    x1 = x + mixed
    yield

    hm = _rms(x1, lnm_ref[...]).astype(BF16)
    yield
    qm = (_dot(hm, wq_ref[...]) * MEM_SCALE).astype(BF16)
    yield
    for b in range(nb):
        for h in range(MEM_HEADS):
            hs = slice(h * MEM_HEAD_DIM, (h + 1) * MEM_HEAD_DIM)
            if len(mk_ref.shape) == 4:
                mk, mv = mk_ref[b, :, h, :], mv_ref[b, :, h, :]
            else:
                mk, mv = mk_ref[b, :, hs], mv_ref[b, :, hs]
            s = _dot_nt(qm[b * gt:(b + 1) * gt, hs], mk.astype(BF16))
            e = jnp.exp(s - jnp.max(s, axis=-1, keepdims=True))
            p = e / jnp.sum(e, axis=-1, keepdims=True)
            om_ref[om0 + b * gt:om0 + (b + 1) * gt, hs] = _dot(
                p.astype(BF16), mv.astype(BF16)).astype(BF16)
    yield
    x2 = x1 + _dot(om_ref[om0:om0 + rows, :], wo_ref[...])
    x2_ref[:, ts, :] = x2.reshape(nb, gt, D_MODEL)
    yield

    h3 = _rms(x2, lnf_ref[...])
    h3_ref[:, ts] = h3.reshape(nb, gt, N_SLAB, HEAD_W)
    yield

    r = _dot(h3.astype(BF16), wr_ref[...]) + br_ref[...]
    lane = lax.broadcasted_iota(jnp.int32, r.shape, 1).astype(F32)
    big = float(4 * HEAD_W)
    g_mask = lane < N_GROUPS
    gl = jnp.where(g_mask, r, NEG_INF)
    g_max = jnp.max(gl, axis=-1, keepdims=True)
    g_idx = jnp.min(jnp.where(gl == g_max, lane, big), axis=-1, keepdims=True)
    g_w = 1.0 / jnp.sum(jnp.where(g_mask, jnp.exp(r - g_max), 0.0), axis=-1, keepdims=True)
    e_lo = N_GROUPS + EXPERTS_PER_GROUP * g_idx
    el = jnp.where((lane >= e_lo) & (lane < e_lo + EXPERTS_PER_GROUP), r, NEG_INF)
    v1 = jnp.max(el, axis=-1, keepdims=True)
    i1 = jnp.min(jnp.where(el == v1, lane, big), axis=-1, keepdims=True)
    el2 = jnp.where(lane == i1, NEG_INF, el)
    v2 = jnp.max(el2, axis=-1, keepdims=True)
    i2 = jnp.min(jnp.where(el2 == v2, lane, big), axis=-1, keepdims=True)
    t = jnp.exp(v2 - v1)
    p1 = 1.0 / (1.0 + t)
    rec = jnp.where(lane == 0.0, i1 - N_GROUPS,
          jnp.where(lane == 1.0, i2 - N_GROUPS,
          jnp.where(lane == 2.0, p1 * g_w,
          jnp.where(lane == 3.0, t * p1 * g_w, 0.0))))
    route_ref[:, ts, :] = rec[:, :ROUTE_W].reshape(nb, gt, ROUTE_W)


def _post_mix(x, oda, ohg, mem_k, mem_v, w_out, ln_mem, w_q, w_o, ln_ffn, w_r, b_r, nb, tm, split):
    B, T, _ = x.shape
    tok = lambda w: pl.BlockSpec((nb, tm, w), lambda b, t: (b, t, 0))
    if mem_k.ndim == 4:
        memspec = pl.BlockSpec((nb, N_MEM, MEM_HEADS, MEM_HEAD_DIM), lambda b, t: (b, 0, 0, 0))
    else:
        memspec = pl.BlockSpec((nb, N_MEM, D_MODEL), lambda b, t: (b, 0, 0))
    fix = lambda s: pl.BlockSpec(s, lambda b, t: (0, 0))
    return pl.pallas_call(
        functools.partial(_post_kernel, nb=nb, tm=tm, split=split),
        grid=(B // nb, T // tm),
        in_specs=[tok(D_MODEL), tok(GROUP_W), tok(GROUP_W), memspec, memspec,
                  fix((D_MODEL, D_MODEL)), fix((1, D_MODEL)), fix((D_MODEL, D_MODEL)),
                  fix((D_MODEL, D_MODEL)), fix((1, D_MODEL)), fix((D_MODEL, HEAD_W)),
                  fix((1, HEAD_W))],
        out_specs=[tok(D_MODEL),
                   pl.BlockSpec((nb, tm, N_SLAB, HEAD_W), lambda b, t: (b, t, 0, 0)),
                   tok(ROUTE_W)],
        out_shape=[jax.ShapeDtypeStruct((B, T, D_MODEL), F32),
                   jax.ShapeDtypeStruct((B, T, N_SLAB, HEAD_W), F32),
                   jax.ShapeDtypeStruct((B, T, ROUTE_W), F32)],
        scratch_shapes=[pltpu.VMEM((nb * tm, D_MODEL), BF16)],
        compiler_params=_cparams(("parallel", "parallel")),
        name="post_mix",
    )(x, oda, ohg, mem_k, mem_v, w_out, ln_mem.reshape(1, D_MODEL), w_q, w_o,
      ln_ffn.reshape(1, D_MODEL), w_r, b_r)


def _experts_kernel(blk_e_ref, n_used_ref, src_next_ref, dst_prev_ref, src0_ref, dst_ref, h_hbm,
                    wg_ref, wu_ref, wd_ref, y_hbm, xbuf, ybuf, gsem, ssem, *, bm, m_tot):
    del blk_e_ref
    i = pl.program_id(0)
    n_used = n_used_ref[0]
    slot = i % 2

    def gather_row(tab_ref, r, s, priority=0):
        pltpu.make_async_copy(h_hbm.at[tab_ref[0, 0, r]], xbuf.at[s, r],
                              gsem.at[s]).start(priority=priority)

    def scatter_row(tab_ref, r, s, priority=0):
        pltpu.make_async_copy(ybuf.at[s, r], y_hbm.at[tab_ref[0, 0, r]],
                              ssem.at[s]).start(priority=priority)

    def wait_block(buf, sem, s):
        pltpu.make_async_copy(buf.at[s], buf.at[s], sem.at[s]).wait()

    def spare_fill(s):
        return pltpu.make_async_copy(ybuf.at[s], y_hbm.at[pl.ds(m_tot + s * bm, bm)], ssem.at[s])

    @pl.when(i == 0)
    def _():
        ybuf[...] = jnp.zeros(ybuf.shape, F32)
        spare_fill(0).start()
        spare_fill(1).start()
        spare_fill(1).wait()

        def body(r, c):
            gather_row(src0_ref, r, 0)
            return c
        lax.fori_loop(0, bm, body, 0)

    @pl.when(i < n_used)
    def _():
        wait_block(xbuf, gsem, slot)
        wait_block(ybuf, ssem, slot)
        x = xbuf[slot].reshape(bm, D_MODEL).astype(BF16)
        hg = _dot(x, wg_ref[0])
        hu = _dot(x, wu_ref[0])
        hb = (hg * jax.nn.sigmoid(hg) * hu).astype(BF16)
        ybuf[slot] = _dot(hb, wd_ref[0]).reshape(bm, N_SLAB, HEAD_W)
        for r in range(bm):
            gather_row(src_next_ref, r, 1 - slot, priority=r % 2)
            scatter_row(dst_prev_ref, r, 1 - slot, priority=(r + 1) % 2)

    @pl.when(i == n_used - 1)
    def _():
        def body(r, c):
            scatter_row(dst_ref, r, slot)
            return c
        lax.fori_loop(0, bm, body, 0)
        wait_block(xbuf, gsem, 1 - slot)
        wait_block(ybuf, ssem, 1 - slot)
        wait_block(ybuf, ssem, slot)


def _experts(h3, blk_e, n_used, src_tab, dst_tab, dst_prev_tab, wg, wu, wd, bm):
    n_blocks = src_tab.shape[0]
    m_tot = TOP_K * h3.shape[0]
    tab = lambda f: pl.BlockSpec((1, 1, bm), f, memory_space=pltpu.SMEM)
    cur = lambda i, e, n: (i, 0, 0)
    nxt = lambda i, e, n: (jnp.minimum(i + 1, n_blocks - 1), 0, 0)
    first = lambda i, e, n: (0, 0, 0)
    wspec = lambda a, b: pl.BlockSpec((1, a, b), lambda i, e, n: (e[i], 0, 0))
    grid_spec = pltpu.PrefetchScalarGridSpec(
        num_scalar_prefetch=2,
        grid=(n_blocks,),
        in_specs=[tab(nxt), tab(cur), tab(first), tab(cur), pl.BlockSpec(memory_space=pl.ANY),
                  wspec(D_MODEL, EXPERT_FF), wspec(D_MODEL, EXPERT_FF), wspec(EXPERT_FF, D_MODEL)],
        out_specs=pl.BlockSpec(memory_space=pl.ANY),
        scratch_shapes=[pltpu.VMEM((2, bm, N_SLAB, HEAD_W), F32), pltpu.VMEM((2, bm, N_SLAB, HEAD_W), F32),
                        pltpu.SemaphoreType.DMA((2,)), pltpu.SemaphoreType.DMA((2,))],
    )
    return pl.pallas_call(
        functools.partial(_experts_kernel, bm=bm, m_tot=m_tot),
        grid_spec=grid_spec,
        out_shape=jax.ShapeDtypeStruct((m_tot + 2 * bm, N_SLAB, HEAD_W), F32),
        compiler_params=_cparams(("arbitrary",)),
        name="experts",
    )(blk_e, n_used, src_tab, dst_prev_tab, src_tab, dst_tab, h3, wg, wu, wd)


def _combine_kernel(x_ref, y0_ref, y1_ref, route_ref, g_ref, o_ref):
    rt = route_ref[...]
    tm = x_ref.shape[0]
    y0 = y0_ref[...].reshape(tm, D_MODEL)
    y1 = y1_ref[...].reshape(tm, D_MODEL)
    y = x_ref[...] + rt[:, 2:3] * y0 + rt[:, 3:4] * y1
    o_ref[...] = _rms(y, g_ref[...])


def _combine(x2, y, route, final_g, tm):
    n = x2.shape[0]
    nt = n // tm
    return pl.pallas_call(
        _combine_kernel,
        grid=(nt,),
        in_specs=[pl.BlockSpec((tm, D_MODEL), lambda i: (i, 0)),
                  pl.BlockSpec((tm, N_SLAB, HEAD_W), lambda i: (i, 0, 0)),
                  pl.BlockSpec((tm, N_SLAB, HEAD_W), lambda i: (i + nt, 0, 0)),
                  pl.BlockSpec((tm, ROUTE_W), lambda i: (i, 0)),
                  pl.BlockSpec((1, D_MODEL), lambda i: (0, 0))],
        out_specs=pl.BlockSpec((tm, D_MODEL), lambda i: (i, 0)),
        out_shape=jax.ShapeDtypeStruct((n, D_MODEL), F32),
        compiler_params=_cparams(("parallel",)),
        name="combine",
    )(x2, y, y, route, final_g.reshape(1, D_MODEL))


def _routing_tables(route, bm):
    n = route.shape[0]
    m_tot = TOP_K * n
    flat_e = jnp.concatenate([route[:, 0], route[:, 1]]).astype(jnp.int32)
    experts = jnp.arange(N_EXPERTS, dtype=jnp.int32)
    counts = jnp.sum(flat_e[:, None] == experts[None, :], axis=0, dtype=jnp.int32)
    n_blk_e = (counts + bm - 1) // bm
    blk_end = jnp.cumsum(n_blk_e)
    n_blocks = m_tot // bm + N_EXPERTS
    blk = jnp.arange(n_blocks, dtype=jnp.int32)
    blk_e = jnp.minimum(jnp.sum(blk[:, None] >= blk_end[None, :], axis=1, dtype=jnp.int32),
                        N_EXPERTS - 1)
    r = jnp.arange(bm, dtype=jnp.int32)[None, :]
    n_pad = (n_blk_e * bm - counts)[:, None]
    pad_keys = jnp.where(r < n_pad, 2 * experts[:, None] + 1, 2 * N_EXPERTS + 1)
    keys = jnp.concatenate([2 * flat_e, pad_keys.reshape(-1)])
    vals = jnp.concatenate([jnp.arange(m_tot, dtype=jnp.int32),
                            jnp.full((N_EXPERTS * bm,), -1, jnp.int32)])
    m = lax.sort((keys, vals), num_keys=1)[1].reshape(n_blocks, bm)
    real = m >= 0
    spare = m_tot + (blk[:, None] % 2) * bm + r
    src = jnp.where(real, m % n, 0)
    dst = jnp.where(real, m, spare)
    dst_prev = jnp.concatenate([m_tot + bm + r, dst[:-1]], axis=0)
    shape = (n_blocks, 1, bm)
    return (blk_e, blk_end[N_EXPERTS - 1:].astype(jnp.int32), src.reshape(shape), dst.reshape(shape),
            dst_prev.reshape(shape))


def _layer(x, past_k, past_v, s0, mem_k, mem_v, p, cfg):
    B, T, _ = x.shape
    n = B * T
    (q, kf, vf, kb, vt, qh, lf, kk, hi, sg) = _in_proj(
        x, p["ln_mix"], p["w_in"], p["lb"], cfg["nb"], cfg["tm_proj"])
    oda = _diff_attn(q, kb, vt, past_k, past_v, p["slopes"], p["lam"], p["da_subln"],
                     cfg["tq"], cfg["tk"])
    ohg, s_new = _hgrn(qh, kk, lf, hi, sg, p["hg_gnorm"], s0, cfg["tb"])
    x2, h3, route = _post_mix(x, oda, ohg, mem_k, mem_v, p["w_out"], p["ln_mem"], p["w_mem_q"],
                              p["w_mem_o"], p["ln_ffn"], p["w_r"], p["b_r"], cfg["nb"], cfg["tm_post"],
                              cfg["split"])
    route2 = route.reshape(n, ROUTE_W)
    blk_e, n_used, src_tab, dst_tab, dst_prev_tab = _routing_tables(route2, cfg["bm"])
    y = _experts(h3.reshape(n, N_SLAB, HEAD_W), blk_e, n_used, src_tab, dst_tab, dst_prev_tab,
                 p["e_gate"], p["e_up"], p["e_down"], cfg["bm"])
    out = _combine(x2.reshape(n, D_MODEL), y, route2, p["final_g"], cfg["tm_in"])
    return out.reshape(B, T, D_MODEL), kf[None], vf[None], s_new[None]


PROMPT_CFG = dict(tm_in=256, tq=512, tk=512, tb=256, nb=1, tm_proj=256, tm_post=512, split=2, bm=512)
SAMPLE_CFG = dict(tm_in=256, tq=64, tk=64, tb=64, nb=4, tm_proj=64, tm_post=64, split=1, bm=128)


def kernel(x_prompt, x_sample, mem_prompt, cache_diff_k, cache_diff_v, state_hgrn, cache_mem_k, cache_mem_v, ln_mix_g, w_in, da_lambda, da_subln_g, hg_lb_logits, hg_gnorm_g, w_out, ln_mem_g, mem_norm_g, w_mem_q, w_mem_k, w_mem_v, w_mem_o, ln_ffn_g, router_group_w, router_group_b, router_expert_w, router_expert_b, exp_w_gate, exp_w_up, exp_w_down, final_norm_g):
    assert w_in.shape[0] == 1, "single-layer configuration"
    lb_all = jnp.cumsum(jax.nn.softmax(hg_lb_logits.astype(F32), axis=0), axis=0)
    lp = da_lambda[0].astype(F32)
    lam = jnp.exp(jnp.sum(lp[0] * lp[1])) - jnp.exp(jnp.sum(lp[2] * lp[3])) + LAM_INIT
    w_r = jnp.zeros((D_MODEL, HEAD_W), F32)
    w_r = w_r.at[:, :N_GROUPS].set(router_group_w[0]).at[:, N_GROUPS:N_GROUPS + N_EXPERTS].set(router_expert_w[0])
    b_r = jnp.zeros((1, HEAD_W), F32)
    b_r = b_r.at[0, :N_GROUPS].set(router_group_b[0]).at[0, N_GROUPS:N_GROUPS + N_EXPERTS].set(router_expert_b[0])
    p = {
        "ln_mix": ln_mix_g[0], "w_in": w_in[0].astype(BF16), "lb": lb_all[0],
        "slopes": jnp.exp2(-8.0 * jnp.arange(1, N_HEADS + 1, dtype=F32) / N_HEADS),
        "lam": lam.reshape(1), "da_subln": da_subln_g[0], "hg_gnorm": hg_gnorm_g[0],
        "w_out": w_out[0].astype(BF16), "ln_mem": ln_mem_g[0], "w_mem_q": w_mem_q[0].astype(BF16),
        "w_mem_o": w_mem_o[0].astype(BF16), "ln_ffn": ln_ffn_g[0], "w_r": w_r.astype(BF16), "b_r": b_r,
        "e_gate": exp_w_gate[0].astype(BF16), "e_up": exp_w_up[0].astype(BF16),
        "e_down": exp_w_down[0].astype(BF16), "final_g": final_norm_g,
    }
    Bp, Tp, _ = x_prompt.shape
    Bs, Ts, _ = x_sample.shape
    past_len = cache_diff_k.shape[2]

    mkf, mvf, mkb, mvb = _memory_kv(mem_prompt, mem_norm_g[0], w_mem_k[0].astype(BF16),
                                    w_mem_v[0].astype(BF16))
    zero_state = jnp.zeros((Bp, N_HEADS, HEAD_W, HEAD_W), F32)
    yp, kp, vp, sp = _layer(x_prompt, None, None, zero_state, mkb, mvb, p, PROMPT_CFG)
    ys, ks, vs, ss = _layer(
        x_sample, cache_diff_k[0].reshape(Bs, past_len, GROUP_W), cache_diff_v[0].reshape(Bs, past_len, GROUP_W),
        state_hgrn[0], cache_mem_k[0], cache_mem_v[0], p, SAMPLE_CFG)
    return (yp, ys, kp, vp, sp, mkf[None], mvf[None], ks, vs, ss)
```

```python
import functools
import math

import jax
import jax.numpy as jnp
from jax import lax
from jax.experimental import pallas as pl
from jax.experimental.pallas import tpu as pltpu

F32 = jnp.float32
BF16 = jnp.bfloat16

D_MODEL = 1024
EPS = 1e-5
CHUNK = 64
N_HEADS = 4
HEAD_W = 128
QK_DIM = 64
GROUP_W = N_HEADS * HEAD_W
N_PROJ = 7
DA_SCALE = QK_DIM ** -0.5
LOG2E = 1.4426950408889634
LAM_INIT = 0.8 - 0.6 * math.exp(-0.3 * 0)
N_MEM = 256
MEM_HEADS = 4
MEM_HEAD_DIM = D_MODEL // MEM_HEADS
MEM_SCALE = MEM_HEAD_DIM ** -0.5
N_GROUPS = 4
EXPERTS_PER_GROUP = 8
N_EXPERTS = N_GROUPS * EXPERTS_PER_GROUP
TOP_K = 2
EXPERT_FF = 512
ROUTE_W = 8
SUB = 16
SUB_KEYS = 128
N_SCORE_SLOTS = 3
N_SLAB = D_MODEL // HEAD_W
NEG_INF = float("-inf")

VMEM_LIMIT = 48 * 1024 * 1024


def _cparams(sem):
    return pltpu.CompilerParams(dimension_semantics=sem, vmem_limit_bytes=VMEM_LIMIT)


def _rms(x, g):
    return x * lax.rsqrt(jnp.mean(x * x, axis=-1, keepdims=True) + EPS) * g


def _dot(a, b):
    return jnp.dot(a, b, preferred_element_type=F32)


def _dot_nt(a, b):
    return lax.dot_general(a, b, (((1,), (1,)), ((), ())), preferred_element_type=F32)


def _dot_tn(a, b):
    return lax.dot_general(a, b, (((0,), (0,)), ((), ())), preferred_element_type=F32)


def _run_skewed(gens):
    waiting, live = list(gens), []
    while waiting or live:
        if waiting:
            live.append(waiting.pop(0))
        for gen in list(live):
            if next(gen, "done") == "done":
                live.remove(gen)


def _memkv_kernel(m_ref, g_ref, wk_ref, wv_ref, kf_ref, vf_ref, kb_ref, vb_ref):
    mn = _rms(m_ref[0], g_ref[...]).astype(BF16)
    k = _dot(mn, wk_ref[...])
    v = _dot(mn, wv_ref[...])
    kf_ref[0] = k.reshape(N_MEM, MEM_HEADS, MEM_HEAD_DIM)
    vf_ref[0] = v.reshape(N_MEM, MEM_HEADS, MEM_HEAD_DIM)
    kb_ref[0] = k.astype(BF16)
    vb_ref[0] = v.astype(BF16)


def _memory_kv(mem, g, wk, wv):
    B = mem.shape[0]
    blk = pl.BlockSpec((1, N_MEM, D_MODEL), lambda b: (b, 0, 0))
    hblk = pl.BlockSpec((1, N_MEM, MEM_HEADS, MEM_HEAD_DIM), lambda b: (b, 0, 0, 0))
    wspec = pl.BlockSpec((D_MODEL, D_MODEL), lambda b: (0, 0))
    return pl.pallas_call(
        _memkv_kernel,
        grid=(B,),
        in_specs=[blk, pl.BlockSpec((1, D_MODEL), lambda b: (0, 0)), wspec, wspec],
        out_specs=[hblk, hblk, blk, blk],
        out_shape=[jax.ShapeDtypeStruct((B, N_MEM, MEM_HEADS, MEM_HEAD_DIM), F32)] * 2
                  + [jax.ShapeDtypeStruct(mem.shape, BF16)] * 2,
        compiler_params=_cparams(("parallel",)),
        name="memory_kv",
    )(mem, g.reshape(1, D_MODEL), wk, wv)


def _in_proj_kernel(x_ref, g_ref, w_ref, lb_ref, q_ref, kf_ref, vf_ref, kb_ref, vt_ref,
                    qh_ref, lf_ref, kk_ref, hi_ref, sg_ref, *, nb, tm):
    rows = nb * tm
    h = _rms(x_ref[...].reshape(rows, D_MODEL), g_ref[...]).astype(BF16)

    def proj(i):
        return _dot(h, w_ref[:, i * GROUP_W:(i + 1) * GROUP_W])

    def put(ref, val):
        ref[...] = val.reshape(nb, tm, GROUP_W).astype(ref.dtype)

    def put_heads(ref, val):
        ref[...] = val.reshape(nb, tm, N_HEADS, HEAD_W)

    put(q_ref, proj(0) * (DA_SCALE * LOG2E))
    dk = proj(1)
    put_heads(kf_ref, dk)
    put(kb_ref, dk)
    dv = proj(2)
    put_heads(vf_ref, dv)
    for b in range(nb):
        vt_ref[b] = dv[b * tm:(b + 1) * tm].T.astype(BF16)
    hq = proj(3)
    put(qh_ref, hq * jax.nn.sigmoid(hq))
    hf = proj(4)
    lb = lb_ref[...]
    put(lf_ref, jnp.log(lb + (1.0 - lb) * jax.nn.sigmoid(hf)))
    put(kk_ref, (1.0 - lb) * jax.nn.sigmoid(-hf))
    put(hi_ref, proj(5))
    hg = proj(6)
    put(sg_ref, hg * jax.nn.sigmoid(hg))


def _in_proj(x, g, w_bf, lb, nb, tm):
    B, T, _ = x.shape
    tok = lambda w: pl.BlockSpec((nb, tm, w), lambda b, t: (b, t, 0))
    fix = lambda s: pl.BlockSpec(s, lambda b, t: (0, 0))
    f32o = jax.ShapeDtypeStruct((B, T, GROUP_W), F32)
    bfo = jax.ShapeDtypeStruct((B, T, GROUP_W), BF16)
    o_spec = tok(GROUP_W)
    ho = jax.ShapeDtypeStruct((B, T, N_HEADS, HEAD_W), F32)
    h_spec = pl.BlockSpec((nb, tm, N_HEADS, HEAD_W), lambda b, t: (b, t, 0, 0))
    vt_spec = pl.BlockSpec((nb, GROUP_W, tm), lambda b, t: (b, 0, t))
    return pl.pallas_call(
        functools.partial(_in_proj_kernel, nb=nb, tm=tm),
        grid=(B // nb, T // tm),
        in_specs=[tok(D_MODEL), fix((1, D_MODEL)), fix((D_MODEL, N_PROJ * GROUP_W)), fix((1, GROUP_W))],
        out_specs=[o_spec, h_spec, h_spec, o_spec, vt_spec] + [o_spec] * 5,
        out_shape=[bfo, ho, ho, bfo, jax.ShapeDtypeStruct((B, GROUP_W, T), BF16),
                   bfo, f32o, bfo, bfo, bfo],
        compiler_params=_cparams(("parallel", "parallel")),
        name="in_proj",
    )(x, g.reshape(1, D_MODEL), w_bf, lb.reshape(1, GROUP_W))


def _attn_kernel(*refs, tq, tk, past_len, tkp):
    if past_len:
        (slope_ref, lam_ref, q_ref, k_ref, vt_ref, pk_ref, pv_ref, g_ref, o_ref,
         m_ref, l_ref, acc_ref, base_ref, t_ref, mb_ref) = refs
    else:
        (slope_ref, lam_ref, q_ref, k_ref, vt_ref, g_ref, o_ref,
         m_ref, l_ref, acc_ref, base_ref, t_ref, mb_ref) = refs
    h = pl.program_id(1)
    qi = pl.program_id(2)
    slope2 = slope_ref[h] * LOG2E
    qt = q_ref[0].astype(F32).T.astype(BF16)
    zeros = jnp.zeros((QK_DIM, tq), BF16)
    qt_maps = (jnp.concatenate([qt[:QK_DIM], zeros], axis=0),
               jnp.concatenate([zeros, qt[QK_DIM:]], axis=0))
    m_ref[...] = jnp.full(m_ref.shape, NEG_INF, F32)
    l_ref[...] = jnp.zeros(l_ref.shape, F32)
    acc_ref[...] = jnp.zeros(acc_ref.shape, F32)
    q0 = past_len + qi * tq

    def lane_pad(x, lo, fill):
        if lo == 0:
            return x
        return jnp.concatenate([jnp.full((x.shape[0], lo), fill, x.dtype), x], axis=1)


    def score_phase(slot, n_keys, get_k, get_bias, shift, q_lo=lambda u: 0):
        sub = min(n_keys, SUB_KEYS)
        mx = [None, None]
        for u in range(n_keys // sub):
            lo = q_lo(u)
            bias, visible = get_bias(u)
            k = get_k(u)
            for c in range(2):
                t = _dot(k, qt_maps[c][:, lo:]) + bias
                if visible is not None:
                    w = visible.shape[1]
                    head = jnp.where(visible, t[:, :w], NEG_INF)
                    t = head if w == t.shape[1] else jnp.concatenate([head, t[:, w:]], axis=1)
                t_ref[slot, c, u * sub:(u + 1) * sub, lo:] = t
                mu = lane_pad(jnp.max(t, axis=0, keepdims=True), lo, NEG_INF)
                mx[c] = mu if mx[c] is None else jnp.maximum(mx[c], mu)
            yield
        for c in range(2):
            mb_ref[slot, c] = mx[c] + shift
        yield

    def prob_phase(slot, n_keys, get_vt, shift, q_lo=lambda u: 0, pv_group=1):
        sub = min(n_keys, SUB_KEYS)
        n_sub = n_keys // sub
        off, alpha = [], []
        for c in range(2):
            m_old = m_ref[c]
            m_new = jnp.maximum(m_old, mb_ref[slot, c])
            off.append(m_new - shift)
            alpha.append(jnp.exp2(m_old - m_new))
            m_ref[c] = m_new
        yield
        lsum = [None, None]
        pv = [None, None]
        grp = pv_group if n_sub % pv_group == 0 else 1
        for u in range(0, n_sub, grp):
            lo = q_lo(u)
            assert all(q_lo(u + i) == lo for i in range(grp))
            parts = [get_vt(u + i) for i in range(grp)]
            vt = parts[0] if grp == 1 else jnp.concatenate(parts, axis=1)
            for c in range(2):
                p = jnp.exp2(t_ref[slot, c, u * sub:(u + grp) * sub, lo:] - off[c][:, lo:])
                ls = lane_pad(jnp.sum(p, axis=0, keepdims=True), lo, 0.0)
                pu = lane_pad(_dot(vt, p.astype(BF16)), lo, 0.0)
                lsum[c] = ls if lsum[c] is None else lsum[c] + ls
                pv[c] = pu if pv[c] is None else pv[c] + pu
            yield
        for c in range(2):
            l_ref[c] = alpha[c] * l_ref[c] + lsum[c]
            acc_ref[c] = alpha[c] * acc_ref[c] + pv[c]
        yield

    def run(*gens):
        live = list(gens)
        while live:
            for gen in list(live):
                if next(gen, "done") == "done":
                    live.remove(gen)

    def update(n_keys, get_k, get_vt, get_bias, shift, q_lo=lambda u: 0, pv_group=1):
        run(score_phase(0, n_keys, get_k, get_bias, shift, q_lo))
        run(prob_phase(0, n_keys, get_vt, shift, q_lo, pv_group))

    def rel_bias(n_keys, key0):
        key = lax.broadcasted_iota(jnp.int32, (n_keys, tq), 0) + key0
        qry = lax.broadcasted_iota(jnp.int32, (n_keys, tq), 1)
        return (qry - key).astype(F32) * (-slope2)

    def block_shift(k0):
        return (q0 - k0).astype(F32) * (-slope2)

    if past_len:
        sub_p = min(tkp, SUB_KEYS)

        def past_body(j, carry):
            k0 = pl.multiple_of(j * tkp, tkp)
            update(tkp,
                   lambda u: pk_ref[0, pl.ds(k0 + u * sub_p, sub_p), :].astype(BF16),
                   lambda u: pv_ref[0, pl.ds(k0 + u * sub_p, sub_p), :].T.astype(BF16),
                   lambda u: (rel_bias(sub_p, u * sub_p), None), block_shift(k0))
            return carry
        lax.fori_loop(0, past_len // tkp, past_body, 0)

    n_q = k_ref.shape[1] // tq
    if n_q > 1 or tq > SUB_KEYS:
        @pl.when(qi == 0)
        def _():
            base_ref[...] = rel_bias(base_ref.shape[0], 0)

    kd = 0 if n_q == 1 else pl.multiple_of(qi * tq, tq)
    sub_d = min(tq, SUB_KEYS)
    key = lax.broadcasted_iota(jnp.int32, (sub_d, sub_d), 0)
    qry = lax.broadcasted_iota(jnp.int32, (sub_d, sub_d), 1)
    strip_bias = jnp.abs(qry - key).astype(F32) * (-slope2)
    strip_visible = (key // CHUNK) <= (qry // CHUNK)

    def diag_bias(u):
        lo = (u + 1) * sub_d
        if lo == tq:
            return strip_bias, strip_visible
        later = base_ref[u * sub_d:(u + 1) * sub_d, lo:]
        return jnp.concatenate([strip_bias, later], axis=1), strip_visible

    diag_k = lambda u: k_ref[0, pl.ds(kd + u * sub_d, sub_d), :]
    diag_vt = lambda u: vt_ref[0, :, pl.ds(kd + u * sub_d, sub_d)]
    diag_lo = lambda u: u * sub_d

    if n_q == 1:
        update(tq, diag_k, diag_vt, diag_bias, 0.0, q_lo=diag_lo)
    else:
        sub_k = min(tk, SUB_KEYS)
        n_prev = qi * (tq // tk)
        assert (tq // tk) % 2 == 0, "earlier blocks are consumed in pairs"

        def prev_scores(slot, j):
            k0 = pl.multiple_of(j * tk, tk)
            return score_phase(slot, tk, lambda u: k_ref[0, pl.ds(k0 + u * sub_k, sub_k), :],
                               lambda u: (base_ref[u * sub_k:(u + 1) * sub_k, :], None),
                               block_shift(past_len + k0))

        def prev_probs(slot, j):
            k0 = pl.multiple_of(j * tk, tk)
            return prob_phase(slot, tk, lambda u: vt_ref[0, :, pl.ds(k0 + u * sub_k, sub_k)],
                              block_shift(past_len + k0), pv_group=2)

        run(score_phase(0, tq, diag_k, diag_bias, 0.0, diag_lo))
        run(prob_phase(0, tq, diag_vt, 0.0, diag_lo), prev_scores(1, 0))

        def pair_body(i, carry):
            j = 2 * i
            run(prev_probs(1, j), prev_scores(2, j + 1))
            run(prev_probs(2, j + 1), prev_scores(1, jnp.minimum(j + 2, n_prev - 1)))
            return carry
        lax.fori_loop(0, n_prev // 2, pair_body, 0)

    lam = lam_ref[0]
    ot = acc_ref[0] / l_ref[0] - lam * (acc_ref[1] / l_ref[1])
    ot = ot * lax.rsqrt(jnp.mean(ot * ot, axis=0, keepdims=True) + EPS)
    o_ref[0] = (ot.T * g_ref[...] * (1.0 - LAM_INIT)).astype(o_ref.dtype)


def _diff_attn(q, k, vt, past_k, past_v, slopes, lam, subln_g, tq, tk):
    B, T, _ = q.shape
    past_len = 0 if past_k is None else past_k.shape[1]
    tkp = min(512, past_len) if past_len else 0
    smem = pl.BlockSpec(memory_space=pltpu.SMEM)
    qspec = pl.BlockSpec((1, tq, HEAD_W), lambda b, h, i: (b, i, h))
    in_specs = [smem, smem, qspec,
                pl.BlockSpec((1, T, HEAD_W), lambda b, h, i: (b, 0, h)),
                pl.BlockSpec((1, HEAD_W, T), lambda b, h, i: (b, h, 0))]
    args = [slopes, lam, q, k, vt]
    if past_len:
        pspec = pl.BlockSpec((1, past_len, HEAD_W), lambda b, h, i: (b, 0, h))
        in_specs += [pspec, pspec]
        args += [past_k, past_v]
    in_specs.append(pl.BlockSpec((1, HEAD_W), lambda b, h, i: (0, 0)))
    args.append(subln_g.reshape(1, HEAD_W))
    return pl.pallas_call(
        functools.partial(_attn_kernel, tq=tq, tk=tk, past_len=past_len, tkp=tkp),
        grid=(B, N_HEADS, T // tq),
        in_specs=in_specs,
        out_specs=qspec,
        out_shape=jax.ShapeDtypeStruct((B, T, GROUP_W), BF16),
        scratch_shapes=[pltpu.VMEM((2, 1, tq), F32), pltpu.VMEM((2, 1, tq), F32),
                        pltpu.VMEM((2, HEAD_W, tq), F32), pltpu.VMEM((max(tk, tq), tq), F32),
                        pltpu.VMEM((N_SCORE_SLOTS, 2, max(tk, tq, tkp), tq), F32),
                        pltpu.VMEM((N_SCORE_SLOTS, 2, 1, tq), F32)],
        compiler_params=_cparams(("parallel", "parallel", "arbitrary")),
        name="diff_attn",
    )(*args)


def _split3(x):
    hi = x.astype(BF16)
    r = x - hi.astype(F32)
    mid = r.astype(BF16)
    lo = (r - mid.astype(F32)).astype(BF16)
    return hi, mid, lo


def _hgrn_kernel(q_ref, k_ref, lf_ref, v_ref, sg_ref, gn_ref, s0_ref, o_ref, sfin_ref,
                 st_ref, b_ref, *, tb):
    t = pl.program_id(1)
    n_chunks = tb // CHUNK

    @pl.when(t == 0)
    def _():
        for h in range(N_HEADS):
            st_ref[h] = s0_ref[0, h].T

    row = lax.broadcasted_iota(jnp.int32, (tb, tb), 0)
    col = lax.broadcasted_iota(jnp.int32, (tb, tb), 1)
    tri = jnp.where((col <= row) & (col // CHUNK == row // CHUNK), 1.0, 0.0).astype(BF16)
    hi, mid, lo = _split3(lf_ref[0] * LOG2E)
    b_ref[...] = _dot(tri, hi) + _dot(tri, mid) + _dot(tri, lo)

    sub_row = lax.broadcasted_iota(jnp.int32, (8, HEAD_W), 0)
    zeros16 = jnp.zeros((SUB, HEAD_W), F32)

    def pad_rows(x, r0):
        parts = []
        if r0:
            parts.append(jnp.zeros((r0, HEAD_W), F32))
        parts.append(x)
        rest = CHUNK - r0 - x.shape[0]
        if rest:
            parts.append(jnp.zeros((rest, HEAD_W), F32))
        return jnp.concatenate(parts, axis=0) if len(parts) > 1 else x

    def chunk_body(c, carry):
        r0 = pl.multiple_of(c * CHUNK, CHUNK)
        def head(h):
            hs = slice(h * HEAD_W, (h + 1) * HEAD_W)
            q = q_ref[0, pl.ds(r0, CHUNK), hs].astype(F32)
            k = k_ref[0, pl.ds(r0, CHUNK), hs].astype(F32)
            v_bf = v_ref[0, pl.ds(r0, CHUNK), hs]
            v = v_bf.astype(F32)
            b = b_ref[pl.ds(r0, CHUNK), hs]
            b_last = b[CHUNK - 1:CHUNK, :]
            st = st_ref[h]

            o = _dot_nt((q * jnp.exp2(b)).astype(BF16), st.astype(BF16))

            q_segs, k_segs = [], []
            for i in range(1, CHUNK // SUB):
                lo_r, hi_r = i * SUB, (i + 1) * SUB
                ref_b = b[lo_r - 1:lo_r, :]
                q_segs.append(pad_rows(q[lo_r:hi_r] * jnp.exp2(b[lo_r:hi_r] - ref_b), lo_r))
                k_segs.append(pad_rows(k[:lo_r] * jnp.exp2(ref_b - b[:lo_r]), 0))
            a_off = _dot_nt(jnp.concatenate(q_segs, axis=1).astype(BF16),
                            jnp.concatenate(k_segs, axis=1).astype(BF16))
            o = o + _dot(a_off.astype(BF16), v_bf)
            yield

            diag = []
            for blk in range(CHUNK // SUB):
                base = blk * SUB
                qa, qb = q[base:base + 8], q[base + 8:base + SUB]
                ba, bb = b[base:base + 8], b[base + 8:base + SUB]
                oa = jnp.zeros((8, HEAD_W), F32)
                ob = jnp.zeros((8, HEAD_W), F32)
                for s in range(SUB):
                    ks = k[base + s:base + s + 1]
                    bs = b[base + s:base + s + 1]
                    vs = v[base + s:base + s + 1]
                    if s < 8:
                        e = jnp.exp2(ba - bs)
                        if s:
                            e = jnp.where(sub_row >= s, e, 0.0)
                        oa = oa + jnp.sum(qa * ks * e, axis=-1, keepdims=True) * vs
                        ob = ob + jnp.sum(qb * ks * jnp.exp2(bb - bs), axis=-1, keepdims=True) * vs
                    else:
                        e = jnp.exp2(bb - bs)
                        if s > 8:
                            e = jnp.where(sub_row >= s - 8, e, 0.0)
                        ob = ob + jnp.sum(qb * ks * e, axis=-1, keepdims=True) * vs
                diag += [oa, ob]
                yield
            o = o + jnp.concatenate(diag, axis=0)

            k_dec = (k * jnp.exp2(b_last - b)).astype(BF16)
            st_ref[h] = st * jnp.exp2(b_last) + _dot_tn(v_bf, k_dec)

            out = _rms(o, gn_ref[...]) * sg_ref[0, pl.ds(r0, CHUNK), hs]
            o_ref[0, pl.ds(r0, CHUNK), hs] = out.astype(o_ref.dtype)
            yield

        _run_skewed([head(h) for h in range(N_HEADS)])
        return carry

    lax.fori_loop(0, n_chunks, chunk_body, 0)

    @pl.when(t == pl.num_programs(1) - 1)
    def _():
        for h in range(N_HEADS):
            sfin_ref[0, h] = st_ref[h].T


def _hgrn(qh, kk, lf, hi, sg, gnorm_g, s0, tb):
    B, T, _ = qh.shape
    tspec = pl.BlockSpec((1, tb, GROUP_W), lambda b, t: (b, t, 0))
    sspec = pl.BlockSpec((1, N_HEADS, HEAD_W, HEAD_W), lambda b, t: (b, 0, 0, 0))
    return pl.pallas_call(
        functools.partial(_hgrn_kernel, tb=tb),
        grid=(B, T // tb),
        in_specs=[tspec] * 5 + [pl.BlockSpec((1, HEAD_W), lambda b, t: (0, 0)), sspec],
        out_specs=[tspec, sspec],
        out_shape=[jax.ShapeDtypeStruct((B, T, GROUP_W), BF16),
                   jax.ShapeDtypeStruct((B, N_HEADS, HEAD_W, HEAD_W), F32)],
        scratch_shapes=[pltpu.VMEM((N_HEADS, HEAD_W, HEAD_W), F32), pltpu.VMEM((tb, GROUP_W), F32)],
        compiler_params=_cparams(("parallel", "arbitrary")),
        name="hgrn",
    )(qh, kk, lf, hi, sg, gnorm_g.reshape(1, HEAD_W), s0)


def _post_kernel(x_ref, oda_ref, ohg_ref, mk_ref, mv_ref, wout_ref, lnm_ref, wq_ref, wo_ref,
                 lnf_ref, wr_ref, br_ref, x2_ref, h3_ref, route_ref, om_ref, *, nb, tm, split):
    gt = tm // split
    groups = [_post_group(slice(g * gt, (g + 1) * gt), g * nb * gt, x_ref, oda_ref, ohg_ref,
                          mk_ref, mv_ref, wout_ref, lnm_ref, wq_ref, wo_ref, lnf_ref, wr_ref,
                          br_ref, x2_ref, h3_ref, route_ref, om_ref, nb=nb, gt=gt)
              for g in range(split)]
    _run_skewed(groups)


def _post_group(ts, om0, x_ref, oda_ref, ohg_ref, mk_ref, mv_ref, wout_ref, lnm_ref, wq_ref, wo_ref,
                lnf_ref, wr_ref, br_ref, x2_ref, h3_ref, route_ref, om_ref, *, nb, gt):
    rows = nb * gt
    x = x_ref[:, ts, :].reshape(rows, D_MODEL)
    mixed = (_dot(oda_ref[:, ts, :].reshape(rows, GROUP_W), wout_ref[:GROUP_W, :])
             + _dot(ohg_ref[:, ts, :].reshape(rows, GROUP_W), wout_ref[GROUP_W:, :]))
    x1 = x + mixed
    yield

    hm = _rms(x1, lnm_ref[...]).astype(BF16)
    yield
    qm = (_dot(hm, wq_ref[...]) * MEM_SCALE).astype(BF16)
    yield
    for b in range(nb):
        for h in range(MEM_HEADS):
            hs = slice(h * MEM_HEAD_DIM, (h + 1) * MEM_HEAD_DIM)
            if len(mk_ref.shape) == 4:
                mk, mv = mk_ref[b, :, h, :], mv_ref[b, :, h, :]
            else:
                mk, mv = mk_ref[b, :, hs], mv_ref[b, :, hs]
            s = _dot_nt(qm[b * gt:(b + 1) * gt, hs], mk.astype(BF16))
            e = jnp.exp(s - jnp.max(s, axis=-1, keepdims=True))
            p = e / jnp.sum(e, axis=-1, keepdims=True)
            om_ref[om0 + b * gt:om0 + (b + 1) * gt, hs] = _dot(
                p.astype(BF16), mv.astype(BF16)).astype(BF16)
    yield
    x2 = x1 + _dot(om_ref[om0:om0 + rows, :], wo_ref[...])
    x2_ref[:, ts, :] = x2.reshape(nb, gt, D_MODEL)
    yield

    h3 = _rms(x2, lnf_ref[...])
    h3_ref[:, ts] = h3.reshape(nb, gt, N_SLAB, HEAD_W)
    yield

    r = _dot(h3.astype(BF16), wr_ref[...]) + br_ref[...]
    lane = lax.broadcasted_iota(jnp.int32, r.shape, 1).astype(F32)
    big = float(4 * HEAD_W)
    g_mask = lane < N_GROUPS
    gl = jnp.where(g_mask, r, NEG_INF)
    g_max = jnp.max(gl, axis=-1, keepdims=True)
    g_idx = jnp.min(jnp.where(gl == g_max, lane, big), axis=-1, keepdims=True)
    g_w = 1.0 / jnp.sum(jnp.where(g_mask, jnp.exp(r - g_max), 0.0), axis=-1, keepdims=True)
    e_lo = N_GROUPS + EXPERTS_PER_GROUP * g_idx
    el = jnp.where((lane >= e_lo) & (lane < e_lo + EXPERTS_PER_GROUP), r, NEG_INF)
    v1 = jnp.max(el, axis=-1, keepdims=True)
    i1 = jnp.min(jnp.where(el == v1, lane, big), axis=-1, keepdims=True)
    el2 = jnp.where(lane == i1, NEG_INF, el)
    v2 = jnp.max(el2, axis=-1, keepdims=True)
    i2 = jnp.min(jnp.where(el2 == v2, lane, big), axis=-1, keepdims=True)
    t = jnp.exp(v2 - v1)
    p1 = 1.0 / (1.0 + t)
    rec = jnp.where(lane == 0.0, i1 - N_GROUPS,
          jnp.where(lane == 1.0, i2 - N_GROUPS,
          jnp.where(lane == 2.0, p1 * g_w,
          jnp.where(lane == 3.0, t * p1 * g_w, 0.0))))
    route_ref[:, ts, :] = rec[:, :ROUTE_W].reshape(nb, gt, ROUTE_W)


def _post_mix(x, oda, ohg, mem_k, mem_v, w_out, ln_mem, w_q, w_o, ln_ffn, w_r, b_r, nb, tm, split):
    B, T, _ = x.shape
    tok = lambda w: pl.BlockSpec((nb, tm, w), lambda b, t: (b, t, 0))
    if mem_k.ndim == 4:
        memspec = pl.BlockSpec((nb, N_MEM, MEM_HEADS, MEM_HEAD_DIM), lambda b, t: (b, 0, 0, 0))
    else:
        memspec = pl.BlockSpec((nb, N_MEM, D_MODEL), lambda b, t: (b, 0, 0))
    fix = lambda s: pl.BlockSpec(s, lambda b, t: (0, 0))
    return pl.pallas_call(
        functools.partial(_post_kernel, nb=nb, tm=tm, split=split),
        grid=(B // nb, T // tm),
        in_specs=[tok(D_MODEL), tok(GROUP_W), tok(GROUP_W), memspec, memspec,
                  fix((D_MODEL, D_MODEL)), fix((1, D_MODEL)), fix((D_MODEL, D_MODEL)),
                  fix((D_MODEL, D_MODEL)), fix((1, D_MODEL)), fix((D_MODEL, HEAD_W)),
                  fix((1, HEAD_W))],
        out_specs=[tok(D_MODEL),
                   pl.BlockSpec((nb, tm, N_SLAB, HEAD_W), lambda b, t: (b, t, 0, 0)),
                   tok(ROUTE_W)],
        out_shape=[jax.ShapeDtypeStruct((B, T, D_MODEL), F32),
                   jax.ShapeDtypeStruct((B, T, N_SLAB, HEAD_W), F32),
                   jax.ShapeDtypeStruct((B, T, ROUTE_W), F32)],
        scratch_shapes=[pltpu.VMEM((nb * tm, D_MODEL), BF16)],
        compiler_params=_cparams(("parallel", "parallel")),
        name="post_mix",
    )(x, oda, ohg, mem_k, mem_v, w_out, ln_mem.reshape(1, D_MODEL), w_q, w_o,
      ln_ffn.reshape(1, D_MODEL), w_r, b_r)


def _experts_kernel(blk_e_ref, n_used_ref, src_next_ref, dst_prev_ref, src0_ref, dst_ref, h_hbm,
                    wg_ref, wu_ref, wd_ref, y_hbm, xbuf, ybuf, gsem, ssem, *, bm, m_tot):
    del blk_e_ref
    i = pl.program_id(0)
    n_used = n_used_ref[0]
    slot = i % 2

    def gather_row(tab_ref, r, s, priority=0):
        pltpu.make_async_copy(h_hbm.at[tab_ref[0, 0, r]], xbuf.at[s, r],
                              gsem.at[s]).start(priority=priority)

    def scatter_row(tab_ref, r, s, priority=0):
        pltpu.make_async_copy(ybuf.at[s, r], y_hbm.at[tab_ref[0, 0, r]],
                              ssem.at[s]).start(priority=priority)

    def wait_block(buf, sem, s):
        pltpu.make_async_copy(buf.at[s], buf.at[s], sem.at[s]).wait()

    def spare_fill(s):
        return pltpu.make_async_copy(ybuf.at[s], y_hbm.at[pl.ds(m_tot + s * bm, bm)], ssem.at[s])

    @pl.when(i == 0)
    def _():
        ybuf[...] = jnp.zeros(ybuf.shape, F32)
        spare_fill(0).start()
        spare_fill(1).start()
        spare_fill(1).wait()

        def body(r, c):
            gather_row(src0_ref, r, 0)
            return c
        lax.fori_loop(0, bm, body, 0)

    @pl.when(i < n_used)
    def _():
        wait_block(xbuf, gsem, slot)
        wait_block(ybuf, ssem, slot)
        x = xbuf[slot].reshape(bm, D_MODEL).astype(BF16)
        hg = _dot(x, wg_ref[0])
        hu = _dot(x, wu_ref[0])
        hb = (hg * jax.nn.sigmoid(hg) * hu).astype(BF16)
        ybuf[slot] = _dot(hb, wd_ref[0]).reshape(bm, N_SLAB, HEAD_W)
        for r in range(bm):
            gather_row(src_next_ref, r, 1 - slot, priority=r % 2)
            scatter_row(dst_prev_ref, r, 1 - slot, priority=(r + 1) % 2)

    @pl.when(i == n_used - 1)
    def _():
        def body(r, c):
            scatter_row(dst_ref, r, slot)
            return c
        lax.fori_loop(0, bm, body, 0)
        wait_block(xbuf, gsem, 1 - slot)
        wait_block(ybuf, ssem, 1 - slot)
        wait_block(ybuf, ssem, slot)


def _experts(h3, blk_e, n_used, src_tab, dst_tab, dst_prev_tab, wg, wu, wd, bm):
    n_blocks = src_tab.shape[0]
    m_tot = TOP_K * h3.shape[0]
    tab = lambda f: pl.BlockSpec((1, 1, bm), f, memory_space=pltpu.SMEM)
    cur = lambda i, e, n: (i, 0, 0)
    nxt = lambda i, e, n: (jnp.minimum(i + 1, n_blocks - 1), 0, 0)
    first = lambda i, e, n: (0, 0, 0)
    wspec = lambda a, b: pl.BlockSpec((1, a, b), lambda i, e, n: (e[i], 0, 0))
    grid_spec = pltpu.PrefetchScalarGridSpec(
        num_scalar_prefetch=2,
        grid=(n_blocks,),
        in_specs=[tab(nxt), tab(cur), tab(first), tab(cur), pl.BlockSpec(memory_space=pl.ANY),
                  wspec(D_MODEL, EXPERT_FF), wspec(D_MODEL, EXPERT_FF), wspec(EXPERT_FF, D_MODEL)],
        out_specs=pl.BlockSpec(memory_space=pl.ANY),
        scratch_shapes=[pltpu.VMEM((2, bm, N_SLAB, HEAD_W), F32), pltpu.VMEM((2, bm, N_SLAB, HEAD_W), F32),
                        pltpu.SemaphoreType.DMA((2,)), pltpu.SemaphoreType.DMA((2,))],
    )
    return pl.pallas_call(
        functools.partial(_experts_kernel, bm=bm, m_tot=m_tot),
        grid_spec=grid_spec,
        out_shape=jax.ShapeDtypeStruct((m_tot + 2 * bm, N_SLAB, HEAD_W), F32),
        compiler_params=_cparams(("arbitrary",)),
        name="experts",
    )(blk_e, n_used, src_tab, dst_prev_tab, src_tab, dst_tab, h3, wg, wu, wd)


def _combine_kernel(x_ref, y0_ref, y1_ref, route_ref, g_ref, o_ref):
    rt = route_ref[...]
    tm = x_ref.shape[0]
    y0 = y0_ref[...].reshape(tm, D_MODEL)
    y1 = y1_ref[...].reshape(tm, D_MODEL)
    y = x_ref[...] + rt[:, 2:3] * y0 + rt[:, 3:4] * y1
    o_ref[...] = _rms(y, g_ref[...])


def _combine(x2, y, route, final_g, tm):
    n = x2.shape[0]
    nt = n // tm
    return pl.pallas_call(
        _combine_kernel,
        grid=(nt,),
        in_specs=[pl.BlockSpec((tm, D_MODEL), lambda i: (i, 0)),
                  pl.BlockSpec((tm, N_SLAB, HEAD_W), lambda i: (i, 0, 0)),
                  pl.BlockSpec((tm, N_SLAB, HEAD_W), lambda i: (i + nt, 0, 0)),
                  pl.BlockSpec((tm, ROUTE_W), lambda i: (i, 0)),
                  pl.BlockSpec((1, D_MODEL), lambda i: (0, 0))],
        out_specs=pl.BlockSpec((tm, D_MODEL), lambda i: (i, 0)),
        out_shape=jax.ShapeDtypeStruct((n, D_MODEL), F32),
        compiler_params=_cparams(("parallel",)),
        name="combine",
    )(x2, y, y, route, final_g.reshape(1, D_MODEL))


def _routing_tables(route, bm):
    n = route.shape[0]
    m_tot = TOP_K * n
    flat_e = jnp.concatenate([route[:, 0], route[:, 1]]).astype(jnp.int32)
    experts = jnp.arange(N_EXPERTS, dtype=jnp.int32)
    counts = jnp.sum(flat_e[:, None] == experts[None, :], axis=0, dtype=jnp.int32)
    n_blk_e = (counts + bm - 1) // bm
    blk_end = jnp.cumsum(n_blk_e)
    n_blocks = m_tot // bm + N_EXPERTS
    blk = jnp.arange(n_blocks, dtype=jnp.int32)
    blk_e = jnp.minimum(jnp.sum(blk[:, None] >= blk_end[None, :], axis=1, dtype=jnp.int32),
                        N_EXPERTS - 1)
    r = jnp.arange(bm, dtype=jnp.int32)[None, :]
    n_pad = (n_blk_e * bm - counts)[:, None]
    pad_keys = jnp.where(r < n_pad, 2 * experts[:, None] + 1, 2 * N_EXPERTS + 1)
    keys = jnp.concatenate([2 * flat_e, pad_keys.reshape(-1)])
    vals = jnp.concatenate([jnp.arange(m_tot, dtype=jnp.int32),
                            jnp.full((N_EXPERTS * bm,), -1, jnp.int32)])
    m = lax.sort((keys, vals), num_keys=1)[1].reshape(n_blocks, bm)
    real = m >= 0
    spare = m_tot + (blk[:, None] % 2) * bm + r
    src = jnp.where(real, m % n, 0)
    dst = jnp.where(real, m, spare)
    dst_prev = jnp.concatenate([m_tot + bm + r, dst[:-1]], axis=0)
    shape = (n_blocks, 1, bm)
    return (blk_e, blk_end[N_EXPERTS - 1:].astype(jnp.int32), src.reshape(shape), dst.reshape(shape),
            dst_prev.reshape(shape))


def _layer(x, past_k, past_v, s0, mem_k, mem_v, p, cfg):
    B, T, _ = x.shape
    n = B * T
    (q, kf, vf, kb, vt, qh, lf, kk, hi, sg) = _in_proj(
        x, p["ln_mix"], p["w_in"], p["lb"], cfg["nb"], cfg["tm_proj"])
    oda = _diff_attn(q, kb, vt, past_k, past_v, p["slopes"], p["lam"], p["da_subln"],
                     cfg["tq"], cfg["tk"])
    ohg, s_new = _hgrn(qh, kk, lf, hi, sg, p["hg_gnorm"], s0, cfg["tb"])
    x2, h3, route = _post_mix(x, oda, ohg, mem_k, mem_v, p["w_out"], p["ln_mem"], p["w_mem_q"],
                              p["w_mem_o"], p["ln_ffn"], p["w_r"], p["b_r"], cfg["nb"], cfg["tm_post"],
                              cfg["split"])
    route2 = route.reshape(n, ROUTE_W)
    blk_e, n_used, src_tab, dst_tab, dst_prev_tab = _routing_tables(route2, cfg["bm"])
    y = _experts(h3.reshape(n, N_SLAB, HEAD_W), blk_e, n_used, src_tab, dst_tab, dst_prev_tab,
                 p["e_gate"], p["e_up"], p["e_down"], cfg["bm"])
    out = _combine(x2.reshape(n, D_MODEL), y, route2, p["final_g"], cfg["tm_in"])
    return out.reshape(B, T, D_MODEL), kf[None], vf[None], s_new[None]


PROMPT_CFG = dict(tm_in=256, tq=512, tk=256, tb=256, nb=1, tm_proj=256, tm_post=512, split=2, bm=512)
SAMPLE_CFG = dict(tm_in=256, tq=64, tk=64, tb=64, nb=4, tm_proj=64, tm_post=64, split=1, bm=128)


def kernel(x_prompt, x_sample, mem_prompt, cache_diff_k, cache_diff_v, state_hgrn, cache_mem_k, cache_mem_v, ln_mix_g, w_in, da_lambda, da_subln_g, hg_lb_logits, hg_gnorm_g, w_out, ln_mem_g, mem_norm_g, w_mem_q, w_mem_k, w_mem_v, w_mem_o, ln_ffn_g, router_group_w, router_group_b, router_expert_w, router_expert_b, exp_w_gate, exp_w_up, exp_w_down, final_norm_g):
    assert w_in.shape[0] == 1, "single-layer configuration"
    lb_all = jnp.cumsum(jax.nn.softmax(hg_lb_logits.astype(F32), axis=0), axis=0)
    lp = da_lambda[0].astype(F32)
    lam = jnp.exp(jnp.sum(lp[0] * lp[1])) - jnp.exp(jnp.sum(lp[2] * lp[3])) + LAM_INIT
    w_r = jnp.zeros((D_MODEL, HEAD_W), F32)
    w_r = w_r.at[:, :N_GROUPS].set(router_group_w[0]).at[:, N_GROUPS:N_GROUPS + N_EXPERTS].set(router_expert_w[0])
    b_r = jnp.zeros((1, HEAD_W), F32)
    b_r = b_r.at[0, :N_GROUPS].set(router_group_b[0]).at[0, N_GROUPS:N_GROUPS + N_EXPERTS].set(router_expert_b[0])
    p = {
        "ln_mix": ln_mix_g[0], "w_in": w_in[0].astype(BF16), "lb": lb_all[0],
        "slopes": jnp.exp2(-8.0 * jnp.arange(1, N_HEADS + 1, dtype=F32) / N_HEADS),
        "lam": lam.reshape(1), "da_subln": da_subln_g[0], "hg_gnorm": hg_gnorm_g[0],
        "w_out": w_out[0].astype(BF16), "ln_mem": ln_mem_g[0], "w_mem_q": w_mem_q[0].astype(BF16),
        "w_mem_o": w_mem_o[0].astype(BF16), "ln_ffn": ln_ffn_g[0], "w_r": w_r.astype(BF16), "b_r": b_r,
        "e_gate": exp_w_gate[0].astype(BF16), "e_up": exp_w_up[0].astype(BF16),
        "e_down": exp_w_down[0].astype(BF16), "final_g": final_norm_g,
    }
    Bp, Tp, _ = x_prompt.shape
    Bs, Ts, _ = x_sample.shape
    past_len = cache_diff_k.shape[2]

    mkf, mvf, mkb, mvb = _memory_kv(mem_prompt, mem_norm_g[0], w_mem_k[0].astype(BF16),
                                    w_mem_v[0].astype(BF16))
    zero_state = jnp.zeros((Bp, N_HEADS, HEAD_W, HEAD_W), F32)
    yp, kp, vp, sp = _layer(x_prompt, None, None, zero_state, mkb, mvb, p, PROMPT_CFG)
    ys, ks, vs, ss = _layer(
        x_sample, cache_diff_k[0].reshape(Bs, past_len, GROUP_W), cache_diff_v[0].reshape(Bs, past_len, GROUP_W),
        state_hgrn[0], cache_mem_k[0], cache_mem_v[0], p, SAMPLE_CFG)
    return (yp, ys, kp, vp, sp, mkf[None], mvf[None], ks, vs, ss)
```

```python
import functools
import math

import jax
import jax.numpy as jnp
from jax import lax
from jax.experimental import pallas as pl
from jax.experimental.pallas import tpu as pltpu

F32 = jnp.float32
BF16 = jnp.bfloat16

D_MODEL = 1024
EPS = 1e-5
CHUNK = 64
N_HEADS = 4
HEAD_W = 128
QK_DIM = 64
GROUP_W = N_HEADS * HEAD_W
N_PROJ = 7
DA_SCALE = QK_DIM ** -0.5
LOG2E = 1.4426950408889634
LAM_INIT = 0.8 - 0.6 * math.exp(-0.3 * 0)
N_MEM = 256
MEM_HEADS = 4
MEM_HEAD_DIM = D_MODEL // MEM_HEADS
MEM_SCALE = MEM_HEAD_DIM ** -0.5
N_GROUPS = 4
EXPERTS_PER_GROUP = 8
N_EXPERTS = N_GROUPS * EXPERTS_PER_GROUP
TOP_K = 2
EXPERT_FF = 512
ROUTE_W = 8
SUB = 16
SUB_KEYS = 128
N_SCORE_SLOTS = 3
N_SLAB = D_MODEL // HEAD_W
NEG_INF = float("-inf")

VMEM_LIMIT = 48 * 1024 * 1024


def _cparams(sem):
    return pltpu.CompilerParams(dimension_semantics=sem, vmem_limit_bytes=VMEM_LIMIT)


def _rms(x, g):
    return x * lax.rsqrt(jnp.mean(x * x, axis=-1, keepdims=True) + EPS) * g


def _dot(a, b):
    return jnp.dot(a, b, preferred_element_type=F32)


def _dot_nt(a, b):
    return lax.dot_general(a, b, (((1,), (1,)), ((), ())), preferred_element_type=F32)


def _dot_tn(a, b):
    return lax.dot_general(a, b, (((0,), (0,)), ((), ())), preferred_element_type=F32)


def _run_skewed(gens):
    waiting, live = list(gens), []
    while waiting or live:
        if waiting:
            live.append(waiting.pop(0))
        for gen in list(live):
            if next(gen, "done") == "done":
                live.remove(gen)


def _memkv_kernel(m_ref, g_ref, wk_ref, wv_ref, kf_ref, vf_ref, kb_ref, vb_ref):
    mn = _rms(m_ref[0], g_ref[...]).astype(BF16)
    k = _dot(mn, wk_ref[...])
    v = _dot(mn, wv_ref[...])
    kf_ref[0] = k.reshape(N_MEM, MEM_HEADS, MEM_HEAD_DIM)
    vf_ref[0] = v.reshape(N_MEM, MEM_HEADS, MEM_HEAD_DIM)
    kb_ref[0] = k.astype(BF16)
    vb_ref[0] = v.astype(BF16)


def _memory_kv(mem, g, wk, wv):
    B = mem.shape[0]
    blk = pl.BlockSpec((1, N_MEM, D_MODEL), lambda b: (b, 0, 0))
    hblk = pl.BlockSpec((1, N_MEM, MEM_HEADS, MEM_HEAD_DIM), lambda b: (b, 0, 0, 0))
    wspec = pl.BlockSpec((D_MODEL, D_MODEL), lambda b: (0, 0))
    return pl.pallas_call(
        _memkv_kernel,
        grid=(B,),
        in_specs=[blk, pl.BlockSpec((1, D_MODEL), lambda b: (0, 0)), wspec, wspec],
        out_specs=[hblk, hblk, blk, blk],
        out_shape=[jax.ShapeDtypeStruct((B, N_MEM, MEM_HEADS, MEM_HEAD_DIM), F32)] * 2
                  + [jax.ShapeDtypeStruct(mem.shape, BF16)] * 2,
        compiler_params=_cparams(("parallel",)),
        name="memory_kv",
    )(mem, g.reshape(1, D_MODEL), wk, wv)


def _in_proj_kernel(x_ref, g_ref, w_ref, lb_ref, q_ref, kf_ref, vf_ref, kb_ref, vt_ref,
                    qh_ref, lf_ref, kk_ref, hi_ref, sg_ref, *, nb, tm):
    rows = nb * tm
    h = _rms(x_ref[...].reshape(rows, D_MODEL), g_ref[...]).astype(BF16)

    def proj(i):
        return _dot(h, w_ref[:, i * GROUP_W:(i + 1) * GROUP_W])

    def put(ref, val):
        ref[...] = val.reshape(nb, tm, GROUP_W).astype(ref.dtype)

    def put_heads(ref, val):
        ref[...] = val.reshape(nb, tm, N_HEADS, HEAD_W)

    put(q_ref, proj(0) * (DA_SCALE * LOG2E))
    dk = proj(1)
    put_heads(kf_ref, dk)
    put(kb_ref, dk)
    dv = proj(2)
    put_heads(vf_ref, dv)
    for b in range(nb):
        vt_ref[b] = dv[b * tm:(b + 1) * tm].T.astype(BF16)
    hq = proj(3)
    put(qh_ref, hq * jax.nn.sigmoid(hq))
    hf = proj(4)
    lb = lb_ref[...]
    put(lf_ref, jnp.log(lb + (1.0 - lb) * jax.nn.sigmoid(hf)))
    put(kk_ref, (1.0 - lb) * jax.nn.sigmoid(-hf))
    put(hi_ref, proj(5))
    hg = proj(6)
    put(sg_ref, hg * jax.nn.sigmoid(hg))


def _in_proj(x, g, w_bf, lb, nb, tm):
    B, T, _ = x.shape
    tok = lambda w: pl.BlockSpec((nb, tm, w), lambda b, t: (b, t, 0))
    fix = lambda s: pl.BlockSpec(s, lambda b, t: (0, 0))
    f32o = jax.ShapeDtypeStruct((B, T, GROUP_W), F32)
    bfo = jax.ShapeDtypeStruct((B, T, GROUP_W), BF16)
    o_spec = tok(GROUP_W)
    ho = jax.ShapeDtypeStruct((B, T, N_HEADS, HEAD_W), F32)
    h_spec = pl.BlockSpec((nb, tm, N_HEADS, HEAD_W), lambda b, t: (b, t, 0, 0))
    vt_spec = pl.BlockSpec((nb, GROUP_W, tm), lambda b, t: (b, 0, t))
    return pl.pallas_call(
        functools.partial(_in_proj_kernel, nb=nb, tm=tm),
        grid=(B // nb, T // tm),
        in_specs=[tok(D_MODEL), fix((1, D_MODEL)), fix((D_MODEL, N_PROJ * GROUP_W)), fix((1, GROUP_W))],
        out_specs=[o_spec, h_spec, h_spec, o_spec, vt_spec] + [o_spec] * 5,
        out_shape=[bfo, ho, ho, bfo, jax.ShapeDtypeStruct((B, GROUP_W, T), BF16),
                   bfo, f32o, bfo, bfo, bfo],
        compiler_params=_cparams(("parallel", "parallel")),
        name="in_proj",
    )(x, g.reshape(1, D_MODEL), w_bf, lb.reshape(1, GROUP_W))


def _attn_kernel(slope_ref, lam_ref, q_ref, k_ref, vt_ref, g_ref, o_ref,
                 m_ref, l_ref, acc_ref, base_ref, t_ref, mb_ref, *, tq, tk):
    h = pl.program_id(1)
    qi = pl.program_id(2)
    slope2 = slope_ref[h] * LOG2E
    qt = q_ref[0].astype(F32).T.astype(BF16)
    zeros = jnp.zeros((QK_DIM, tq), BF16)
    qt_maps = (jnp.concatenate([qt[:QK_DIM], zeros], axis=0),
               jnp.concatenate([zeros, qt[QK_DIM:]], axis=0))
    m_ref[...] = jnp.full(m_ref.shape, NEG_INF, F32)
    l_ref[...] = jnp.zeros(l_ref.shape, F32)
    acc_ref[...] = jnp.zeros(acc_ref.shape, F32)
    q0 = qi * tq

    def lane_pad(x, lo, fill):
        if lo == 0:
            return x
        return jnp.concatenate([jnp.full((x.shape[0], lo), fill, x.dtype), x], axis=1)


    def score_phase(slot, n_keys, get_k, get_bias, shift, q_lo=lambda u: 0):
        sub = min(n_keys, SUB_KEYS)
        mx = [None, None]
        for u in range(n_keys // sub):
            lo = q_lo(u)
            bias, visible = get_bias(u)
            k = get_k(u)
            for c in range(2):
                t = _dot(k, qt_maps[c][:, lo:]) + bias
                if visible is not None:
                    w = visible.shape[1]
                    head = jnp.where(visible, t[:, :w], NEG_INF)
                    t = head if w == t.shape[1] else jnp.concatenate([head, t[:, w:]], axis=1)
                t_ref[slot, c, u * sub:(u + 1) * sub, lo:] = t
                mu = lane_pad(jnp.max(t, axis=0, keepdims=True), lo, NEG_INF)
                mx[c] = mu if mx[c] is None else jnp.maximum(mx[c], mu)
            yield
        for c in range(2):
            mb_ref[slot, c] = mx[c] + shift
        yield

    def prob_phase(slot, n_keys, get_vt, shift, q_lo=lambda u: 0, pv_group=1):
        sub = min(n_keys, SUB_KEYS)
        n_sub = n_keys // sub
        off, alpha = [], []
        for c in range(2):
            m_old = m_ref[c]
            m_new = jnp.maximum(m_old, mb_ref[slot, c])
            off.append(m_new - shift)
            alpha.append(jnp.exp2(m_old - m_new))
            m_ref[c] = m_new
        yield
        lsum = [None, None]
        pv = [None, None]
        grp = pv_group if n_sub % pv_group == 0 else 1
        for u in range(0, n_sub, grp):
            lo = q_lo(u)
            assert all(q_lo(u + i) == lo for i in range(grp))
            parts = [get_vt(u + i) for i in range(grp)]
            vt = parts[0] if grp == 1 else jnp.concatenate(parts, axis=1)
            for c in range(2):
                p = jnp.exp2(t_ref[slot, c, u * sub:(u + grp) * sub, lo:] - off[c][:, lo:])
                ls = lane_pad(jnp.sum(p, axis=0, keepdims=True), lo, 0.0)
                pu = lane_pad(_dot(vt, p.astype(BF16)), lo, 0.0)
                lsum[c] = ls if lsum[c] is None else lsum[c] + ls
                pv[c] = pu if pv[c] is None else pv[c] + pu
            yield
        for c in range(2):
            l_ref[c] = alpha[c] * l_ref[c] + lsum[c]
            acc_ref[c] = alpha[c] * acc_ref[c] + pv[c]
        yield

    def run(*gens):
        live = list(gens)
        while live:
            for gen in list(live):
                if next(gen, "done") == "done":
                    live.remove(gen)

    def update(n_keys, get_k, get_vt, get_bias, shift, q_lo=lambda u: 0, pv_group=1):
        run(score_phase(0, n_keys, get_k, get_bias, shift, q_lo))
        run(prob_phase(0, n_keys, get_vt, shift, q_lo, pv_group))

    def rel_bias(n_keys, key0):
        key = lax.broadcasted_iota(jnp.int32, (n_keys, tq), 0) + key0
        qry = lax.broadcasted_iota(jnp.int32, (n_keys, tq), 1)
        return (qry - key).astype(F32) * (-slope2)

    def block_shift(k0):
        return (q0 - k0).astype(F32) * (-slope2)

    n_q = k_ref.shape[1] // tq
    if n_q > 1 or tq > SUB_KEYS:
        @pl.when(qi == 0)
        def _():
            base_ref[...] = rel_bias(base_ref.shape[0], 0)

    kd = 0 if n_q == 1 else pl.multiple_of(qi * tq, tq)
    sub_d = min(tq, SUB_KEYS)
    key = lax.broadcasted_iota(jnp.int32, (sub_d, sub_d), 0)
    qry = lax.broadcasted_iota(jnp.int32, (sub_d, sub_d), 1)
    strip_bias = jnp.abs(qry - key).astype(F32) * (-slope2)
    strip_visible = (key // CHUNK) <= (qry // CHUNK)

    def diag_bias(u):
        lo = (u + 1) * sub_d
        if lo == tq:
            return strip_bias, strip_visible
        later = base_ref[u * sub_d:(u + 1) * sub_d, lo:]
        return jnp.concatenate([strip_bias, later], axis=1), strip_visible

    diag_k = lambda u: k_ref[0, pl.ds(kd + u * sub_d, sub_d), :]
    diag_vt = lambda u: vt_ref[0, :, pl.ds(kd + u * sub_d, sub_d)]
    diag_lo = lambda u: u * sub_d

    if n_q == 1:
        update(tq, diag_k, diag_vt, diag_bias, 0.0, q_lo=diag_lo)
    else:
        sub_k = min(tk, SUB_KEYS)
        n_prev = qi * (tq // tk)
        assert (tq // tk) % 2 == 0, "earlier blocks are consumed in pairs"

        def prev_scores(slot, j):
            k0 = pl.multiple_of(j * tk, tk)
            return score_phase(slot, tk, lambda u: k_ref[0, pl.ds(k0 + u * sub_k, sub_k), :],
                               lambda u: (base_ref[u * sub_k:(u + 1) * sub_k, :], None),
                               block_shift(k0))

        def prev_probs(slot, j):
            k0 = pl.multiple_of(j * tk, tk)
            return prob_phase(slot, tk, lambda u: vt_ref[0, :, pl.ds(k0 + u * sub_k, sub_k)],
                              block_shift(k0), pv_group=2)

        run(score_phase(0, tq, diag_k, diag_bias, 0.0, diag_lo))
        run(prob_phase(0, tq, diag_vt, 0.0, diag_lo), prev_scores(1, 0))

        def pair_body(i, carry):
            j = 2 * i
            run(prev_probs(1, j), prev_scores(2, j + 1))
            run(prev_probs(2, j + 1), prev_scores(1, jnp.minimum(j + 2, n_prev - 1)))
            return carry
        lax.fori_loop(0, n_prev // 2, pair_body, 0)

    lam = lam_ref[0]
    ot = acc_ref[0] / l_ref[0] - lam * (acc_ref[1] / l_ref[1])
    ot = ot * lax.rsqrt(jnp.mean(ot * ot, axis=0, keepdims=True) + EPS)
    o_ref[0] = (ot.T * g_ref[...] * (1.0 - LAM_INIT)).astype(o_ref.dtype)


def _diff_attn(q, k, vt, slopes, lam, subln_g, tq, tk):
    B, T, _ = q.shape
    smem = pl.BlockSpec(memory_space=pltpu.SMEM)
    qspec = pl.BlockSpec((1, tq, HEAD_W), lambda b, h, i: (b, i, h))
    return pl.pallas_call(
        functools.partial(_attn_kernel, tq=tq, tk=tk),
        grid=(B, N_HEADS, T // tq),
        in_specs=[smem, smem, qspec,
                  pl.BlockSpec((1, T, HEAD_W), lambda b, h, i: (b, 0, h)),
                  pl.BlockSpec((1, HEAD_W, T), lambda b, h, i: (b, h, 0)),
                  pl.BlockSpec((1, HEAD_W), lambda b, h, i: (0, 0))],
        out_specs=qspec,
        out_shape=jax.ShapeDtypeStruct((B, T, GROUP_W), BF16),
        scratch_shapes=[pltpu.VMEM((2, 1, tq), F32), pltpu.VMEM((2, 1, tq), F32),
                        pltpu.VMEM((2, HEAD_W, tq), F32), pltpu.VMEM((max(tk, tq), tq), F32),
                        pltpu.VMEM((N_SCORE_SLOTS, 2, max(tk, tq), tq), F32),
                        pltpu.VMEM((N_SCORE_SLOTS, 2, 1, tq), F32)],
        compiler_params=_cparams(("parallel", "parallel", "arbitrary")),
        name="diff_attn",
    )(slopes, lam, q, k, vt, subln_g.reshape(1, HEAD_W))


def _attn_step_kernel(slope_ref, lam_ref, q_ref, k_ref, vt_ref, pk_ref, pv_ref, g_ref, o_ref, *,
                      t_new, past_len):
    assert 2 * t_new <= HEAD_W and t_new % CHUNK == 0 and past_len % CHUNK == 0
    pk = pk_ref[0].reshape(past_len, GROUP_W)
    pv = pv_ref[0].reshape(past_len, GROUP_W)
    lam = lam_ref[0]
    qry2 = lax.broadcasted_iota(jnp.int32, (1, 2 * t_new), 1) % t_new
    key_p = lax.broadcasted_iota(jnp.int32, (past_len, 2 * t_new), 0)
    key_n = lax.broadcasted_iota(jnp.int32, (t_new, 2 * t_new), 0)
    dist_p = (qry2 + past_len - key_p).astype(F32)
    dist_n = jnp.abs(qry2 - key_n).astype(F32)
    visible = (key_n // CHUNK) <= (qry2 // CHUNK)
    zeros = jnp.zeros((QK_DIM, t_new), BF16)
    for h in range(N_HEADS):
        hs = slice(h * HEAD_W, (h + 1) * HEAD_W)
        slope2 = slope_ref[h] * LOG2E
        qt = q_ref[0, :, hs].astype(F32).T.astype(BF16)
        qt2 = jnp.concatenate([jnp.concatenate([qt[:QK_DIM], zeros], axis=0),
                               jnp.concatenate([zeros, qt[QK_DIM:]], axis=0)], axis=1)
        s_p = _dot(pk[:, hs].astype(BF16), qt2) - slope2 * dist_p
        s_n = jnp.where(visible, _dot(k_ref[0, :, hs], qt2) - slope2 * dist_n, NEG_INF)
        m = jnp.maximum(jnp.max(s_p, axis=0, keepdims=True), jnp.max(s_n, axis=0, keepdims=True))
        p_p = jnp.exp2(s_p - m)
        p_n = jnp.exp2(s_n - m)
        l = jnp.sum(p_p, axis=0, keepdims=True) + jnp.sum(p_n, axis=0, keepdims=True)
        acc = (_dot_tn(pv[:, hs].astype(BF16), p_p.astype(BF16))
               + _dot(vt_ref[0, hs, :], p_n.astype(BF16))) / l
        ot = acc[:, :t_new] - lam * acc[:, t_new:]
        ot = ot * lax.rsqrt(jnp.mean(ot * ot, axis=0, keepdims=True) + EPS)
        o_ref[0, :, hs] = (ot.T * g_ref[...] * (1.0 - LAM_INIT)).astype(o_ref.dtype)


def _diff_attn_step(q, k, vt, past_k, past_v, slopes, lam, subln_g):
    B, T, _ = q.shape
    past_len = past_k.shape[1]
    smem = pl.BlockSpec(memory_space=pltpu.SMEM)
    tok = pl.BlockSpec((1, T, GROUP_W), lambda b: (b, 0, 0))
    pspec = pl.BlockSpec((1, past_len, N_HEADS, HEAD_W), lambda b: (b, 0, 0, 0))
    return pl.pallas_call(
        functools.partial(_attn_step_kernel, t_new=T, past_len=past_len),
        grid=(B,),
        in_specs=[smem, smem, tok, tok, pl.BlockSpec((1, GROUP_W, T), lambda b: (b, 0, 0)),
                  pspec, pspec, pl.BlockSpec((1, HEAD_W), lambda b: (0, 0))],
        out_specs=tok,
        out_shape=jax.ShapeDtypeStruct((B, T, GROUP_W), BF16),
        compiler_params=_cparams(("parallel",)),
        name="diff_attn_step",
    )(slopes, lam, q, k, vt, past_k, past_v, subln_g.reshape(1, HEAD_W))


def _split3(x):
    hi = x.astype(BF16)
    r = x - hi.astype(F32)
    mid = r.astype(BF16)
    lo = (r - mid.astype(F32)).astype(BF16)
    return hi, mid, lo


def _hgrn_kernel(q_ref, k_ref, lf_ref, v_ref, sg_ref, gn_ref, s0_ref, o_ref, sfin_ref,
                 st_ref, b_ref, *, tb):
    t = pl.program_id(1)
    n_chunks = tb // CHUNK

    @pl.when(t == 0)
    def _():
        for h in range(N_HEADS):
            st_ref[h] = s0_ref[0, h].T

    row = lax.broadcasted_iota(jnp.int32, (tb, tb), 0)
    col = lax.broadcasted_iota(jnp.int32, (tb, tb), 1)
    tri = jnp.where((col <= row) & (col // CHUNK == row // CHUNK), 1.0, 0.0).astype(BF16)
    hi, mid, lo = _split3(lf_ref[0] * LOG2E)
    b_ref[...] = _dot(tri, hi) + _dot(tri, mid) + _dot(tri, lo)

    sub_row = lax.broadcasted_iota(jnp.int32, (8, HEAD_W), 0)
    zeros16 = jnp.zeros((SUB, HEAD_W), F32)

    def pad_rows(x, r0):
        parts = []
        if r0:
            parts.append(jnp.zeros((r0, HEAD_W), F32))
        parts.append(x)
        rest = CHUNK - r0 - x.shape[0]
        if rest:
            parts.append(jnp.zeros((rest, HEAD_W), F32))
        return jnp.concatenate(parts, axis=0) if len(parts) > 1 else x

    def chunk_body(c, carry):
        r0 = pl.multiple_of(c * CHUNK, CHUNK)
        def head(h):
            hs = slice(h * HEAD_W, (h + 1) * HEAD_W)
            q = q_ref[0, pl.ds(r0, CHUNK), hs].astype(F32)
            k = k_ref[0, pl.ds(r0, CHUNK), hs].astype(F32)
            v_bf = v_ref[0, pl.ds(r0, CHUNK), hs]
            v = v_bf.astype(F32)
            b = b_ref[pl.ds(r0, CHUNK), hs]
            b_last = b[CHUNK - 1:CHUNK, :]
            st = st_ref[h]

            o = _dot_nt((q * jnp.exp2(b)).astype(BF16), st.astype(BF16))

            q_segs, k_segs = [], []
            for i in range(1, CHUNK // SUB):
                lo_r, hi_r = i * SUB, (i + 1) * SUB
                ref_b = b[lo_r - 1:lo_r, :]
                q_segs.append(pad_rows(q[lo_r:hi_r] * jnp.exp2(b[lo_r:hi_r] - ref_b), lo_r))
                k_segs.append(pad_rows(k[:lo_r] * jnp.exp2(ref_b - b[:lo_r]), 0))
            a_off = _dot_nt(jnp.concatenate(q_segs, axis=1).astype(BF16),
                            jnp.concatenate(k_segs, axis=1).astype(BF16))
            o = o + _dot(a_off.astype(BF16), v_bf)
            yield

            diag = []
            for blk in range(CHUNK // SUB):
                base = blk * SUB
                qa, qb = q[base:base + 8], q[base + 8:base + SUB]
                ba, bb = b[base:base + 8], b[base + 8:base + SUB]
                oa = jnp.zeros((8, HEAD_W), F32)
                ob = jnp.zeros((8, HEAD_W), F32)
                for s in range(SUB):
                    ks = k[base + s:base + s + 1]
                    bs = b[base + s:base + s + 1]
                    vs = v[base + s:base + s + 1]
                    if s < 8:
                        e = jnp.exp2(ba - bs)
                        if s:
                            e = jnp.where(sub_row >= s, e, 0.0)
                        oa = oa + jnp.sum(qa * ks * e, axis=-1, keepdims=True) * vs
                        ob = ob + jnp.sum(qb * ks * jnp.exp2(bb - bs), axis=-1, keepdims=True) * vs
                    else:
                        e = jnp.exp2(bb - bs)
                        if s > 8:
                            e = jnp.where(sub_row >= s - 8, e, 0.0)
                        ob = ob + jnp.sum(qb * ks * e, axis=-1, keepdims=True) * vs
                diag += [oa, ob]
                yield
            o = o + jnp.concatenate(diag, axis=0)

            k_dec = (k * jnp.exp2(b_last - b)).astype(BF16)
            st_ref[h] = st * jnp.exp2(b_last) + _dot_tn(v_bf, k_dec)

            out = _rms(o, gn_ref[...]) * sg_ref[0, pl.ds(r0, CHUNK), hs]
            o_ref[0, pl.ds(r0, CHUNK), hs] = out.astype(o_ref.dtype)
            yield

        _run_skewed([head(h) for h in range(N_HEADS)])
        return carry

    lax.fori_loop(0, n_chunks, chunk_body, 0)

    @pl.when(t == pl.num_programs(1) - 1)
    def _():
        for h in range(N_HEADS):
            sfin_ref[0, h] = st_ref[h].T


def _hgrn(qh, kk, lf, hi, sg, gnorm_g, s0, tb):
    B, T, _ = qh.shape
    tspec = pl.BlockSpec((1, tb, GROUP_W), lambda b, t: (b, t, 0))
    sspec = pl.BlockSpec((1, N_HEADS, HEAD_W, HEAD_W), lambda b, t: (b, 0, 0, 0))
    return pl.pallas_call(
        functools.partial(_hgrn_kernel, tb=tb),
        grid=(B, T // tb),
        in_specs=[tspec] * 5 + [pl.BlockSpec((1, HEAD_W), lambda b, t: (0, 0)), sspec],
        out_specs=[tspec, sspec],
        out_shape=[jax.ShapeDtypeStruct((B, T, GROUP_W), BF16),
                   jax.ShapeDtypeStruct((B, N_HEADS, HEAD_W, HEAD_W), F32)],
        scratch_shapes=[pltpu.VMEM((N_HEADS, HEAD_W, HEAD_W), F32), pltpu.VMEM((tb, GROUP_W), F32)],
        compiler_params=_cparams(("parallel", "arbitrary")),
        name="hgrn",
    )(qh, kk, lf, hi, sg, gnorm_g.reshape(1, HEAD_W), s0)


def _post_kernel(x_ref, oda_ref, ohg_ref, mk_ref, mv_ref, wout_ref, lnm_ref, wq_ref, wo_ref,
                 lnf_ref, wr_ref, br_ref, x2_ref, h3_ref, route_ref, om_ref, *, nb, tm, split):
    gt = tm // split
    groups = [_post_group(slice(g * gt, (g + 1) * gt), g * nb * gt, x_ref, oda_ref, ohg_ref,
                          mk_ref, mv_ref, wout_ref, lnm_ref, wq_ref, wo_ref, lnf_ref, wr_ref,
                          br_ref, x2_ref, h3_ref, route_ref, om_ref, nb=nb, gt=gt)
              for g in range(split)]
    _run_skewed(groups)


def _post_group(ts, om0, x_ref, oda_ref, ohg_ref, mk_ref, mv_ref, wout_ref, lnm_ref, wq_ref, wo_ref,
                lnf_ref, wr_ref, br_ref, x2_ref, h3_ref, route_ref, om_ref, *, nb, gt):
    rows = nb * gt
    x = x_ref[:, ts, :].reshape(rows, D_MODEL)
    mixed = (_dot(oda_ref[:, ts, :].reshape(rows, GROUP_W), wout_ref[:GROUP_W, :])
             + _dot(ohg_ref[:, ts, :].reshape(rows, GROUP_W), wout_ref[GROUP_W:, :]))
    x1 = x + mixed
    yield

    hm = _rms(x1, lnm_ref[...]).astype(BF16)
    yield
    qm = (_dot(hm, wq_ref[...]) * MEM_SCALE).astype(BF16)
    yield
    for b in range(nb):
        if len(mk_ref.shape) == 4:
            mk_b = mk_ref[b].reshape(N_MEM, D_MODEL).astype(BF16)
            mv_b = mv_ref[b].reshape(N_MEM, D_MODEL).astype(BF16)
        for h in range(MEM_HEADS):
            hs = slice(h * MEM_HEAD_DIM, (h + 1) * MEM_HEAD_DIM)
            if len(mk_ref.shape) == 4:
                mk, mv = mk_b[:, hs], mv_b[:, hs]
            else:
                mk, mv = mk_ref[b, :, hs], mv_ref[b, :, hs]
            s = _dot_nt(qm[b * gt:(b + 1) * gt, hs], mk.astype(BF16))
            e = jnp.exp(s - jnp.max(s, axis=-1, keepdims=True))
            p = e / jnp.sum(e, axis=-1, keepdims=True)
            om_ref[om0 + b * gt:om0 + (b + 1) * gt, hs] = _dot(
                p.astype(BF16), mv.astype(BF16)).astype(BF16)
    yield
    x2 = x1 + _dot(om_ref[om0:om0 + rows, :], wo_ref[...])
    x2_ref[:, ts, :] = x2.reshape(nb, gt, D_MODEL)
    yield

    h3 = _rms(x2, lnf_ref[...])
    h3_ref[:, ts] = h3.reshape(nb, gt, N_SLAB, HEAD_W)
    yield

    r = _dot(h3.astype(BF16), wr_ref[...]) + br_ref[...]
    lane = lax.broadcasted_iota(jnp.int32, r.shape, 1).astype(F32)
    big = float(4 * HEAD_W)
    g_mask = lane < N_GROUPS
    gl = jnp.where(g_mask, r, NEG_INF)
    g_max = jnp.max(gl, axis=-1, keepdims=True)
    g_idx = jnp.min(jnp.where(gl == g_max, lane, big), axis=-1, keepdims=True)
    g_w = 1.0 / jnp.sum(jnp.where(g_mask, jnp.exp(r - g_max), 0.0), axis=-1, keepdims=True)
    e_lo = N_GROUPS + EXPERTS_PER_GROUP * g_idx
    el = jnp.where((lane >= e_lo) & (lane < e_lo + EXPERTS_PER_GROUP), r, NEG_INF)
    v1 = jnp.max(el, axis=-1, keepdims=True)
    i1 = jnp.min(jnp.where(el == v1, lane, big), axis=-1, keepdims=True)
    el2 = jnp.where(lane == i1, NEG_INF, el)
    v2 = jnp.max(el2, axis=-1, keepdims=True)
    i2 = jnp.min(jnp.where(el2 == v2, lane, big), axis=-1, keepdims=True)
    t = jnp.exp(v2 - v1)
    p1 = 1.0 / (1.0 + t)
    rec = jnp.where(lane == 0.0, i1 - N_GROUPS,
          jnp.where(lane == 1.0, i2 - N_GROUPS,
          jnp.where(lane == 2.0, p1 * g_w,
          jnp.where(lane == 3.0, t * p1 * g_w, 0.0))))
    route_ref[:, ts, :] = rec[:, :ROUTE_W].reshape(nb, gt, ROUTE_W)


def _post_mix(x, oda, ohg, mem_k, mem_v, w_out, ln_mem, w_q, w_o, ln_ffn, w_r, b_r, nb, tm, split):
    B, T, _ = x.shape
    tok = lambda w: pl.BlockSpec((nb, tm, w), lambda b, t: (b, t, 0))
    if mem_k.ndim == 4:
        memspec = pl.BlockSpec((nb, N_MEM, MEM_HEADS, MEM_HEAD_DIM), lambda b, t: (b, 0, 0, 0))
    else:
        memspec = pl.BlockSpec((nb, N_MEM, D_MODEL), lambda b, t: (b, 0, 0))
    fix = lambda s: pl.BlockSpec(s, lambda b, t: (0, 0))
    return pl.pallas_call(
        functools.partial(_post_kernel, nb=nb, tm=tm, split=split),
        grid=(B // nb, T // tm),
        in_specs=[tok(D_MODEL), tok(GROUP_W), tok(GROUP_W), memspec, memspec,
                  fix((D_MODEL, D_MODEL)), fix((1, D_MODEL)), fix((D_MODEL, D_MODEL)),
                  fix((D_MODEL, D_MODEL)), fix((1, D_MODEL)), fix((D_MODEL, HEAD_W)),
                  fix((1, HEAD_W))],
        out_specs=[tok(D_MODEL),
                   pl.BlockSpec((nb, tm, N_SLAB, HEAD_W), lambda b, t: (b, t, 0, 0)),
                   tok(ROUTE_W)],
        out_shape=[jax.ShapeDtypeStruct((B, T, D_MODEL), F32),
                   jax.ShapeDtypeStruct((B, T, N_SLAB, HEAD_W), F32),
                   jax.ShapeDtypeStruct((B, T, ROUTE_W), F32)],
        scratch_shapes=[pltpu.VMEM((nb * tm, D_MODEL), BF16)],
        compiler_params=_cparams(("parallel", "parallel")),
        name="post_mix",
    )(x, oda, ohg, mem_k, mem_v, w_out, ln_mem.reshape(1, D_MODEL), w_q, w_o,
      ln_ffn.reshape(1, D_MODEL), w_r, b_r)


def _experts_kernel(blk_e_ref, n_used_ref, src_next_ref, dst_prev_ref, src0_ref, dst_ref, h_hbm,
                    wg_ref, wu_ref, wd_ref, y_hbm, xbuf, ybuf, gsem, ssem, *, bm, m_tot):
    del blk_e_ref
    i = pl.program_id(0)
    n_used = n_used_ref[0]
    slot = i % 2

    def gather_row(tab_ref, r, s, priority=0):
        pltpu.make_async_copy(h_hbm.at[tab_ref[0, 0, r]], xbuf.at[s, r],
                              gsem.at[s]).start(priority=priority)

    def scatter_row(tab_ref, r, s, priority=0):
        pltpu.make_async_copy(ybuf.at[s, r], y_hbm.at[tab_ref[0, 0, r]],
                              ssem.at[s]).start(priority=priority)

    def wait_block(buf, sem, s):
        pltpu.make_async_copy(buf.at[s], buf.at[s], sem.at[s]).wait()

    def spare_fill(s):
        return pltpu.make_async_copy(ybuf.at[s], y_hbm.at[pl.ds(m_tot + s * bm, bm)], ssem.at[s])

    @pl.when(i == 0)
    def _():
        ybuf[...] = jnp.zeros(ybuf.shape, F32)
        spare_fill(0).start()
        spare_fill(1).start()
        spare_fill(1).wait()

        def body(r, c):
            gather_row(src0_ref, r, 0)
            return c
        lax.fori_loop(0, bm, body, 0)

    @pl.when(i < n_used)
    def _():
        wait_block(xbuf, gsem, slot)
        wait_block(ybuf, ssem, slot)
        x = xbuf[slot].reshape(bm, D_MODEL).astype(BF16)
        hg = _dot(x, wg_ref[0])
        hu = _dot(x, wu_ref[0])
        hb = (hg * jax.nn.sigmoid(hg) * hu).astype(BF16)
        ybuf[slot] = _dot(hb, wd_ref[0]).reshape(bm, N_SLAB, HEAD_W)
        for r in range(bm):
            gather_row(src_next_ref, r, 1 - slot, priority=r % 2)
            scatter_row(dst_prev_ref, r, 1 - slot, priority=(r + 1) % 2)

    @pl.when(i == n_used - 1)
    def _():
        def body(r, c):
            scatter_row(dst_ref, r, slot)
            return c
        lax.fori_loop(0, bm, body, 0)
        wait_block(xbuf, gsem, 1 - slot)
        wait_block(ybuf, ssem, 1 - slot)
        wait_block(ybuf, ssem, slot)


def _experts(h3, blk_e, n_used, src_tab, dst_tab, dst_prev_tab, wg, wu, wd, bm):
    n_blocks = src_tab.shape[0]
    m_tot = TOP_K * h3.shape[0]
    tab = lambda f: pl.BlockSpec((1, 1, bm), f, memory_space=pltpu.SMEM)
    cur = lambda i, e, n: (i, 0, 0)
    nxt = lambda i, e, n: (jnp.minimum(i + 1, n_blocks - 1), 0, 0)
    first = lambda i, e, n: (0, 0, 0)
    wspec = lambda a, b: pl.BlockSpec((1, a, b), lambda i, e, n: (e[i], 0, 0))
    grid_spec = pltpu.PrefetchScalarGridSpec(
        num_scalar_prefetch=2,
        grid=(n_blocks,),
        in_specs=[tab(nxt), tab(cur), tab(first), tab(cur), pl.BlockSpec(memory_space=pl.ANY),
                  wspec(D_MODEL, EXPERT_FF), wspec(D_MODEL, EXPERT_FF), wspec(EXPERT_FF, D_MODEL)],
        out_specs=pl.BlockSpec(memory_space=pl.ANY),
        scratch_shapes=[pltpu.VMEM((2, bm, N_SLAB, HEAD_W), F32), pltpu.VMEM((2, bm, N_SLAB, HEAD_W), F32),
                        pltpu.SemaphoreType.DMA((2,)), pltpu.SemaphoreType.DMA((2,))],
    )
    return pl.pallas_call(
        functools.partial(_experts_kernel, bm=bm, m_tot=m_tot),
        grid_spec=grid_spec,
        out_shape=jax.ShapeDtypeStruct((m_tot + 2 * bm, N_SLAB, HEAD_W), F32),
        compiler_params=_cparams(("arbitrary",)),
        name="experts",
    )(blk_e, n_used, src_tab, dst_prev_tab, src_tab, dst_tab, h3, wg, wu, wd)


def _combine_kernel(x_ref, y0_ref, y1_ref, route_ref, g_ref, o_ref):
    rt = route_ref[...]
    tm = x_ref.shape[0]
    y0 = y0_ref[...].reshape(tm, D_MODEL)
    y1 = y1_ref[...].reshape(tm, D_MODEL)
    y = x_ref[...] + rt[:, 2:3] * y0 + rt[:, 3:4] * y1
    o_ref[...] = _rms(y, g_ref[...])


def _combine(x2, y, route, final_g, tm):
    n = x2.shape[0]
    nt = n // tm
    return pl.pallas_call(
        _combine_kernel,
        grid=(nt,),
        in_specs=[pl.BlockSpec((tm, D_MODEL), lambda i: (i, 0)),
                  pl.BlockSpec((tm, N_SLAB, HEAD_W), lambda i: (i, 0, 0)),
                  pl.BlockSpec((tm, N_SLAB, HEAD_W), lambda i: (i + nt, 0, 0)),
                  pl.BlockSpec((tm, ROUTE_W), lambda i: (i, 0)),
                  pl.BlockSpec((1, D_MODEL), lambda i: (0, 0))],
        out_specs=pl.BlockSpec((tm, D_MODEL), lambda i: (i, 0)),
        out_shape=jax.ShapeDtypeStruct((n, D_MODEL), F32),
        compiler_params=_cparams(("parallel",)),
        name="combine",
    )(x2, y, y, route, final_g.reshape(1, D_MODEL))


def _routing_tables(route, bm):
    n = route.shape[0]
    m_tot = TOP_K * n
    flat_e = jnp.concatenate([route[:, 0], route[:, 1]]).astype(jnp.int32)
    experts = jnp.arange(N_EXPERTS, dtype=jnp.int32)
    counts = jnp.sum(flat_e[:, None] == experts[None, :], axis=0, dtype=jnp.int32)
    n_blk_e = (counts + bm - 1) // bm
    blk_end = jnp.cumsum(n_blk_e)
    n_blocks = m_tot // bm + N_EXPERTS
    blk = jnp.arange(n_blocks, dtype=jnp.int32)
    blk_e = jnp.minimum(jnp.sum(blk[:, None] >= blk_end[None, :], axis=1, dtype=jnp.int32),
                        N_EXPERTS - 1)
    r = jnp.arange(bm, dtype=jnp.int32)[None, :]
    n_pad = (n_blk_e * bm - counts)[:, None]
    pad_keys = jnp.where(r < n_pad, 2 * experts[:, None] + 1, 2 * N_EXPERTS + 1)
    keys = jnp.concatenate([2 * flat_e, pad_keys.reshape(-1)])
    vals = jnp.concatenate([jnp.arange(m_tot, dtype=jnp.int32),
                            jnp.full((N_EXPERTS * bm,), -1, jnp.int32)])
    m = lax.sort((keys, vals), num_keys=1)[1].reshape(n_blocks, bm)
    real = m >= 0
    spare = m_tot + (blk[:, None] % 2) * bm + r
    src = jnp.where(real, m % n, 0)
    dst = jnp.where(real, m, spare)
    dst_prev = jnp.concatenate([m_tot + bm + r, dst[:-1]], axis=0)
    shape = (n_blocks, 1, bm)
    return (blk_e, blk_end[N_EXPERTS - 1:].astype(jnp.int32), src.reshape(shape), dst.reshape(shape),
            dst_prev.reshape(shape))


def _layer(x, past_k, past_v, s0, mem_k, mem_v, p, cfg):
    B, T, _ = x.shape
    n = B * T
    (q, kf, vf, kb, vt, qh, lf, kk, hi, sg) = _in_proj(
        x, p["ln_mix"], p["w_in"], p["lb"], cfg["nb"], cfg["tm_proj"])
    if past_k is None:
        oda = _diff_attn(q, kb, vt, p["slopes"], p["lam"], p["da_subln"], cfg["tq"], cfg["tk"])
    else:
        oda = _diff_attn_step(q, kb, vt, past_k, past_v, p["slopes"], p["lam"], p["da_subln"])
    ohg, s_new = _hgrn(qh, kk, lf, hi, sg, p["hg_gnorm"], s0, cfg["tb"])
    x2, h3, route = _post_mix(x, oda, ohg, mem_k, mem_v, p["w_out"], p["ln_mem"], p["w_mem_q"],
                              p["w_mem_o"], p["ln_ffn"], p["w_r"], p["b_r"], cfg["nb"], cfg["tm_post"],
                              cfg["split"])
    route2 = route.reshape(n, ROUTE_W)
    blk_e, n_used, src_tab, dst_tab, dst_prev_tab = _routing_tables(route2, cfg["bm"])
    y = _experts(h3.reshape(n, N_SLAB, HEAD_W), blk_e, n_used, src_tab, dst_tab, dst_prev_tab,
                 p["e_gate"], p["e_up"], p["e_down"], cfg["bm"])
    out = _combine(x2.reshape(n, D_MODEL), y, route2, p["final_g"], cfg["tm_in"])
    return out.reshape(B, T, D_MODEL), kf[None], vf[None], s_new[None]


PROMPT_CFG = dict(tm_in=256, tq=512, tk=256, tb=256, nb=1, tm_proj=256, tm_post=512, split=2, bm=512)
SAMPLE_CFG = dict(tm_in=256, tq=64, tk=64, tb=64, nb=4, tm_proj=64, tm_post=64, split=1, bm=128)


def kernel(x_prompt, x_sample, mem_prompt, cache_diff_k, cache_diff_v, state_hgrn, cache_mem_k, cache_mem_v, ln_mix_g, w_in, da_lambda, da_subln_g, hg_lb_logits, hg_gnorm_g, w_out, ln_mem_g, mem_norm_g, w_mem_q, w_mem_k, w_mem_v, w_mem_o, ln_ffn_g, router_group_w, router_group_b, router_expert_w, router_expert_b, exp_w_gate, exp_w_up, exp_w_down, final_norm_g):
    assert w_in.shape[0] == 1, "single-layer configuration"
    lb_all = jnp.cumsum(jax.nn.softmax(hg_lb_logits.astype(F32), axis=0), axis=0)
    lp = da_lambda[0].astype(F32)
    lam = jnp.exp(jnp.sum(lp[0] * lp[1])) - jnp.exp(jnp.sum(lp[2] * lp[3])) + LAM_INIT
    w_r = jnp.zeros((D_MODEL, HEAD_W), F32)
    w_r = w_r.at[:, :N_GROUPS].set(router_group_w[0]).at[:, N_GROUPS:N_GROUPS + N_EXPERTS].set(router_expert_w[0])
    b_r = jnp.zeros((1, HEAD_W), F32)
    b_r = b_r.at[0, :N_GROUPS].set(router_group_b[0]).at[0, N_GROUPS:N_GROUPS + N_EXPERTS].set(router_expert_b[0])
    p = {
        "ln_mix": ln_mix_g[0], "w_in": w_in[0].astype(BF16), "lb": lb_all[0],
        "slopes": jnp.exp2(-8.0 * jnp.arange(1, N_HEADS + 1, dtype=F32) / N_HEADS),
        "lam": lam.reshape(1), "da_subln": da_subln_g[0], "hg_gnorm": hg_gnorm_g[0],
        "w_out": w_out[0].astype(BF16), "ln_mem": ln_mem_g[0], "w_mem_q": w_mem_q[0].astype(BF16),
        "w_mem_o": w_mem_o[0].astype(BF16), "ln_ffn": ln_ffn_g[0], "w_r": w_r.astype(BF16), "b_r": b_r,
        "e_gate": exp_w_gate[0].astype(BF16), "e_up": exp_w_up[0].astype(BF16),
        "e_down": exp_w_down[0].astype(BF16), "final_g": final_norm_g,
    }
    Bp, Tp, _ = x_prompt.shape
    Bs, Ts, _ = x_sample.shape

    mkf, mvf, mkb, mvb = _memory_kv(mem_prompt, mem_norm_g[0], w_mem_k[0].astype(BF16),
                                    w_mem_v[0].astype(BF16))
    zero_state = jnp.zeros((Bp, N_HEADS, HEAD_W, HEAD_W), F32)
    yp, kp, vp, sp = _layer(x_prompt, None, None, zero_state, mkb, mvb, p, PROMPT_CFG)
    ys, ks, vs, ss = _layer(
        x_sample, cache_diff_k[0], cache_diff_v[0], state_hgrn[0], cache_mem_k[0], cache_mem_v[0],
        p, SAMPLE_CFG)
    return (yp, ys, kp, vp, sp, mkf[None], mvf[None], ks, vs, ss)
```

```python
import functools
import math

import jax
import jax.numpy as jnp
from jax import lax
from jax.experimental import pallas as pl
from jax.experimental.pallas import tpu as pltpu

F32 = jnp.float32
BF16 = jnp.bfloat16

D_MODEL = 1024
EPS = 1e-5
CHUNK = 64
N_HEADS = 4
HEAD_W = 128
QK_DIM = 64
GROUP_W = N_HEADS * HEAD_W
N_PROJ = 7
DA_SCALE = QK_DIM ** -0.5
LOG2E = 1.4426950408889634
LAM_INIT = 0.8 - 0.6 * math.exp(-0.3 * 0)
N_MEM = 256
MEM_HEADS = 4
MEM_HEAD_DIM = D_MODEL // MEM_HEADS
MEM_SCALE = MEM_HEAD_DIM ** -0.5
N_GROUPS = 4
EXPERTS_PER_GROUP = 8
N_EXPERTS = N_GROUPS * EXPERTS_PER_GROUP
TOP_K = 2
EXPERT_FF = 512
ROUTE_W = 8
SUB = 16
SUB_KEYS = 256
N_SCORE_SLOTS = 3
N_SLAB = D_MODEL // HEAD_W
NEG_INF = float("-inf")

VMEM_LIMIT = 48 * 1024 * 1024


def _cparams(sem):
    return pltpu.CompilerParams(dimension_semantics=sem, vmem_limit_bytes=VMEM_LIMIT)


def _rms(x, g):
    return x * lax.rsqrt(jnp.mean(x * x, axis=-1, keepdims=True) + EPS) * g


def _dot(a, b):
    return jnp.dot(a, b, preferred_element_type=F32)


def _dot_nt(a, b):
    return lax.dot_general(a, b, (((1,), (1,)), ((), ())), preferred_element_type=F32)


def _dot_tn(a, b):
    return lax.dot_general(a, b, (((0,), (0,)), ((), ())), preferred_element_type=F32)


def _run_skewed(gens):
    waiting, live = list(gens), []
    while waiting or live:
        if waiting:
            live.append(waiting.pop(0))
        for gen in list(live):
            if next(gen, "done") == "done":
                live.remove(gen)


def _memkv_kernel(m_ref, g_ref, wk_ref, wv_ref, kf_ref, vf_ref, kb_ref, vb_ref):
    mn = _rms(m_ref[0], g_ref[...]).astype(BF16)
    k = _dot(mn, wk_ref[...])
    v = _dot(mn, wv_ref[...])
    kf_ref[0] = k.reshape(N_MEM, MEM_HEADS, MEM_HEAD_DIM)
    vf_ref[0] = v.reshape(N_MEM, MEM_HEADS, MEM_HEAD_DIM)
    kb_ref[0] = k.astype(BF16)
    vb_ref[0] = v.astype(BF16)


def _memory_kv(mem, g, wk, wv):
    B = mem.shape[0]
    blk = pl.BlockSpec((1, N_MEM, D_MODEL), lambda b: (b, 0, 0))
    hblk = pl.BlockSpec((1, N_MEM, MEM_HEADS, MEM_HEAD_DIM), lambda b: (b, 0, 0, 0))
    wspec = pl.BlockSpec((D_MODEL, D_MODEL), lambda b: (0, 0))
    return pl.pallas_call(
        _memkv_kernel,
        grid=(B,),
        in_specs=[blk, pl.BlockSpec((1, D_MODEL), lambda b: (0, 0)), wspec, wspec],
        out_specs=[hblk, hblk, blk, blk],
        out_shape=[jax.ShapeDtypeStruct((B, N_MEM, MEM_HEADS, MEM_HEAD_DIM), F32)] * 2
                  + [jax.ShapeDtypeStruct(mem.shape, BF16)] * 2,
        compiler_params=_cparams(("parallel",)),
        name="memory_kv",
    )(mem, g.reshape(1, D_MODEL), wk, wv)


def _in_proj_kernel(x_ref, g_ref, w_ref, lb_ref, q_ref, kf_ref, vf_ref, kb_ref, vt_ref,
                    qh_ref, lf_ref, kk_ref, hi_ref, sg_ref, *, nb, tm):
    rows = nb * tm
    h = _rms(x_ref[...].reshape(rows, D_MODEL), g_ref[...]).astype(BF16)

    def proj(i):
        return _dot(h, w_ref[:, i * GROUP_W:(i + 1) * GROUP_W])

    def put(ref, val):
        ref[...] = val.reshape(nb, tm, GROUP_W).astype(ref.dtype)

    def put_heads(ref, val):
        ref[...] = val.reshape(nb, tm, N_HEADS, HEAD_W)

    put(q_ref, proj(0) * (DA_SCALE * LOG2E))
    dk = proj(1)
    put_heads(kf_ref, dk)
    put(kb_ref, dk)
    dv = proj(2)
    put_heads(vf_ref, dv)
    for b in range(nb):
        vt_ref[b] = dv[b * tm:(b + 1) * tm].T.astype(BF16)
    hq = proj(3)
    put(qh_ref, hq * jax.nn.sigmoid(hq))
    hf = proj(4)
    lb = lb_ref[...]
    put(lf_ref, jnp.log(lb + (1.0 - lb) * jax.nn.sigmoid(hf)))
    put(kk_ref, (1.0 - lb) * jax.nn.sigmoid(-hf))
    put(hi_ref, proj(5))
    hg = proj(6)
    put(sg_ref, hg * jax.nn.sigmoid(hg))


def _in_proj(x, g, w_bf, lb, nb, tm):
    B, T, _ = x.shape
    tok = lambda w: pl.BlockSpec((nb, tm, w), lambda b, t: (b, t, 0))
    fix = lambda s: pl.BlockSpec(s, lambda b, t: (0, 0))
    f32o = jax.ShapeDtypeStruct((B, T, GROUP_W), F32)
    bfo = jax.ShapeDtypeStruct((B, T, GROUP_W), BF16)
    o_spec = tok(GROUP_W)
    ho = jax.ShapeDtypeStruct((B, T, N_HEADS, HEAD_W), F32)
    h_spec = pl.BlockSpec((nb, tm, N_HEADS, HEAD_W), lambda b, t: (b, t, 0, 0))
    vt_spec = pl.BlockSpec((nb, GROUP_W, tm), lambda b, t: (b, 0, t))
    return pl.pallas_call(
        functools.partial(_in_proj_kernel, nb=nb, tm=tm),
        grid=(B // nb, T // tm),
        in_specs=[tok(D_MODEL), fix((1, D_MODEL)), fix((D_MODEL, N_PROJ * GROUP_W)), fix((1, GROUP_W))],
        out_specs=[o_spec, h_spec, h_spec, o_spec, vt_spec] + [o_spec] * 5,
        out_shape=[bfo, ho, ho, bfo, jax.ShapeDtypeStruct((B, GROUP_W, T), BF16),
                   bfo, f32o, bfo, bfo, bfo],
        compiler_params=_cparams(("parallel", "parallel")),
        name="in_proj",
    )(x, g.reshape(1, D_MODEL), w_bf, lb.reshape(1, GROUP_W))


def _attn_kernel(slope_ref, lam_ref, q_ref, k_ref, vt_ref, g_ref, o_ref,
                 m_ref, l_ref, acc_ref, base_ref, t_ref, mb_ref, *, tq, tk):
    h = pl.program_id(1)
    qi = pl.program_id(2)
    slope2 = slope_ref[h] * LOG2E
    qt = q_ref[0].astype(F32).T.astype(BF16)
    zeros = jnp.zeros((QK_DIM, tq), BF16)
    qt_maps = (jnp.concatenate([qt[:QK_DIM], zeros], axis=0),
               jnp.concatenate([zeros, qt[QK_DIM:]], axis=0))
    m_ref[...] = jnp.full(m_ref.shape, NEG_INF, F32)
    l_ref[...] = jnp.zeros(l_ref.shape, F32)
    acc_ref[...] = jnp.zeros(acc_ref.shape, F32)
    q0 = qi * tq

    def lane_pad(x, lo, fill):
        if lo == 0:
            return x
        return jnp.concatenate([jnp.full((x.shape[0], lo), fill, x.dtype), x], axis=1)


    def score_phase(slot, n_keys, get_k, get_bias, shift, q_lo=lambda u: 0):
        sub = min(n_keys, SUB_KEYS)
        mx = [None, None]
        for u in range(n_keys // sub):
            lo = q_lo(u)
            bias, visible = get_bias(u)
            k = get_k(u)
            for c in range(2):
                t = _dot(k, qt_maps[c][:, lo:]) + bias
                if visible is not None:
                    w = visible.shape[1]
                    head = jnp.where(visible, t[:, :w], NEG_INF)
                    t = head if w == t.shape[1] else jnp.concatenate([head, t[:, w:]], axis=1)
                t_ref[slot, c, u * sub:(u + 1) * sub, lo:] = t
                mu = lane_pad(jnp.max(t, axis=0, keepdims=True), lo, NEG_INF)
                mx[c] = mu if mx[c] is None else jnp.maximum(mx[c], mu)
            yield
        for c in range(2):
            mb_ref[slot, c] = mx[c] + shift
        yield

    def prob_phase(slot, n_keys, get_vt, shift, q_lo=lambda u: 0, pv_group=1):
        sub = min(n_keys, SUB_KEYS)
        n_sub = n_keys // sub
        off, alpha = [], []
        for c in range(2):
            m_old = m_ref[c]
            m_new = jnp.maximum(m_old, mb_ref[slot, c])
            off.append(m_new - shift)
            alpha.append(jnp.exp2(m_old - m_new))
            m_ref[c] = m_new
        yield
        lsum = [None, None]
        pv = [None, None]
        grp = pv_group if n_sub % pv_group == 0 else 1
        for u in range(0, n_sub, grp):
            lo = q_lo(u)
            assert all(q_lo(u + i) == lo for i in range(grp))
            parts = [get_vt(u + i) for i in range(grp)]
            vt = parts[0] if grp == 1 else jnp.concatenate(parts, axis=1)
            for c in range(2):
                p = jnp.exp2(t_ref[slot, c, u * sub:(u + grp) * sub, lo:] - off[c][:, lo:])
                ls = lane_pad(jnp.sum(p, axis=0, keepdims=True), lo, 0.0)
                pu = lane_pad(_dot(vt, p.astype(BF16)), lo, 0.0)
                lsum[c] = ls if lsum[c] is None else lsum[c] + ls
                pv[c] = pu if pv[c] is None else pv[c] + pu
            yield
        for c in range(2):
            l_ref[c] = alpha[c] * l_ref[c] + lsum[c]
            acc_ref[c] = alpha[c] * acc_ref[c] + pv[c]
        yield

    def run(*gens):
        live = list(gens)
        while live:
            for gen in list(live):
                if next(gen, "done") == "done":
                    live.remove(gen)

    def update(n_keys, get_k, get_vt, get_bias, shift, q_lo=lambda u: 0, pv_group=1):
        run(score_phase(0, n_keys, get_k, get_bias, shift, q_lo))
        run(prob_phase(0, n_keys, get_vt, shift, q_lo, pv_group))

    def rel_bias(n_keys, key0):
        key = lax.broadcasted_iota(jnp.int32, (n_keys, tq), 0) + key0
        qry = lax.broadcasted_iota(jnp.int32, (n_keys, tq), 1)
        return (qry - key).astype(F32) * (-slope2)

    def block_shift(k0):
        return (q0 - k0).astype(F32) * (-slope2)

    n_q = k_ref.shape[1] // tq
    if n_q > 1 or tq > SUB_KEYS:
        @pl.when(qi == 0)
        def _():
            base_ref[...] = rel_bias(base_ref.shape[0], 0)

    kd = 0 if n_q == 1 else pl.multiple_of(qi * tq, tq)
    sub_d = min(tq, SUB_KEYS)
    key = lax.broadcasted_iota(jnp.int32, (sub_d, sub_d), 0)
    qry = lax.broadcasted_iota(jnp.int32, (sub_d, sub_d), 1)
    strip_bias = jnp.abs(qry - key).astype(F32) * (-slope2)
    strip_visible = (key // CHUNK) <= (qry // CHUNK)

    def diag_bias(u):
        lo = (u + 1) * sub_d
        if lo == tq:
            return strip_bias, strip_visible
        later = base_ref[u * sub_d:(u + 1) * sub_d, lo:]
        return jnp.concatenate([strip_bias, later], axis=1), strip_visible

    diag_k = lambda u: k_ref[0, pl.ds(kd + u * sub_d, sub_d), :]
    diag_vt = lambda u: vt_ref[0, :, pl.ds(kd + u * sub_d, sub_d)]
    diag_lo = lambda u: u * sub_d

    if n_q == 1:
        update(tq, diag_k, diag_vt, diag_bias, 0.0, q_lo=diag_lo)
    else:
        sub_k = min(tk, SUB_KEYS)
        n_prev = qi * (tq // tk)
        assert (tq // tk) % 2 == 0, "earlier blocks are consumed in pairs"

        def prev_scores(slot, j):
            k0 = pl.multiple_of(j * tk, tk)
            return score_phase(slot, tk, lambda u: k_ref[0, pl.ds(k0 + u * sub_k, sub_k), :],
                               lambda u: (base_ref[u * sub_k:(u + 1) * sub_k, :], None),
                               block_shift(k0))

        def prev_probs(slot, j):
            k0 = pl.multiple_of(j * tk, tk)
            return prob_phase(slot, tk, lambda u: vt_ref[0, :, pl.ds(k0 + u * sub_k, sub_k)],
                              block_shift(k0), pv_group=2)

        run(score_phase(0, tq, diag_k, diag_bias, 0.0, diag_lo))
        run(prob_phase(0, tq, diag_vt, 0.0, diag_lo), prev_scores(1, 0))

        def pair_body(i, carry):
            j = 2 * i
            run(prev_probs(1, j), prev_scores(2, j + 1))
            run(prev_probs(2, j + 1), prev_scores(1, jnp.minimum(j + 2, n_prev - 1)))
            return carry
        lax.fori_loop(0, n_prev // 2, pair_body, 0)

    lam = lam_ref[0]
    ot = acc_ref[0] / l_ref[0] - lam * (acc_ref[1] / l_ref[1])
    ot = ot * lax.rsqrt(jnp.mean(ot * ot, axis=0, keepdims=True) + EPS)
    o_ref[0] = (ot.T * g_ref[...] * (1.0 - LAM_INIT)).astype(o_ref.dtype)


def _diff_attn(q, k, vt, slopes, lam, subln_g, tq, tk):
    B, T, _ = q.shape
    smem = pl.BlockSpec(memory_space=pltpu.SMEM)
    qspec = pl.BlockSpec((1, tq, HEAD_W), lambda b, h, i: (b, i, h))
    return pl.pallas_call(
        functools.partial(_attn_kernel, tq=tq, tk=tk),
        grid=(B, N_HEADS, T // tq),
        in_specs=[smem, smem, qspec,
                  pl.BlockSpec((1, T, HEAD_W), lambda b, h, i: (b, 0, h)),
                  pl.BlockSpec((1, HEAD_W, T), lambda b, h, i: (b, h, 0)),
                  pl.BlockSpec((1, HEAD_W), lambda b, h, i: (0, 0))],
        out_specs=qspec,
        out_shape=jax.ShapeDtypeStruct((B, T, GROUP_W), BF16),
        scratch_shapes=[pltpu.VMEM((2, 1, tq), F32), pltpu.VMEM((2, 1, tq), F32),
                        pltpu.VMEM((2, HEAD_W, tq), F32), pltpu.VMEM((max(tk, tq), tq), F32),
                        pltpu.VMEM((N_SCORE_SLOTS, 2, max(tk, tq), tq), F32),
                        pltpu.VMEM((N_SCORE_SLOTS, 2, 1, tq), F32)],
        compiler_params=_cparams(("parallel", "parallel", "arbitrary")),
        name="diff_attn",
    )(slopes, lam, q, k, vt, subln_g.reshape(1, HEAD_W))


def _attn_step_kernel(slope_ref, lam_ref, q_ref, k_ref, vt_ref, pk_ref, pv_ref, g_ref, o_ref, *,
                      t_new, past_len):
    assert 2 * t_new <= HEAD_W and t_new % CHUNK == 0 and past_len % CHUNK == 0
    pk = pk_ref[0].reshape(past_len, GROUP_W)
    pv = pv_ref[0].reshape(past_len, GROUP_W)
    lam = lam_ref[0]
    qry2 = lax.broadcasted_iota(jnp.int32, (1, 2 * t_new), 1) % t_new
    key_p = lax.broadcasted_iota(jnp.int32, (past_len, 2 * t_new), 0)
    key_n = lax.broadcasted_iota(jnp.int32, (t_new, 2 * t_new), 0)
    dist_p = (qry2 + past_len - key_p).astype(F32)
    dist_n = jnp.abs(qry2 - key_n).astype(F32)
    visible = (key_n // CHUNK) <= (qry2 // CHUNK)
    zeros = jnp.zeros((QK_DIM, t_new), BF16)
    for h in range(N_HEADS):
        hs = slice(h * HEAD_W, (h + 1) * HEAD_W)
        slope2 = slope_ref[h] * LOG2E
        qt = q_ref[0, :, hs].astype(F32).T.astype(BF16)
        qt2 = jnp.concatenate([jnp.concatenate([qt[:QK_DIM], zeros], axis=0),
                               jnp.concatenate([zeros, qt[QK_DIM:]], axis=0)], axis=1)
        s_p = _dot(pk[:, hs].astype(BF16), qt2) - slope2 * dist_p
        s_n = jnp.where(visible, _dot(k_ref[0, :, hs], qt2) - slope2 * dist_n, NEG_INF)
        m = jnp.maximum(jnp.max(s_p, axis=0, keepdims=True), jnp.max(s_n, axis=0, keepdims=True))
        p_p = jnp.exp2(s_p - m)
        p_n = jnp.exp2(s_n - m)
        l = jnp.sum(p_p, axis=0, keepdims=True) + jnp.sum(p_n, axis=0, keepdims=True)
        acc = (_dot_tn(pv[:, hs].astype(BF16), p_p.astype(BF16))
               + _dot(vt_ref[0, hs, :], p_n.astype(BF16))) / l
        ot = acc[:, :t_new] - lam * acc[:, t_new:]
        ot = ot * lax.rsqrt(jnp.mean(ot * ot, axis=0, keepdims=True) + EPS)
        o_ref[0, :, hs] = (ot.T * g_ref[...] * (1.0 - LAM_INIT)).astype(o_ref.dtype)


def _diff_attn_step(q, k, vt, past_k, past_v, slopes, lam, subln_g):
    B, T, _ = q.shape
    past_len = past_k.shape[1]
    smem = pl.BlockSpec(memory_space=pltpu.SMEM)
    tok = pl.BlockSpec((1, T, GROUP_W), lambda b: (b, 0, 0))
    pspec = pl.BlockSpec((1, past_len, N_HEADS, HEAD_W), lambda b: (b, 0, 0, 0))
    return pl.pallas_call(
        functools.partial(_attn_step_kernel, t_new=T, past_len=past_len),
        grid=(B,),
        in_specs=[smem, smem, tok, tok, pl.BlockSpec((1, GROUP_W, T), lambda b: (b, 0, 0)),
                  pspec, pspec, pl.BlockSpec((1, HEAD_W), lambda b: (0, 0))],
        out_specs=tok,
        out_shape=jax.ShapeDtypeStruct((B, T, GROUP_W), BF16),
        compiler_params=_cparams(("parallel",)),
        name="diff_attn_step",
    )(slopes, lam, q, k, vt, past_k, past_v, subln_g.reshape(1, HEAD_W))


def _split3(x):
    hi = x.astype(BF16)
    r = x - hi.astype(F32)
    mid = r.astype(BF16)
    lo = (r - mid.astype(F32)).astype(BF16)
    return hi, mid, lo


def _hgrn_kernel(q_ref, k_ref, lf_ref, v_ref, sg_ref, gn_ref, s0_ref, o_ref, sfin_ref,
                 st_ref, b_ref, *, tb):
    t = pl.program_id(1)
    n_chunks = tb // CHUNK

    @pl.when(t == 0)
    def _():
        for h in range(N_HEADS):
            st_ref[h] = s0_ref[0, h].T

    row = lax.broadcasted_iota(jnp.int32, (tb, tb), 0)
    col = lax.broadcasted_iota(jnp.int32, (tb, tb), 1)
    tri = jnp.where((col <= row) & (col // CHUNK == row // CHUNK), 1.0, 0.0).astype(BF16)
    hi, mid, lo = _split3(lf_ref[0] * LOG2E)
    b_ref[...] = _dot(tri, hi) + _dot(tri, mid) + _dot(tri, lo)

    sub_row = lax.broadcasted_iota(jnp.int32, (8, HEAD_W), 0)
    zeros16 = jnp.zeros((SUB, HEAD_W), F32)

    def pad_rows(x, r0):
        parts = []
        if r0:
            parts.append(jnp.zeros((r0, HEAD_W), F32))
        parts.append(x)
        rest = CHUNK - r0 - x.shape[0]
        if rest:
            parts.append(jnp.zeros((rest, HEAD_W), F32))
        return jnp.concatenate(parts, axis=0) if len(parts) > 1 else x

    def chunk_body(c, carry):
        r0 = pl.multiple_of(c * CHUNK, CHUNK)
        def head(h):
            hs = slice(h * HEAD_W, (h + 1) * HEAD_W)
            q = q_ref[0, pl.ds(r0, CHUNK), hs].astype(F32)
            k = k_ref[0, pl.ds(r0, CHUNK), hs].astype(F32)
            v_bf = v_ref[0, pl.ds(r0, CHUNK), hs]
            v = v_bf.astype(F32)
            b = b_ref[pl.ds(r0, CHUNK), hs]
            b_last = b[CHUNK - 1:CHUNK, :]
            st = st_ref[h]

            o = _dot_nt((q * jnp.exp2(b)).astype(BF16), st.astype(BF16))

            q_segs, k_segs = [], []
            for i in range(1, CHUNK // SUB):
                lo_r, hi_r = i * SUB, (i + 1) * SUB
                ref_b = b[lo_r - 1:lo_r, :]
                q_segs.append(pad_rows(q[lo_r:hi_r] * jnp.exp2(b[lo_r:hi_r] - ref_b), lo_r))
                k_segs.append(pad_rows(k[:lo_r] * jnp.exp2(ref_b - b[:lo_r]), 0))
            a_off = _dot_nt(jnp.concatenate(q_segs, axis=1).astype(BF16),
                            jnp.concatenate(k_segs, axis=1).astype(BF16))
            o = o + _dot(a_off.astype(BF16), v_bf)
            yield

            diag = []
            for blk in range(CHUNK // SUB):
                base = blk * SUB
                qa, qb = q[base:base + 8], q[base + 8:base + SUB]
                ba, bb = b[base:base + 8], b[base + 8:base + SUB]
                oa = jnp.zeros((8, HEAD_W), F32)
                ob = jnp.zeros((8, HEAD_W), F32)
                for s in range(SUB):
                    ks = k[base + s:base + s + 1]
                    bs = b[base + s:base + s + 1]
                    vs = v[base + s:base + s + 1]
                    if s < 8:
                        e = jnp.exp2(ba - bs)
                        if s:
                            e = jnp.where(sub_row >= s, e, 0.0)
                        oa = oa + jnp.sum(qa * ks * e, axis=-1, keepdims=True) * vs
                        ob = ob + jnp.sum(qb * ks * jnp.exp2(bb - bs), axis=-1, keepdims=True) * vs
                    else:
                        e = jnp.exp2(bb - bs)
                        if s > 8:
                            e = jnp.where(sub_row >= s - 8, e, 0.0)
                        ob = ob + jnp.sum(qb * ks * e, axis=-1, keepdims=True) * vs
                diag += [oa, ob]
                yield
            o = o + jnp.concatenate(diag, axis=0)

            k_dec = (k * jnp.exp2(b_last - b)).astype(BF16)
            st_ref[h] = st * jnp.exp2(b_last) + _dot_tn(v_bf, k_dec)

            out = _rms(o, gn_ref[...]) * sg_ref[0, pl.ds(r0, CHUNK), hs]
            o_ref[0, pl.ds(r0, CHUNK), hs] = out.astype(o_ref.dtype)
            yield

        _run_skewed([head(h) for h in range(N_HEADS)])
        return carry

    lax.fori_loop(0, n_chunks, chunk_body, 0)

    @pl.when(t == pl.num_programs(1) - 1)
    def _():
        for h in range(N_HEADS):
            sfin_ref[0, h] = st_ref[h].T


def _hgrn(qh, kk, lf, hi, sg, gnorm_g, s0, tb):
    B, T, _ = qh.shape
    tspec = pl.BlockSpec((1, tb, GROUP_W), lambda b, t: (b, t, 0))
    sspec = pl.BlockSpec((1, N_HEADS, HEAD_W, HEAD_W), lambda b, t: (b, 0, 0, 0))
    return pl.pallas_call(
        functools.partial(_hgrn_kernel, tb=tb),
        grid=(B, T // tb),
        in_specs=[tspec] * 5 + [pl.BlockSpec((1, HEAD_W), lambda b, t: (0, 0)), sspec],
        out_specs=[tspec, sspec],
        out_shape=[jax.ShapeDtypeStruct((B, T, GROUP_W), BF16),
                   jax.ShapeDtypeStruct((B, N_HEADS, HEAD_W, HEAD_W), F32)],
        scratch_shapes=[pltpu.VMEM((N_HEADS, HEAD_W, HEAD_W), F32), pltpu.VMEM((tb, GROUP_W), F32)],
        compiler_params=_cparams(("parallel", "arbitrary")),
        name="hgrn",
    )(qh, kk, lf, hi, sg, gnorm_g.reshape(1, HEAD_W), s0)


def _post_kernel(x_ref, oda_ref, ohg_ref, mk_ref, mv_ref, wout_ref, lnm_ref, wq_ref, wo_ref,
                 lnf_ref, wr_ref, br_ref, x2_ref, h3_ref, route_ref, om_ref, *, nb, tm, split):
    gt = tm // split
    groups = [_post_group(slice(g * gt, (g + 1) * gt), g * nb * gt, x_ref, oda_ref, ohg_ref,
                          mk_ref, mv_ref, wout_ref, lnm_ref, wq_ref, wo_ref, lnf_ref, wr_ref,
                          br_ref, x2_ref, h3_ref, route_ref, om_ref, nb=nb, gt=gt)
              for g in range(split)]
    _run_skewed(groups)


def _post_group(ts, om0, x_ref, oda_ref, ohg_ref, mk_ref, mv_ref, wout_ref, lnm_ref, wq_ref, wo_ref,
                lnf_ref, wr_ref, br_ref, x2_ref, h3_ref, route_ref, om_ref, *, nb, gt):
    rows = nb * gt
    x = x_ref[:, ts, :].reshape(rows, D_MODEL)
    mixed = (_dot(oda_ref[:, ts, :].reshape(rows, GROUP_W), wout_ref[:GROUP_W, :])
             + _dot(ohg_ref[:, ts, :].reshape(rows, GROUP_W), wout_ref[GROUP_W:, :]))
    x1 = x + mixed
    yield

    hm = _rms(x1, lnm_ref[...]).astype(BF16)
    yield
    qm = (_dot(hm, wq_ref[...]) * MEM_SCALE).astype(BF16)
    yield
    for b in range(nb):
        if len(mk_ref.shape) == 4:
            mk_b = mk_ref[b].reshape(N_MEM, D_MODEL).astype(BF16)
            mv_b = mv_ref[b].reshape(N_MEM, D_MODEL).astype(BF16)
        for h in range(MEM_HEADS):
            hs = slice(h * MEM_HEAD_DIM, (h + 1) * MEM_HEAD_DIM)
            if len(mk_ref.shape) == 4:
                mk, mv = mk_b[:, hs], mv_b[:, hs]
            else:
                mk, mv = mk_ref[b, :, hs], mv_ref[b, :, hs]
            s = _dot_nt(qm[b * gt:(b + 1) * gt, hs], mk.astype(BF16))
            e = jnp.exp(s - jnp.max(s, axis=-1, keepdims=True))
            p = e / jnp.sum(e, axis=-1, keepdims=True)
            om_ref[om0 + b * gt:om0 + (b + 1) * gt, hs] = _dot(
                p.astype(BF16), mv.astype(BF16)).astype(BF16)
    yield
    x2 = x1 + _dot(om_ref[om0:om0 + rows, :], wo_ref[...])
    x2_ref[:, ts, :] = x2.reshape(nb, gt, D_MODEL)
    yield

    h3 = _rms(x2, lnf_ref[...])
    h3_ref[:, ts] = h3.reshape(nb, gt, N_SLAB, HEAD_W)
    yield

    r = _dot(h3.astype(BF16), wr_ref[...]) + br_ref[...]
    lane = lax.broadcasted_iota(jnp.int32, r.shape, 1).astype(F32)
    big = float(4 * HEAD_W)
    g_mask = lane < N_GROUPS
    gl = jnp.where(g_mask, r, NEG_INF)
    g_max = jnp.max(gl, axis=-1, keepdims=True)
    g_idx = jnp.min(jnp.where(gl == g_max, lane, big), axis=-1, keepdims=True)
    g_w = 1.0 / jnp.sum(jnp.where(g_mask, jnp.exp(r - g_max), 0.0), axis=-1, keepdims=True)
    e_lo = N_GROUPS + EXPERTS_PER_GROUP * g_idx
    el = jnp.where((lane >= e_lo) & (lane < e_lo + EXPERTS_PER_GROUP), r, NEG_INF)
    v1 = jnp.max(el, axis=-1, keepdims=True)
    i1 = jnp.min(jnp.where(el == v1, lane, big), axis=-1, keepdims=True)
    el2 = jnp.where(lane == i1, NEG_INF, el)
    v2 = jnp.max(el2, axis=-1, keepdims=True)
    i2 = jnp.min(jnp.where(el2 == v2, lane, big), axis=-1, keepdims=True)
    t = jnp.exp(v2 - v1)
    p1 = 1.0 / (1.0 + t)
    rec = jnp.where(lane == 0.0, i1 - N_GROUPS,
          jnp.where(lane == 1.0, i2 - N_GROUPS,
          jnp.where(lane == 2.0, p1 * g_w,
          jnp.where(lane == 3.0, t * p1 * g_w, 0.0))))
    route_ref[:, ts, :] = rec[:, :ROUTE_W].reshape(nb, gt, ROUTE_W)


def _post_mix(x, oda, ohg, mem_k, mem_v, w_out, ln_mem, w_q, w_o, ln_ffn, w_r, b_r, nb, tm, split):
    B, T, _ = x.shape
    tok = lambda w: pl.BlockSpec((nb, tm, w), lambda b, t: (b, t, 0))
    if mem_k.ndim == 4:
        memspec = pl.BlockSpec((nb, N_MEM, MEM_HEADS, MEM_HEAD_DIM), lambda b, t: (b, 0, 0, 0))
    else:
        memspec = pl.BlockSpec((nb, N_MEM, D_MODEL), lambda b, t: (b, 0, 0))
    fix = lambda s: pl.BlockSpec(s, lambda b, t: (0, 0))
    return pl.pallas_call(
        functools.partial(_post_kernel, nb=nb, tm=tm, split=split),
        grid=(B // nb, T // tm),
        in_specs=[tok(D_MODEL), tok(GROUP_W), tok(GROUP_W), memspec, memspec,
                  fix((D_MODEL, D_MODEL)), fix((1, D_MODEL)), fix((D_MODEL, D_MODEL)),
                  fix((D_MODEL, D_MODEL)), fix((1, D_MODEL)), fix((D_MODEL, HEAD_W)),
                  fix((1, HEAD_W))],
        out_specs=[tok(D_MODEL),
                   pl.BlockSpec((nb, tm, N_SLAB, HEAD_W), lambda b, t: (b, t, 0, 0)),
                   tok(ROUTE_W)],
        out_shape=[jax.ShapeDtypeStruct((B, T, D_MODEL), F32),
                   jax.ShapeDtypeStruct((B, T, N_SLAB, HEAD_W), F32),
                   jax.ShapeDtypeStruct((B, T, ROUTE_W), F32)],
        scratch_shapes=[pltpu.VMEM((nb * tm, D_MODEL), BF16)],
        compiler_params=_cparams(("parallel", "parallel")),
        name="post_mix",
    )(x, oda, ohg, mem_k, mem_v, w_out, ln_mem.reshape(1, D_MODEL), w_q, w_o,
      ln_ffn.reshape(1, D_MODEL), w_r, b_r)


def _experts_kernel(blk_e_ref, n_used_ref, src_next_ref, dst_prev_ref, src0_ref, dst_ref, h_hbm,
                    wg_ref, wu_ref, wd_ref, y_hbm, xbuf, ybuf, gsem, ssem, *, bm, m_tot):
    del blk_e_ref
    i = pl.program_id(0)
    n_used = n_used_ref[0]
    slot = i % 2

    def gather_row(tab_ref, r, s, priority=0):
        pltpu.make_async_copy(h_hbm.at[tab_ref[0, 0, r]], xbuf.at[s, r],
                              gsem.at[s]).start(priority=priority)

    def scatter_row(tab_ref, r, s, priority=0):
        pltpu.make_async_copy(ybuf.at[s, r], y_hbm.at[tab_ref[0, 0, r]],
                              ssem.at[s]).start(priority=priority)

    def wait_block(buf, sem, s):
        pltpu.make_async_copy(buf.at[s], buf.at[s], sem.at[s]).wait()

    def spare_fill(s):
        return pltpu.make_async_copy(ybuf.at[s], y_hbm.at[pl.ds(m_tot + s * bm, bm)], ssem.at[s])

    @pl.when(i == 0)
    def _():
        ybuf[...] = jnp.zeros(ybuf.shape, F32)
        spare_fill(0).start()
        spare_fill(1).start()
        spare_fill(1).wait()

        def body(r, c):
            gather_row(src0_ref, r, 0)
            return c
        lax.fori_loop(0, bm, body, 0)

    @pl.when(i < n_used)
    def _():
        wait_block(xbuf, gsem, slot)
        wait_block(ybuf, ssem, slot)
        x = xbuf[slot].reshape(bm, D_MODEL).astype(BF16)
        hg = _dot(x, wg_ref[0])
        hu = _dot(x, wu_ref[0])
        hb = (hg * jax.nn.sigmoid(hg) * hu).astype(BF16)
        ybuf[slot] = _dot(hb, wd_ref[0]).reshape(bm, N_SLAB, HEAD_W)
        for r in range(bm):
            gather_row(src_next_ref, r, 1 - slot, priority=r % 2)
            scatter_row(dst_prev_ref, r, 1 - slot, priority=(r + 1) % 2)

    @pl.when(i == n_used - 1)
    def _():
        def body(r, c):
            scatter_row(dst_ref, r, slot)
            return c
        lax.fori_loop(0, bm, body, 0)
        wait_block(xbuf, gsem, 1 - slot)
        wait_block(ybuf, ssem, 1 - slot)
        wait_block(ybuf, ssem, slot)


def _experts(h3, blk_e, n_used, src_tab, dst_tab, dst_prev_tab, wg, wu, wd, bm):
    n_blocks = src_tab.shape[0]
    m_tot = TOP_K * h3.shape[0]
    tab = lambda f: pl.BlockSpec((1, 1, bm), f, memory_space=pltpu.SMEM)
    cur = lambda i, e, n: (i, 0, 0)
    nxt = lambda i, e, n: (jnp.minimum(i + 1, n_blocks - 1), 0, 0)
    first = lambda i, e, n: (0, 0, 0)
    wspec = lambda a, b: pl.BlockSpec((1, a, b), lambda i, e, n: (e[i], 0, 0))
    grid_spec = pltpu.PrefetchScalarGridSpec(
        num_scalar_prefetch=2,
        grid=(n_blocks,),
        in_specs=[tab(nxt), tab(cur), tab(first), tab(cur), pl.BlockSpec(memory_space=pl.ANY),
                  wspec(D_MODEL, EXPERT_FF), wspec(D_MODEL, EXPERT_FF), wspec(EXPERT_FF, D_MODEL)],
        out_specs=pl.BlockSpec(memory_space=pl.ANY),
        scratch_shapes=[pltpu.VMEM((2, bm, N_SLAB, HEAD_W), F32), pltpu.VMEM((2, bm, N_SLAB, HEAD_W), F32),
                        pltpu.SemaphoreType.DMA((2,)), pltpu.SemaphoreType.DMA((2,))],
    )
    return pl.pallas_call(
        functools.partial(_experts_kernel, bm=bm, m_tot=m_tot),
        grid_spec=grid_spec,
        out_shape=jax.ShapeDtypeStruct((m_tot + 2 * bm, N_SLAB, HEAD_W), F32),
        compiler_params=_cparams(("arbitrary",)),
        name="experts",
    )(blk_e, n_used, src_tab, dst_prev_tab, src_tab, dst_tab, h3, wg, wu, wd)


def _combine_kernel(x_ref, y0_ref, y1_ref, route_ref, g_ref, o_ref):
    rt = route_ref[...]
    tm = x_ref.shape[0]
    y0 = y0_ref[...].reshape(tm, D_MODEL)
    y1 = y1_ref[...].reshape(tm, D_MODEL)
    y = x_ref[...] + rt[:, 2:3] * y0 + rt[:, 3:4] * y1
    o_ref[...] = _rms(y, g_ref[...])


def _combine(x2, y, route, final_g, tm):
    n = x2.shape[0]
    nt = n // tm
    return pl.pallas_call(
        _combine_kernel,
        grid=(nt,),
        in_specs=[pl.BlockSpec((tm, D_MODEL), lambda i: (i, 0)),
                  pl.BlockSpec((tm, N_SLAB, HEAD_W), lambda i: (i, 0, 0)),
                  pl.BlockSpec((tm, N_SLAB, HEAD_W), lambda i: (i + nt, 0, 0)),
                  pl.BlockSpec((tm, ROUTE_W), lambda i: (i, 0)),
                  pl.BlockSpec((1, D_MODEL), lambda i: (0, 0))],
        out_specs=pl.BlockSpec((tm, D_MODEL), lambda i: (i, 0)),
        out_shape=jax.ShapeDtypeStruct((n, D_MODEL), F32),
        compiler_params=_cparams(("parallel",)),
        name="combine",
    )(x2, y, y, route, final_g.reshape(1, D_MODEL))


def _routing_tables(route, bm):
    n = route.shape[0]
    m_tot = TOP_K * n
    flat_e = jnp.concatenate([route[:, 0], route[:, 1]]).astype(jnp.int32)
    experts = jnp.arange(N_EXPERTS, dtype=jnp.int32)
    counts = jnp.sum(flat_e[:, None] == experts[None, :], axis=0, dtype=jnp.int32)
    n_blk_e = (counts + bm - 1) // bm
    blk_end = jnp.cumsum(n_blk_e)
    n_blocks = m_tot // bm + N_EXPERTS
    blk = jnp.arange(n_blocks, dtype=jnp.int32)
    blk_e = jnp.minimum(jnp.sum(blk[:, None] >= blk_end[None, :], axis=1, dtype=jnp.int32),
                        N_EXPERTS - 1)
    r = jnp.arange(bm, dtype=jnp.int32)[None, :]
    n_pad = (n_blk_e * bm - counts)[:, None]
    pad_keys = jnp.where(r < n_pad, 2 * experts[:, None] + 1, 2 * N_EXPERTS + 1)
    keys = jnp.concatenate([2 * flat_e, pad_keys.reshape(-1)])
    vals = jnp.concatenate([jnp.arange(m_tot, dtype=jnp.int32),
                            jnp.full((N_EXPERTS * bm,), -1, jnp.int32)])
    m = lax.sort((keys, vals), num_keys=1)[1].reshape(n_blocks, bm)
    real = m >= 0
    spare = m_tot + (blk[:, None] % 2) * bm + r
    src = jnp.where(real, m % n, 0)
    dst = jnp.where(real, m, spare)
    dst_prev = jnp.concatenate([m_tot + bm + r, dst[:-1]], axis=0)
    shape = (n_blocks, 1, bm)
    return (blk_e, blk_end[N_EXPERTS - 1:].astype(jnp.int32), src.reshape(shape), dst.reshape(shape),
            dst_prev.reshape(shape))


def _layer(x, past_k, past_v, s0, mem_k, mem_v, p, cfg):
    B, T, _ = x.shape
    n = B * T
    (q, kf, vf, kb, vt, qh, lf, kk, hi, sg) = _in_proj(
        x, p["ln_mix"], p["w_in"], p["lb"], cfg["nb"], cfg["tm_proj"])
    if past_k is None:
        oda = _diff_attn(q, kb, vt, p["slopes"], p["lam"], p["da_subln"], cfg["tq"], cfg["tk"])
    else:
        oda = _diff_attn_step(q, kb, vt, past_k, past_v, p["slopes"], p["lam"], p["da_subln"])
    ohg, s_new = _hgrn(qh, kk, lf, hi, sg, p["hg_gnorm"], s0, cfg["tb"])
    x2, h3, route = _post_mix(x, oda, ohg, mem_k, mem_v, p["w_out"], p["ln_mem"], p["w_mem_q"],
                              p["w_mem_o"], p["ln_ffn"], p["w_r"], p["b_r"], cfg["nb"], cfg["tm_post"],
                              cfg["split"])
    route2 = route.reshape(n, ROUTE_W)
    blk_e, n_used, src_tab, dst_tab, dst_prev_tab = _routing_tables(route2, cfg["bm"])
    y = _experts(h3.reshape(n, N_SLAB, HEAD_W), blk_e, n_used, src_tab, dst_tab, dst_prev_tab,
                 p["e_gate"], p["e_up"], p["e_down"], cfg["bm"])
    out = _combine(x2.reshape(n, D_MODEL), y, route2, p["final_g"], cfg["tm_in"])
    return out.reshape(B, T, D_MODEL), kf[None], vf[None], s_new[None]


PROMPT_CFG = dict(tm_in=256, tq=512, tk=256, tb=256, nb=1, tm_proj=256, tm_post=512, split=2, bm=512)
SAMPLE_CFG = dict(tm_in=256, tq=64, tk=64, tb=64, nb=4, tm_proj=64, tm_post=64, split=1, bm=128)


def kernel(x_prompt, x_sample, mem_prompt, cache_diff_k, cache_diff_v, state_hgrn, cache_mem_k, cache_mem_v, ln_mix_g, w_in, da_lambda, da_subln_g, hg_lb_logits, hg_gnorm_g, w_out, ln_mem_g, mem_norm_g, w_mem_q, w_mem_k, w_mem_v, w_mem_o, ln_ffn_g, router_group_w, router_group_b, router_expert_w, router_expert_b, exp_w_gate, exp_w_up, exp_w_down, final_norm_g):
    assert w_in.shape[0] == 1, "single-layer configuration"
    lb_all = jnp.cumsum(jax.nn.softmax(hg_lb_logits.astype(F32), axis=0), axis=0)
    lp = da_lambda[0].astype(F32)
    lam = jnp.exp(jnp.sum(lp[0] * lp[1])) - jnp.exp(jnp.sum(lp[2] * lp[3])) + LAM_INIT
    w_r = jnp.zeros((D_MODEL, HEAD_W), F32)
    w_r = w_r.at[:, :N_GROUPS].set(router_group_w[0]).at[:, N_GROUPS:N_GROUPS + N_EXPERTS].set(router_expert_w[0])
    b_r = jnp.zeros((1, HEAD_W), F32)
    b_r = b_r.at[0, :N_GROUPS].set(router_group_b[0]).at[0, N_GROUPS:N_GROUPS + N_EXPERTS].set(router_expert_b[0])
    p = {
        "ln_mix": ln_mix_g[0], "w_in": w_in[0].astype(BF16), "lb": lb_all[0],
        "slopes": jnp.exp2(-8.0 * jnp.arange(1, N_HEADS + 1, dtype=F32) / N_HEADS),
        "lam": lam.reshape(1), "da_subln": da_subln_g[0], "hg_gnorm": hg_gnorm_g[0],
        "w_out": w_out[0].astype(BF16), "ln_mem": ln_mem_g[0], "w_mem_q": w_mem_q[0].astype(BF16),
        "w_mem_o": w_mem_o[0].astype(BF16), "ln_ffn": ln_ffn_g[0], "w_r": w_r.astype(BF16), "b_r": b_r,
        "e_gate": exp_w_gate[0].astype(BF16), "e_up": exp_w_up[0].astype(BF16),
        "e_down": exp_w_down[0].astype(BF16), "final_g": final_norm_g,
    }
    Bp, Tp, _ = x_prompt.shape
    Bs, Ts, _ = x_sample.shape

    mkf, mvf, mkb, mvb = _memory_kv(mem_prompt, mem_norm_g[0], w_mem_k[0].astype(BF16),
                                    w_mem_v[0].astype(BF16))
    zero_state = jnp.zeros((Bp, N_HEADS, HEAD_W, HEAD_W), F32)
    yp, kp, vp, sp = _layer(x_prompt, None, None, zero_state, mkb, mvb, p, PROMPT_CFG)
    ys, ks, vs, ss = _layer(
        x_sample, cache_diff_k[0], cache_diff_v[0], state_hgrn[0], cache_mem_k[0], cache_mem_v[0],
        p, SAMPLE_CFG)
    return (yp, ys, kp, vp, sp, mkf[None], mvf[None], ks, vs, ss)
```

```python
import functools
import math

import jax
import jax.numpy as jnp
from jax import lax
from jax.experimental import pallas as pl
from jax.experimental.pallas import tpu as pltpu

F32 = jnp.float32
BF16 = jnp.bfloat16

D_MODEL = 1024
EPS = 1e-5
CHUNK = 64
N_HEADS = 4
HEAD_W = 128
QK_DIM = 64
GROUP_W = N_HEADS * HEAD_W
N_PROJ = 7
DA_SCALE = QK_DIM ** -0.5
LOG2E = 1.4426950408889634
LAM_INIT = 0.8 - 0.6 * math.exp(-0.3 * 0)
N_MEM = 256
MEM_HEADS = 4
MEM_HEAD_DIM = D_MODEL // MEM_HEADS
MEM_SCALE = MEM_HEAD_DIM ** -0.5
N_GROUPS = 4
EXPERTS_PER_GROUP = 8
N_EXPERTS = N_GROUPS * EXPERTS_PER_GROUP
TOP_K = 2
EXPERT_FF = 512
ROUTE_W = 8
SUB = 16
SUB_KEYS = 256
N_SCORE_SLOTS = 3
N_SLAB = D_MODEL // HEAD_W
NEG_INF = float("-inf")

VMEM_LIMIT = 48 * 1024 * 1024


def _cparams(sem):
    return pltpu.CompilerParams(dimension_semantics=sem, vmem_limit_bytes=VMEM_LIMIT)


def _rms(x, g):
    return x * lax.rsqrt(jnp.mean(x * x, axis=-1, keepdims=True) + EPS) * g


def _dot(a, b):
    return jnp.dot(a, b, preferred_element_type=F32)


def _dot_nt(a, b):
    return lax.dot_general(a, b, (((1,), (1,)), ((), ())), preferred_element_type=F32)


def _dot_tn(a, b):
    return lax.dot_general(a, b, (((0,), (0,)), ((), ())), preferred_element_type=F32)


def _run_skewed(gens):
    waiting, live = list(gens), []
    while waiting or live:
        if waiting:
            live.append(waiting.pop(0))
        for gen in list(live):
            if next(gen, "done") == "done":
                live.remove(gen)


def _memkv_kernel(m_ref, g_ref, wk_ref, wv_ref, kf_ref, vf_ref, kb_ref, vb_ref):
    mn = _rms(m_ref[0], g_ref[...]).astype(BF16)
    k = _dot(mn, wk_ref[...])
    v = _dot(mn, wv_ref[...])
    kf_ref[0] = k.reshape(N_MEM, MEM_HEADS, MEM_HEAD_DIM)
    vf_ref[0] = v.reshape(N_MEM, MEM_HEADS, MEM_HEAD_DIM)
    kb_ref[0] = k.astype(BF16)
    vb_ref[0] = v.astype(BF16)


def _memory_kv(mem, g, wk, wv):
    B = mem.shape[0]
    blk = pl.BlockSpec((1, N_MEM, D_MODEL), lambda b: (b, 0, 0))
    hblk = pl.BlockSpec((1, N_MEM, MEM_HEADS, MEM_HEAD_DIM), lambda b: (b, 0, 0, 0))
    wspec = pl.BlockSpec((D_MODEL, D_MODEL), lambda b: (0, 0))
    return pl.pallas_call(
        _memkv_kernel,
        grid=(B,),
        in_specs=[blk, pl.BlockSpec((1, D_MODEL), lambda b: (0, 0)), wspec, wspec],
        out_specs=[hblk, hblk, blk, blk],
        out_shape=[jax.ShapeDtypeStruct((B, N_MEM, MEM_HEADS, MEM_HEAD_DIM), F32)] * 2
                  + [jax.ShapeDtypeStruct(mem.shape, BF16)] * 2,
        compiler_params=_cparams(("parallel",)),
        name="memory_kv",
    )(mem, g.reshape(1, D_MODEL), wk, wv)


def _in_proj_kernel(x_ref, g_ref, w_ref, lb_ref, q_ref, kf_ref, vf_ref, kb_ref, vt_ref,
                    qh_ref, lf_ref, kk_ref, hi_ref, sg_ref, *, nb, tm, split):
    gt = tm // split
    out_refs = (q_ref, kf_ref, vf_ref, kb_ref, vt_ref, qh_ref, lf_ref, kk_ref, hi_ref, sg_ref)
    _run_skewed([_in_proj_group(slice(g * gt, (g + 1) * gt), x_ref, g_ref, w_ref, lb_ref, out_refs,
                                nb=nb, gt=gt) for g in range(split)])


def _in_proj_group(ts, x_ref, g_ref, w_ref, lb_ref, out_refs, *, nb, gt):
    q_ref, kf_ref, vf_ref, kb_ref, vt_ref, qh_ref, lf_ref, kk_ref, hi_ref, sg_ref = out_refs
    rows = nb * gt
    h = _rms(x_ref[:, ts, :].reshape(rows, D_MODEL), g_ref[...]).astype(BF16)
    yield

    def proj(i):
        return _dot(h, w_ref[:, i * GROUP_W:(i + 1) * GROUP_W])

    def put(ref, val):
        ref[:, ts, :] = val.reshape(nb, gt, GROUP_W).astype(ref.dtype)

    def put_heads(ref, val):
        ref[:, ts] = val.reshape(nb, gt, N_HEADS, HEAD_W)

    put(q_ref, proj(0) * (DA_SCALE * LOG2E))
    yield
    dk = proj(1)
    put_heads(kf_ref, dk)
    put(kb_ref, dk)
    yield
    dv = proj(2)
    put_heads(vf_ref, dv)
    for b in range(nb):
        vt_ref[b, :, ts] = dv[b * gt:(b + 1) * gt].T.astype(BF16)
    yield
    hq = proj(3)
    put(qh_ref, hq * jax.nn.sigmoid(hq))
    yield
    hf = proj(4)
    lb = lb_ref[...]
    put(lf_ref, jnp.log(lb + (1.0 - lb) * jax.nn.sigmoid(hf)))
    put(kk_ref, (1.0 - lb) * jax.nn.sigmoid(-hf))
    yield
    put(hi_ref, proj(5))
    yield
    hg = proj(6)
    put(sg_ref, hg * jax.nn.sigmoid(hg))
    yield


def _in_proj(x, g, w_bf, lb, nb, tm, split):
    B, T, _ = x.shape
    tok = lambda w: pl.BlockSpec((nb, tm, w), lambda b, t: (b, t, 0))
    fix = lambda s: pl.BlockSpec(s, lambda b, t: (0, 0))
    f32o = jax.ShapeDtypeStruct((B, T, GROUP_W), F32)
    bfo = jax.ShapeDtypeStruct((B, T, GROUP_W), BF16)
    o_spec = tok(GROUP_W)
    ho = jax.ShapeDtypeStruct((B, T, N_HEADS, HEAD_W), F32)
    h_spec = pl.BlockSpec((nb, tm, N_HEADS, HEAD_W), lambda b, t: (b, t, 0, 0))
    vt_spec = pl.BlockSpec((nb, GROUP_W, tm), lambda b, t: (b, 0, t))
    return pl.pallas_call(
        functools.partial(_in_proj_kernel, nb=nb, tm=tm, split=split),
        grid=(B // nb, T // tm),
        in_specs=[tok(D_MODEL), fix((1, D_MODEL)), fix((D_MODEL, N_PROJ * GROUP_W)), fix((1, GROUP_W))],
        out_specs=[o_spec, h_spec, h_spec, o_spec, vt_spec] + [o_spec] * 5,
        out_shape=[bfo, ho, ho, bfo, jax.ShapeDtypeStruct((B, GROUP_W, T), BF16),
                   bfo, f32o, bfo, bfo, bfo],
        compiler_params=_cparams(("parallel", "parallel")),
        name="in_proj",
    )(x, g.reshape(1, D_MODEL), w_bf, lb.reshape(1, GROUP_W))


def _attn_kernel(slope_ref, lam_ref, q_ref, k_ref, vt_ref, g_ref, o_ref,
                 m_ref, l_ref, acc_ref, base_ref, t_ref, mb_ref, *, tq, tk):
    h = pl.program_id(1)
    qi = pl.program_id(2)
    slope2 = slope_ref[h] * LOG2E
    qt = q_ref[0].astype(F32).T.astype(BF16)
    zeros = jnp.zeros((QK_DIM, tq), BF16)
    qt_maps = (jnp.concatenate([qt[:QK_DIM], zeros], axis=0),
               jnp.concatenate([zeros, qt[QK_DIM:]], axis=0))
    m_ref[...] = jnp.full(m_ref.shape, NEG_INF, F32)
    l_ref[...] = jnp.zeros(l_ref.shape, F32)
    acc_ref[...] = jnp.zeros(acc_ref.shape, F32)
    q0 = qi * tq

    def lane_pad(x, lo, fill):
        if lo == 0:
            return x
        return jnp.concatenate([jnp.full((x.shape[0], lo), fill, x.dtype), x], axis=1)


    def score_phase(slot, n_keys, get_k, get_bias, shift, q_lo=lambda u: 0):
        sub = min(n_keys, SUB_KEYS)
        mx = [None, None]
        for u in range(n_keys // sub):
            lo = q_lo(u)
            bias, visible = get_bias(u)
            k = get_k(u)
            for c in range(2):
                t = _dot(k, qt_maps[c][:, lo:]) + bias
                if visible is not None:
                    w = visible.shape[1]
                    head = jnp.where(visible, t[:, :w], NEG_INF)
                    t = head if w == t.shape[1] else jnp.concatenate([head, t[:, w:]], axis=1)
                t_ref[slot, c, u * sub:(u + 1) * sub, lo:] = t
                mu = lane_pad(jnp.max(t, axis=0, keepdims=True), lo, NEG_INF)
                mx[c] = mu if mx[c] is None else jnp.maximum(mx[c], mu)
            yield
        for c in range(2):
            mb_ref[slot, c] = mx[c] + shift
        yield

    def prob_phase(slot, n_keys, get_vt, shift, q_lo=lambda u: 0, pv_group=1):
        sub = min(n_keys, SUB_KEYS)
        n_sub = n_keys // sub
        off, alpha = [], []
        for c in range(2):
            m_old = m_ref[c]
            m_new = jnp.maximum(m_old, mb_ref[slot, c])
            off.append(m_new - shift)
            alpha.append(jnp.exp2(m_old - m_new))
            m_ref[c] = m_new
        yield
        lsum = [None, None]
        pv = [None, None]
        grp = pv_group if n_sub % pv_group == 0 else 1
        for u in range(0, n_sub, grp):
            lo = q_lo(u)
            assert all(q_lo(u + i) == lo for i in range(grp))
            parts = [get_vt(u + i) for i in range(grp)]
            vt = parts[0] if grp == 1 else jnp.concatenate(parts, axis=1)
            for c in range(2):
                p = jnp.exp2(t_ref[slot, c, u * sub:(u + grp) * sub, lo:] - off[c][:, lo:])
                ls = lane_pad(jnp.sum(p, axis=0, keepdims=True), lo, 0.0)
                pu = lane_pad(_dot(vt, p.astype(BF16)), lo, 0.0)
                lsum[c] = ls if lsum[c] is None else lsum[c] + ls
                pv[c] = pu if pv[c] is None else pv[c] + pu
            yield
        for c in range(2):
            l_ref[c] = alpha[c] * l_ref[c] + lsum[c]
            acc_ref[c] = alpha[c] * acc_ref[c] + pv[c]
        yield

    def run(*gens):
        live = list(gens)
        while live:
            for gen in list(live):
                if next(gen, "done") == "done":
                    live.remove(gen)

    def update(n_keys, get_k, get_vt, get_bias, shift, q_lo=lambda u: 0, pv_group=1):
        run(score_phase(0, n_keys, get_k, get_bias, shift, q_lo))
        run(prob_phase(0, n_keys, get_vt, shift, q_lo, pv_group))

    def rel_bias(n_keys, key0):
        key = lax.broadcasted_iota(jnp.int32, (n_keys, tq), 0) + key0
        qry = lax.broadcasted_iota(jnp.int32, (n_keys, tq), 1)
        return (qry - key).astype(F32) * (-slope2)

    def block_shift(k0):
        return (q0 - k0).astype(F32) * (-slope2)

    n_q = k_ref.shape[1] // tq
    if n_q > 1 or tq > SUB_KEYS:
        @pl.when(qi == 0)
        def _():
            base_ref[...] = rel_bias(base_ref.shape[0], 0)

    kd = 0 if n_q == 1 else pl.multiple_of(qi * tq, tq)
    sub_d = min(tq, SUB_KEYS)
    key = lax.broadcasted_iota(jnp.int32, (sub_d, sub_d), 0)
    qry = lax.broadcasted_iota(jnp.int32, (sub_d, sub_d), 1)
    strip_bias = jnp.abs(qry - key).astype(F32) * (-slope2)
    strip_visible = (key // CHUNK) <= (qry // CHUNK)

    def diag_bias(u):
        lo = (u + 1) * sub_d
        if lo == tq:
            return strip_bias, strip_visible
        later = base_ref[u * sub_d:(u + 1) * sub_d, lo:]
        return jnp.concatenate([strip_bias, later], axis=1), strip_visible

    diag_k = lambda u: k_ref[0, pl.ds(kd + u * sub_d, sub_d), :]
    diag_vt = lambda u: vt_ref[0, :, pl.ds(kd + u * sub_d, sub_d)]
    diag_lo = lambda u: u * sub_d

    if n_q == 1:
        update(tq, diag_k, diag_vt, diag_bias, 0.0, q_lo=diag_lo)
    else:
        sub_k = min(tk, SUB_KEYS)
        n_prev = qi * (tq // tk)
        assert (tq // tk) % 2 == 0, "earlier blocks are consumed in pairs"

        def prev_scores(slot, j):
            k0 = pl.multiple_of(j * tk, tk)
            return score_phase(slot, tk, lambda u: k_ref[0, pl.ds(k0 + u * sub_k, sub_k), :],
                               lambda u: (base_ref[u * sub_k:(u + 1) * sub_k, :], None),
                               block_shift(k0))

        def prev_probs(slot, j):
            k0 = pl.multiple_of(j * tk, tk)
            return prob_phase(slot, tk, lambda u: vt_ref[0, :, pl.ds(k0 + u * sub_k, sub_k)],
                              block_shift(k0), pv_group=2)

        run(score_phase(0, tq, diag_k, diag_bias, 0.0, diag_lo))
        run(prob_phase(0, tq, diag_vt, 0.0, diag_lo), prev_scores(1, 0))

        def pair_body(i, carry):
            j = 2 * i
            run(prev_probs(1, j), prev_scores(2, j + 1))
            run(prev_probs(2, j + 1), prev_scores(1, jnp.minimum(j + 2, n_prev - 1)))
            return carry
        lax.fori_loop(0, n_prev // 2, pair_body, 0)

    lam = lam_ref[0]
    ot = acc_ref[0] / l_ref[0] - lam * (acc_ref[1] / l_ref[1])
    ot = ot * lax.rsqrt(jnp.mean(ot * ot, axis=0, keepdims=True) + EPS)
    o_ref[0] = (ot.T * g_ref[...] * (1.0 - LAM_INIT)).astype(o_ref.dtype)


def _diff_attn(q, k, vt, slopes, lam, subln_g, tq, tk):
    B, T, _ = q.shape
    smem = pl.BlockSpec(memory_space=pltpu.SMEM)
    qspec = pl.BlockSpec((1, tq, HEAD_W), lambda b, h, i: (b, i, h))
    return pl.pallas_call(
        functools.partial(_attn_kernel, tq=tq, tk=tk),
        grid=(B, N_HEADS, T // tq),
        in_specs=[smem, smem, qspec,
                  pl.BlockSpec((1, T, HEAD_W), lambda b, h, i: (b, 0, h)),
                  pl.BlockSpec((1, HEAD_W, T), lambda b, h, i: (b, h, 0)),
                  pl.BlockSpec((1, HEAD_W), lambda b, h, i: (0, 0))],
        out_specs=qspec,
        out_shape=jax.ShapeDtypeStruct((B, T, GROUP_W), BF16),
        scratch_shapes=[pltpu.VMEM((2, 1, tq), F32), pltpu.VMEM((2, 1, tq), F32),
                        pltpu.VMEM((2, HEAD_W, tq), F32), pltpu.VMEM((max(tk, tq), tq), F32),
                        pltpu.VMEM((N_SCORE_SLOTS, 2, max(tk, tq), tq), F32),
                        pltpu.VMEM((N_SCORE_SLOTS, 2, 1, tq), F32)],
        compiler_params=_cparams(("parallel", "parallel", "arbitrary")),
        name="diff_attn",
    )(slopes, lam, q, k, vt, subln_g.reshape(1, HEAD_W))


def _attn_step_kernel(slope_ref, lam_ref, q_ref, k_ref, vt_ref, pk_ref, pv_ref, g_ref, o_ref, *,
                      t_new, past_len):
    assert 2 * t_new <= HEAD_W and t_new % CHUNK == 0 and past_len % CHUNK == 0
    pk = pk_ref[0].reshape(past_len, GROUP_W)
    pv = pv_ref[0].reshape(past_len, GROUP_W)
    lam = lam_ref[0]
    qry2 = lax.broadcasted_iota(jnp.int32, (1, 2 * t_new), 1) % t_new
    key_p = lax.broadcasted_iota(jnp.int32, (past_len, 2 * t_new), 0)
    key_n = lax.broadcasted_iota(jnp.int32, (t_new, 2 * t_new), 0)
    dist_p = (qry2 + past_len - key_p).astype(F32)
    dist_n = jnp.abs(qry2 - key_n).astype(F32)
    visible = (key_n // CHUNK) <= (qry2 // CHUNK)
    zeros = jnp.zeros((QK_DIM, t_new), BF16)
    for h in range(N_HEADS):
        hs = slice(h * HEAD_W, (h + 1) * HEAD_W)
        slope2 = slope_ref[h] * LOG2E
        qt = q_ref[0, :, hs].astype(F32).T.astype(BF16)
        qt2 = jnp.concatenate([jnp.concatenate([qt[:QK_DIM], zeros], axis=0),
                               jnp.concatenate([zeros, qt[QK_DIM:]], axis=0)], axis=1)
        s_p = _dot(pk[:, hs].astype(BF16), qt2) - slope2 * dist_p
        s_n = jnp.where(visible, _dot(k_ref[0, :, hs], qt2) - slope2 * dist_n, NEG_INF)
        m = jnp.maximum(jnp.max(s_p, axis=0, keepdims=True), jnp.max(s_n, axis=0, keepdims=True))
        p_p = jnp.exp2(s_p - m)
        p_n = jnp.exp2(s_n - m)
        l = jnp.sum(p_p, axis=0, keepdims=True) + jnp.sum(p_n, axis=0, keepdims=True)
        acc = (_dot_tn(pv[:, hs].astype(BF16), p_p.astype(BF16))
               + _dot(vt_ref[0, hs, :], p_n.astype(BF16))) / l
        ot = acc[:, :t_new] - lam * acc[:, t_new:]
        ot = ot * lax.rsqrt(jnp.mean(ot * ot, axis=0, keepdims=True) + EPS)
        o_ref[0, :, hs] = (ot.T * g_ref[...] * (1.0 - LAM_INIT)).astype(o_ref.dtype)


def _diff_attn_step(q, k, vt, past_k, past_v, slopes, lam, subln_g):
    B, T, _ = q.shape
    past_len = past_k.shape[1]
    smem = pl.BlockSpec(memory_space=pltpu.SMEM)
    tok = pl.BlockSpec((1, T, GROUP_W), lambda b: (b, 0, 0))
    pspec = pl.BlockSpec((1, past_len, N_HEADS, HEAD_W), lambda b: (b, 0, 0, 0))
    return pl.pallas_call(
        functools.partial(_attn_step_kernel, t_new=T, past_len=past_len),
        grid=(B,),
        in_specs=[smem, smem, tok, tok, pl.BlockSpec((1, GROUP_W, T), lambda b: (b, 0, 0)),
                  pspec, pspec, pl.BlockSpec((1, HEAD_W), lambda b: (0, 0))],
        out_specs=tok,
        out_shape=jax.ShapeDtypeStruct((B, T, GROUP_W), BF16),
        compiler_params=_cparams(("parallel",)),
        name="diff_attn_step",
    )(slopes, lam, q, k, vt, past_k, past_v, subln_g.reshape(1, HEAD_W))


def _split3(x):
    hi = x.astype(BF16)
    r = x - hi.astype(F32)
    mid = r.astype(BF16)
    lo = (r - mid.astype(F32)).astype(BF16)
    return hi, mid, lo


def _hgrn_kernel(q_ref, k_ref, lf_ref, v_ref, sg_ref, gn_ref, s0_ref, o_ref, sfin_ref,
                 st_ref, b_ref, *, tb):
    t = pl.program_id(1)
    n_chunks = tb // CHUNK

    @pl.when(t == 0)
    def _():
        for h in range(N_HEADS):
            st_ref[h] = s0_ref[0, h].T

    row = lax.broadcasted_iota(jnp.int32, (tb, tb), 0)
    col = lax.broadcasted_iota(jnp.int32, (tb, tb), 1)
    tri = jnp.where((col <= row) & (col // CHUNK == row // CHUNK), 1.0, 0.0).astype(BF16)
    hi, mid, lo = _split3(lf_ref[0] * LOG2E)
    b_ref[...] = _dot(tri, hi) + _dot(tri, mid) + _dot(tri, lo)

    sub_row = lax.broadcasted_iota(jnp.int32, (8, HEAD_W), 0)
    zeros16 = jnp.zeros((SUB, HEAD_W), F32)

    def pad_rows(x, r0):
        parts = []
        if r0:
            parts.append(jnp.zeros((r0, HEAD_W), F32))
        parts.append(x)
        rest = CHUNK - r0 - x.shape[0]
        if rest:
            parts.append(jnp.zeros((rest, HEAD_W), F32))
        return jnp.concatenate(parts, axis=0) if len(parts) > 1 else x

    def chunk_body(c, carry):
        r0 = pl.multiple_of(c * CHUNK, CHUNK)
        def head(h):
            hs = slice(h * HEAD_W, (h + 1) * HEAD_W)
            q = q_ref[0, pl.ds(r0, CHUNK), hs].astype(F32)
            k = k_ref[0, pl.ds(r0, CHUNK), hs].astype(F32)
            v_bf = v_ref[0, pl.ds(r0, CHUNK), hs]
            v = v_bf.astype(F32)
            b = b_ref[pl.ds(r0, CHUNK), hs]
            b_last = b[CHUNK - 1:CHUNK, :]
            st = st_ref[h]

            o = _dot_nt((q * jnp.exp2(b)).astype(BF16), st.astype(BF16))

            q_segs, k_segs = [], []
            for i in range(1, CHUNK // SUB):
                lo_r, hi_r = i * SUB, (i + 1) * SUB
                ref_b = b[lo_r - 1:lo_r, :]
                q_segs.append(pad_rows(q[lo_r:hi_r] * jnp.exp2(b[lo_r:hi_r] - ref_b), lo_r))
                k_segs.append(pad_rows(k[:lo_r] * jnp.exp2(ref_b - b[:lo_r]), 0))
            a_off = _dot_nt(jnp.concatenate(q_segs, axis=1).astype(BF16),
                            jnp.concatenate(k_segs, axis=1).astype(BF16))
            o = o + _dot(a_off.astype(BF16), v_bf)
            yield

            diag = []
            for blk in range(CHUNK // SUB):
                base = blk * SUB
                qa, qb = q[base:base + 8], q[base + 8:base + SUB]
                ba, bb = b[base:base + 8], b[base + 8:base + SUB]
                oa = jnp.zeros((8, HEAD_W), F32)
                ob = jnp.zeros((8, HEAD_W), F32)
                for s in range(SUB):
                    ks = k[base + s:base + s + 1]
                    bs = b[base + s:base + s + 1]
                    vs = v[base + s:base + s + 1]
                    if s < 8:
                        e = jnp.exp2(ba - bs)
                        if s:
                            e = jnp.where(sub_row >= s, e, 0.0)
                        oa = oa + jnp.sum(qa * ks * e, axis=-1, keepdims=True) * vs
                        ob = ob + jnp.sum(qb * ks * jnp.exp2(bb - bs), axis=-1, keepdims=True) * vs
                    else:
                        e = jnp.exp2(bb - bs)
                        if s > 8:
                            e = jnp.where(sub_row >= s - 8, e, 0.0)
                        ob = ob + jnp.sum(qb * ks * e, axis=-1, keepdims=True) * vs
                diag += [oa, ob]
                yield
            o = o + jnp.concatenate(diag, axis=0)

            k_dec = (k * jnp.exp2(b_last - b)).astype(BF16)
            st_ref[h] = st * jnp.exp2(b_last) + _dot_tn(v_bf, k_dec)

            out = _rms(o, gn_ref[...]) * sg_ref[0, pl.ds(r0, CHUNK), hs]
            o_ref[0, pl.ds(r0, CHUNK), hs] = out.astype(o_ref.dtype)
            yield

        _run_skewed([head(h) for h in range(N_HEADS)])
        return carry

    lax.fori_loop(0, n_chunks, chunk_body, 0)

    @pl.when(t == pl.num_programs(1) - 1)
    def _():
        for h in range(N_HEADS):
            sfin_ref[0, h] = st_ref[h].T


def _hgrn(qh, kk, lf, hi, sg, gnorm_g, s0, tb):
    B, T, _ = qh.shape
    tspec = pl.BlockSpec((1, tb, GROUP_W), lambda b, t: (b, t, 0))
    sspec = pl.BlockSpec((1, N_HEADS, HEAD_W, HEAD_W), lambda b, t: (b, 0, 0, 0))
    return pl.pallas_call(
        functools.partial(_hgrn_kernel, tb=tb),
        grid=(B, T // tb),
        in_specs=[tspec] * 5 + [pl.BlockSpec((1, HEAD_W), lambda b, t: (0, 0)), sspec],
        out_specs=[tspec, sspec],
        out_shape=[jax.ShapeDtypeStruct((B, T, GROUP_W), BF16),
                   jax.ShapeDtypeStruct((B, N_HEADS, HEAD_W, HEAD_W), F32)],
        scratch_shapes=[pltpu.VMEM((N_HEADS, HEAD_W, HEAD_W), F32), pltpu.VMEM((tb, GROUP_W), F32)],
        compiler_params=_cparams(("parallel", "arbitrary")),
        name="hgrn",
    )(qh, kk, lf, hi, sg, gnorm_g.reshape(1, HEAD_W), s0)


def _post_kernel(x_ref, oda_ref, ohg_ref, mk_ref, mv_ref, wout_ref, lnm_ref, wq_ref, wo_ref,
                 lnf_ref, wr_ref, br_ref, x2_ref, h3_ref, route_ref, om_ref, *, nb, tm, split):
    gt = tm // split
    groups = [_post_group(slice(g * gt, (g + 1) * gt), g * nb * gt, x_ref, oda_ref, ohg_ref,
                          mk_ref, mv_ref, wout_ref, lnm_ref, wq_ref, wo_ref, lnf_ref, wr_ref,
                          br_ref, x2_ref, h3_ref, route_ref, om_ref, nb=nb, gt=gt)
              for g in range(split)]
    _run_skewed(groups)


def _post_group(ts, om0, x_ref, oda_ref, ohg_ref, mk_ref, mv_ref, wout_ref, lnm_ref, wq_ref, wo_ref,
                lnf_ref, wr_ref, br_ref, x2_ref, h3_ref, route_ref, om_ref, *, nb, gt):
    rows = nb * gt
    x = x_ref[:, ts, :].reshape(rows, D_MODEL)
    mixed = (_dot(oda_ref[:, ts, :].reshape(rows, GROUP_W), wout_ref[:GROUP_W, :])
             + _dot(ohg_ref[:, ts, :].reshape(rows, GROUP_W), wout_ref[GROUP_W:, :]))
    x1 = x + mixed
    yield

    hm = _rms(x1, lnm_ref[...]).astype(BF16)
    yield
    qm = (_dot(hm, wq_ref[...]) * MEM_SCALE).astype(BF16)
    yield
    for b in range(nb):
        if len(mk_ref.shape) == 4:
            mk_b = mk_ref[b].reshape(N_MEM, D_MODEL).astype(BF16)
            mv_b = mv_ref[b].reshape(N_MEM, D_MODEL).astype(BF16)
        for h in range(MEM_HEADS):
            hs = slice(h * MEM_HEAD_DIM, (h + 1) * MEM_HEAD_DIM)
            if len(mk_ref.shape) == 4:
                mk, mv = mk_b[:, hs], mv_b[:, hs]
            else:
                mk, mv = mk_ref[b, :, hs], mv_ref[b, :, hs]
            s = _dot_nt(qm[b * gt:(b + 1) * gt, hs], mk.astype(BF16))
            e = jnp.exp(s - jnp.max(s, axis=-1, keepdims=True))
            p = e / jnp.sum(e, axis=-1, keepdims=True)
            om_ref[om0 + b * gt:om0 + (b + 1) * gt, hs] = _dot(
                p.astype(BF16), mv.astype(BF16)).astype(BF16)
    yield
    x2 = x1 + _dot(om_ref[om0:om0 + rows, :], wo_ref[...])
    x2_ref[:, ts, :] = x2.reshape(nb, gt, D_MODEL)
    yield

    h3 = _rms(x2, lnf_ref[...])
    h3_ref[:, ts] = h3.reshape(nb, gt, N_SLAB, HEAD_W)
    yield

    r = _dot(h3.astype(BF16), wr_ref[...]) + br_ref[...]
    lane = lax.broadcasted_iota(jnp.int32, r.shape, 1).astype(F32)
    big = float(4 * HEAD_W)
    g_mask = lane < N_GROUPS
    gl = jnp.where(g_mask, r, NEG_INF)
    g_max = jnp.max(gl, axis=-1, keepdims=True)
    g_idx = jnp.min(jnp.where(gl == g_max, lane, big), axis=-1, keepdims=True)
    g_w = 1.0 / jnp.sum(jnp.where(g_mask, jnp.exp(r - g_max), 0.0), axis=-1, keepdims=True)
    e_lo = N_GROUPS + EXPERTS_PER_GROUP * g_idx
    el = jnp.where((lane >= e_lo) & (lane < e_lo + EXPERTS_PER_GROUP), r, NEG_INF)
    v1 = jnp.max(el, axis=-1, keepdims=True)
    i1 = jnp.min(jnp.where(el == v1, lane, big), axis=-1, keepdims=True)
    el2 = jnp.where(lane == i1, NEG_INF, el)
    v2 = jnp.max(el2, axis=-1, keepdims=True)
    i2 = jnp.min(jnp.where(el2 == v2, lane, big), axis=-1, keepdims=True)
    t = jnp.exp(v2 - v1)
    p1 = 1.0 / (1.0 + t)
    rec = jnp.where(lane == 0.0, i1 - N_GROUPS,
          jnp.where(lane == 1.0, i2 - N_GROUPS,
          jnp.where(lane == 2.0, p1 * g_w,
          jnp.where(lane == 3.0, t * p1 * g_w, 0.0))))
    route_ref[:, ts, :] = rec[:, :ROUTE_W].reshape(nb, gt, ROUTE_W)


def _post_mix(x, oda, ohg, mem_k, mem_v, w_out, ln_mem, w_q, w_o, ln_ffn, w_r, b_r, nb, tm, split):
    B, T, _ = x.shape
    tok = lambda w: pl.BlockSpec((nb, tm, w), lambda b, t: (b, t, 0))
    if mem_k.ndim == 4:
        memspec = pl.BlockSpec((nb, N_MEM, MEM_HEADS, MEM_HEAD_DIM), lambda b, t: (b, 0, 0, 0))
    else:
        memspec = pl.BlockSpec((nb, N_MEM, D_MODEL), lambda b, t: (b, 0, 0))
    fix = lambda s: pl.BlockSpec(s, lambda b, t: (0, 0))
    return pl.pallas_call(
        functools.partial(_post_kernel, nb=nb, tm=tm, split=split),
        grid=(B // nb, T // tm),
        in_specs=[tok(D_MODEL), tok(GROUP_W), tok(GROUP_W), memspec, memspec,
                  fix((D_MODEL, D_MODEL)), fix((1, D_MODEL)), fix((D_MODEL, D_MODEL)),
                  fix((D_MODEL, D_MODEL)), fix((1, D_MODEL)), fix((D_MODEL, HEAD_W)),
                  fix((1, HEAD_W))],
        out_specs=[tok(D_MODEL),
                   pl.BlockSpec((nb, tm, N_SLAB, HEAD_W), lambda b, t: (b, t, 0, 0)),
                   tok(ROUTE_W)],
        out_shape=[jax.ShapeDtypeStruct((B, T, D_MODEL), F32),
                   jax.ShapeDtypeStruct((B, T, N_SLAB, HEAD_W), F32),
                   jax.ShapeDtypeStruct((B, T, ROUTE_W), F32)],
        scratch_shapes=[pltpu.VMEM((nb * tm, D_MODEL), BF16)],
        compiler_params=_cparams(("parallel", "parallel")),
        name="post_mix",
    )(x, oda, ohg, mem_k, mem_v, w_out, ln_mem.reshape(1, D_MODEL), w_q, w_o,
      ln_ffn.reshape(1, D_MODEL), w_r, b_r)


def _experts_kernel(blk_e_ref, n_used_ref, src_next_ref, dst_prev_ref, src0_ref, dst_ref, h_hbm,
                    wg_ref, wu_ref, wd_ref, y_hbm, xbuf, ybuf, gsem, ssem, *, bm, m_tot):
    del blk_e_ref
    i = pl.program_id(0)
    n_used = n_used_ref[0]
    slot = i % 2

    def gather_row(tab_ref, r, s, priority=0):
        pltpu.make_async_copy(h_hbm.at[tab_ref[0, 0, r]], xbuf.at[s, r],
                              gsem.at[s]).start(priority=priority)

    def scatter_row(tab_ref, r, s, priority=0):
        pltpu.make_async_copy(ybuf.at[s, r], y_hbm.at[tab_ref[0, 0, r]],
                              ssem.at[s]).start(priority=priority)

    def wait_block(buf, sem, s):
        pltpu.make_async_copy(buf.at[s], buf.at[s], sem.at[s]).wait()

    def spare_fill(s):
        return pltpu.make_async_copy(ybuf.at[s], y_hbm.at[pl.ds(m_tot + s * bm, bm)], ssem.at[s])

    @pl.when(i == 0)
    def _():
        ybuf[...] = jnp.zeros(ybuf.shape, F32)
        spare_fill(0).start()
        spare_fill(1).start()
        spare_fill(1).wait()

        def body(r, c):
            gather_row(src0_ref, r, 0)
            return c
        lax.fori_loop(0, bm, body, 0)

    @pl.when(i < n_used)
    def _():
        wait_block(xbuf, gsem, slot)
        wait_block(ybuf, ssem, slot)
        x = xbuf[slot].reshape(bm, D_MODEL).astype(BF16)
        hg = _dot(x, wg_ref[0].astype(BF16))
        hu = _dot(x, wu_ref[0].astype(BF16))
        hb = (hg * jax.nn.sigmoid(hg) * hu).astype(BF16)
        ybuf[slot] = _dot(hb, wd_ref[0].astype(BF16)).reshape(bm, N_SLAB, HEAD_W)
        for r in range(bm):
            gather_row(src_next_ref, r, 1 - slot, priority=r % 2)
            scatter_row(dst_prev_ref, r, 1 - slot, priority=(r + 1) % 2)

    @pl.when(i == n_used - 1)
    def _():
        def body(r, c):
            scatter_row(dst_ref, r, slot)
            return c
        lax.fori_loop(0, bm, body, 0)
        wait_block(xbuf, gsem, 1 - slot)
        wait_block(ybuf, ssem, 1 - slot)
        wait_block(ybuf, ssem, slot)


def _experts(h3, blk_e, n_used, src_tab, dst_tab, dst_prev_tab, wg, wu, wd, bm):
    n_blocks = src_tab.shape[0]
    m_tot = TOP_K * h3.shape[0]
    tab = lambda f: pl.BlockSpec((1, 1, bm), f, memory_space=pltpu.SMEM)
    cur = lambda i, e, n: (i, 0, 0)
    nxt = lambda i, e, n: (jnp.minimum(i + 1, n_blocks - 1), 0, 0)
    first = lambda i, e, n: (0, 0, 0)
    wspec = lambda a, b: pl.BlockSpec((1, a, b), lambda i, e, n: (e[i], 0, 0))
    grid_spec = pltpu.PrefetchScalarGridSpec(
        num_scalar_prefetch=2,
        grid=(n_blocks,),
        in_specs=[tab(nxt), tab(cur), tab(first), tab(cur), pl.BlockSpec(memory_space=pl.ANY),
                  wspec(D_MODEL, EXPERT_FF), wspec(D_MODEL, EXPERT_FF), wspec(EXPERT_FF, D_MODEL)],
        out_specs=pl.BlockSpec(memory_space=pl.ANY),
        scratch_shapes=[pltpu.VMEM((2, bm, N_SLAB, HEAD_W), F32), pltpu.VMEM((2, bm, N_SLAB, HEAD_W), F32),
                        pltpu.SemaphoreType.DMA((2,)), pltpu.SemaphoreType.DMA((2,))],
    )
    return pl.pallas_call(
        functools.partial(_experts_kernel, bm=bm, m_tot=m_tot),
        grid_spec=grid_spec,
        out_shape=jax.ShapeDtypeStruct((m_tot + 2 * bm, N_SLAB, HEAD_W), F32),
        compiler_params=_cparams(("arbitrary",)),
        name="experts",
    )(blk_e, n_used, src_tab, dst_prev_tab, src_tab, dst_tab, h3, wg, wu, wd)


def _combine_kernel(x_ref, y0_ref, y1_ref, route_ref, g_ref, o_ref):
    rt = route_ref[...]
    tm = x_ref.shape[0]
    y0 = y0_ref[...].reshape(tm, D_MODEL)
    y1 = y1_ref[...].reshape(tm, D_MODEL)
    y = x_ref[...] + rt[:, 2:3] * y0 + rt[:, 3:4] * y1
    o_ref[...] = _rms(y, g_ref[...])


def _combine(x2, y, route, final_g, tm):
    n = x2.shape[0]
    nt = n // tm
    return pl.pallas_call(
        _combine_kernel,
        grid=(nt,),
        in_specs=[pl.BlockSpec((tm, D_MODEL), lambda i: (i, 0)),
                  pl.BlockSpec((tm, N_SLAB, HEAD_W), lambda i: (i, 0, 0)),
                  pl.BlockSpec((tm, N_SLAB, HEAD_W), lambda i: (i + nt, 0, 0)),
                  pl.BlockSpec((tm, ROUTE_W), lambda i: (i, 0)),
                  pl.BlockSpec((1, D_MODEL), lambda i: (0, 0))],
        out_specs=pl.BlockSpec((tm, D_MODEL), lambda i: (i, 0)),
        out_shape=jax.ShapeDtypeStruct((n, D_MODEL), F32),
        compiler_params=_cparams(("parallel",)),
        name="combine",
    )(x2, y, y, route, final_g.reshape(1, D_MODEL))


def _routing_tables(route, bm):
    n = route.shape[0]
    m_tot = TOP_K * n
    flat_e = jnp.concatenate([route[:, 0], route[:, 1]]).astype(jnp.int32)
    experts = jnp.arange(N_EXPERTS, dtype=jnp.int32)
    counts = jnp.sum(flat_e[:, None] == experts[None, :], axis=0, dtype=jnp.int32)
    n_blk_e = (counts + bm - 1) // bm
    blk_end = jnp.cumsum(n_blk_e)
    n_blocks = m_tot // bm + N_EXPERTS
    blk = jnp.arange(n_blocks, dtype=jnp.int32)
    blk_e = jnp.minimum(jnp.sum(blk[:, None] >= blk_end[None, :], axis=1, dtype=jnp.int32),
                        N_EXPERTS - 1)
    r = jnp.arange(bm, dtype=jnp.int32)[None, :]
    n_pad = (n_blk_e * bm - counts)[:, None]
    pad_keys = jnp.where(r < n_pad, 2 * experts[:, None] + 1, 2 * N_EXPERTS + 1)
    keys = jnp.concatenate([2 * flat_e, pad_keys.reshape(-1)])
    vals = jnp.concatenate([jnp.arange(m_tot, dtype=jnp.int32),
                            jnp.full((N_EXPERTS * bm,), -1, jnp.int32)])
    m = lax.sort((keys, vals), num_keys=1)[1].reshape(n_blocks, bm)
    real = m >= 0
    spare = m_tot + (blk[:, None] % 2) * bm + r
    src = jnp.where(real, m % n, 0)
    dst = jnp.where(real, m, spare)
    dst_prev = jnp.concatenate([m_tot + bm + r, dst[:-1]], axis=0)
    shape = (n_blocks, 1, bm)
    return (blk_e, blk_end[N_EXPERTS - 1:].astype(jnp.int32), src.reshape(shape), dst.reshape(shape),
            dst_prev.reshape(shape))


def _layer(x, past_k, past_v, s0, mem_k, mem_v, p, cfg):
    B, T, _ = x.shape
    n = B * T
    (q, kf, vf, kb, vt, qh, lf, kk, hi, sg) = _in_proj(
        x, p["ln_mix"], p["w_in"], p["lb"], cfg["nb"], cfg["tm_proj"], cfg["split_proj"])
    if past_k is None:
        oda = _diff_attn(q, kb, vt, p["slopes"], p["lam"], p["da_subln"], cfg["tq"], cfg["tk"])
    else:
        oda = _diff_attn_step(q, kb, vt, past_k, past_v, p["slopes"], p["lam"], p["da_subln"])
    ohg, s_new = _hgrn(qh, kk, lf, hi, sg, p["hg_gnorm"], s0, cfg["tb"])
    x2, h3, route = _post_mix(x, oda, ohg, mem_k, mem_v, p["w_out"], p["ln_mem"], p["w_mem_q"],
                              p["w_mem_o"], p["ln_ffn"], p["w_r"], p["b_r"], cfg["nb"], cfg["tm_post"],
                              cfg["split"])
    route2 = route.reshape(n, ROUTE_W)
    blk_e, n_used, src_tab, dst_tab, dst_prev_tab = _routing_tables(route2, cfg["bm"])
    y = _experts(h3.reshape(n, N_SLAB, HEAD_W), blk_e, n_used, src_tab, dst_tab, dst_prev_tab,
                 p["e_gate"], p["e_up"], p["e_down"], cfg["bm"])
    out = _combine(x2.reshape(n, D_MODEL), y, route2, p["final_g"], cfg["tm_in"])
    return out.reshape(B, T, D_MODEL), kf[None], vf[None], s_new[None]


PROMPT_CFG = dict(tm_in=256, tq=512, tk=256, tb=256, nb=1, tm_proj=512, split_proj=2, tm_post=512, split=2,
                  bm=512)
SAMPLE_CFG = dict(tm_in=256, tq=64, tk=64, tb=64, nb=4, tm_proj=64, split_proj=1, tm_post=64, split=1,
                  bm=128)


def kernel(x_prompt, x_sample, mem_prompt, cache_diff_k, cache_diff_v, state_hgrn, cache_mem_k, cache_mem_v, ln_mix_g, w_in, da_lambda, da_subln_g, hg_lb_logits, hg_gnorm_g, w_out, ln_mem_g, mem_norm_g, w_mem_q, w_mem_k, w_mem_v, w_mem_o, ln_ffn_g, router_group_w, router_group_b, router_expert_w, router_expert_b, exp_w_gate, exp_w_up, exp_w_down, final_norm_g):
    assert w_in.shape[0] == 1, "single-layer configuration"
    lb_all = jnp.cumsum(jax.nn.softmax(hg_lb_logits.astype(F32), axis=0), axis=0)
    lp = da_lambda[0].astype(F32)
    lam = jnp.exp(jnp.sum(lp[0] * lp[1])) - jnp.exp(jnp.sum(lp[2] * lp[3])) + LAM_INIT
    w_r = jnp.zeros((D_MODEL, HEAD_W), F32)
    w_r = w_r.at[:, :N_GROUPS].set(router_group_w[0]).at[:, N_GROUPS:N_GROUPS + N_EXPERTS].set(router_expert_w[0])
    b_r = jnp.zeros((1, HEAD_W), F32)
    b_r = b_r.at[0, :N_GROUPS].set(router_group_b[0]).at[0, N_GROUPS:N_GROUPS + N_EXPERTS].set(router_expert_b[0])
    p = {
        "ln_mix": ln_mix_g[0], "w_in": w_in[0].astype(BF16), "lb": lb_all[0],
        "slopes": jnp.exp2(-8.0 * jnp.arange(1, N_HEADS + 1, dtype=F32) / N_HEADS),
        "lam": lam.reshape(1), "da_subln": da_subln_g[0], "hg_gnorm": hg_gnorm_g[0],
        "w_out": w_out[0].astype(BF16), "ln_mem": ln_mem_g[0], "w_mem_q": w_mem_q[0].astype(BF16),
        "w_mem_o": w_mem_o[0].astype(BF16), "ln_ffn": ln_ffn_g[0], "w_r": w_r.astype(BF16), "b_r": b_r,
        "e_gate": exp_w_gate[0], "e_up": exp_w_up[0],
        "e_down": exp_w_down[0], "final_g": final_norm_g,
    }
    Bp, Tp, _ = x_prompt.shape
    Bs, Ts, _ = x_sample.shape

    mkf, mvf, mkb, mvb = _memory_kv(mem_prompt, mem_norm_g[0], w_mem_k[0].astype(BF16),
                                    w_mem_v[0].astype(BF16))
    zero_state = jnp.zeros((Bp, N_HEADS, HEAD_W, HEAD_W), F32)
    yp, kp, vp, sp = _layer(x_prompt, None, None, zero_state, mkb, mvb, p, PROMPT_CFG)
    ys, ks, vs, ss = _layer(
        x_sample, cache_diff_k[0], cache_diff_v[0], state_hgrn[0], cache_mem_k[0], cache_mem_v[0],
        p, SAMPLE_CFG)
    return (yp, ys, kp, vp, sp, mkf[None], mvf[None], ks, vs, ss)
```

```python
import functools
import math

import jax
import jax.numpy as jnp
from jax import lax
from jax.experimental import pallas as pl
from jax.experimental.pallas import tpu as pltpu

F32 = jnp.float32
BF16 = jnp.bfloat16

D_MODEL = 1024
EPS = 1e-5
CHUNK = 64
N_HEADS = 4
HEAD_W = 128
QK_DIM = 64
GROUP_W = N_HEADS * HEAD_W
N_PROJ = 7
DA_SCALE = QK_DIM ** -0.5
LOG2E = 1.4426950408889634
LAM_INIT = 0.8 - 0.6 * math.exp(-0.3 * 0)
N_MEM = 256
MEM_HEADS = 4
MEM_HEAD_DIM = D_MODEL // MEM_HEADS
MEM_SCALE = MEM_HEAD_DIM ** -0.5
N_GROUPS = 4
EXPERTS_PER_GROUP = 8
N_EXPERTS = N_GROUPS * EXPERTS_PER_GROUP
TOP_K = 2
EXPERT_FF = 512
ROUTE_W = 8
SUB = 16
SUB_KEYS = 256
N_SCORE_SLOTS = 3
N_SLAB = D_MODEL // HEAD_W
NEG_INF = float("-inf")

VMEM_LIMIT = 48 * 1024 * 1024


def _cparams(sem):
    return pltpu.CompilerParams(dimension_semantics=sem, vmem_limit_bytes=VMEM_LIMIT)


def _rms(x, g):
    return x * lax.rsqrt(jnp.mean(x * x, axis=-1, keepdims=True) + EPS) * g


def _dot(a, b):
    return jnp.dot(a, b, preferred_element_type=F32)


def _dot_nt(a, b):
    return lax.dot_general(a, b, (((1,), (1,)), ((), ())), preferred_element_type=F32)


def _dot_tn(a, b):
    return lax.dot_general(a, b, (((0,), (0,)), ((), ())), preferred_element_type=F32)


def _run_skewed(gens):
    waiting, live = list(gens), []
    while waiting or live:
        if waiting:
            live.append(waiting.pop(0))
        for gen in list(live):
            if next(gen, "done") == "done":
                live.remove(gen)


def _memkv_kernel(m_ref, g_ref, wk_ref, wv_ref, kf_ref, vf_ref, kb_ref, vb_ref):
    mn = _rms(m_ref[0], g_ref[...]).astype(BF16)
    k = _dot(mn, wk_ref[...])
    v = _dot(mn, wv_ref[...])
    kf_ref[0] = k.reshape(N_MEM, MEM_HEADS, MEM_HEAD_DIM)
    vf_ref[0] = v.reshape(N_MEM, MEM_HEADS, MEM_HEAD_DIM)
    kb_ref[0] = k.astype(BF16)
    vb_ref[0] = v.astype(BF16)


def _memory_kv(mem, g, wk, wv):
    B = mem.shape[0]
    blk = pl.BlockSpec((1, N_MEM, D_MODEL), lambda b: (b, 0, 0))
    hblk = pl.BlockSpec((1, N_MEM, MEM_HEADS, MEM_HEAD_DIM), lambda b: (b, 0, 0, 0))
    wspec = pl.BlockSpec((D_MODEL, D_MODEL), lambda b: (0, 0))
    return pl.pallas_call(
        _memkv_kernel,
        grid=(B,),
        in_specs=[blk, pl.BlockSpec((1, D_MODEL), lambda b: (0, 0)), wspec, wspec],
        out_specs=[hblk, hblk, blk, blk],
        out_shape=[jax.ShapeDtypeStruct((B, N_MEM, MEM_HEADS, MEM_HEAD_DIM), F32)] * 2
                  + [jax.ShapeDtypeStruct(mem.shape, BF16)] * 2,
        compiler_params=_cparams(("parallel",)),
        name="memory_kv",
    )(mem, g.reshape(1, D_MODEL), wk, wv)


def _in_proj_kernel(x_ref, g_ref, w_ref, lb_ref, q_ref, kf_ref, vf_ref, kb_ref, vt_ref,
                    qh_ref, lf_ref, kk_ref, hi_ref, sg_ref, *, nb, tm, split):
    gt = tm // split
    out_refs = (q_ref, kf_ref, vf_ref, kb_ref, vt_ref, qh_ref, lf_ref, kk_ref, hi_ref, sg_ref)
    _run_skewed([_in_proj_group(slice(g * gt, (g + 1) * gt), x_ref, g_ref, w_ref, lb_ref, out_refs,
                                nb=nb, gt=gt) for g in range(split)])


def _in_proj_group(ts, x_ref, g_ref, w_ref, lb_ref, out_refs, *, nb, gt):
    q_ref, kf_ref, vf_ref, kb_ref, vt_ref, qh_ref, lf_ref, kk_ref, hi_ref, sg_ref = out_refs
    rows = nb * gt
    h = _rms(x_ref[:, ts, :].reshape(rows, D_MODEL), g_ref[...]).astype(BF16)
    yield

    def proj(i):
        return _dot(h, w_ref[:, i * GROUP_W:(i + 1) * GROUP_W])

    def put(ref, val):
        ref[:, ts, :] = val.reshape(nb, gt, GROUP_W).astype(ref.dtype)

    def put_heads(ref, val):
        ref[:, ts] = val.reshape(nb, gt, N_HEADS, HEAD_W)

    put(q_ref, proj(0) * (DA_SCALE * LOG2E))
    yield
    dk = proj(1)
    put_heads(kf_ref, dk)
    put(kb_ref, dk)
    yield
    dv = proj(2)
    put_heads(vf_ref, dv)
    for b in range(nb):
        vt_ref[b, :, ts] = dv[b * gt:(b + 1) * gt].T.astype(BF16)
    yield
    hq = proj(3)
    put(qh_ref, hq * jax.nn.sigmoid(hq))
    yield
    hf = proj(4)
    lb = lb_ref[...]
    put(lf_ref, jnp.log(lb + (1.0 - lb) * jax.nn.sigmoid(hf)))
    put(kk_ref, (1.0 - lb) * jax.nn.sigmoid(-hf))
    yield
    put(hi_ref, proj(5))
    yield
    hg = proj(6)
    put(sg_ref, hg * jax.nn.sigmoid(hg))
    yield


def _in_proj(x, g, w_bf, lb, nb, tm, split):
    B, T, _ = x.shape
    tok = lambda w: pl.BlockSpec((nb, tm, w), lambda b, t: (b, t, 0))
    fix = lambda s: pl.BlockSpec(s, lambda b, t: (0, 0))
    f32o = jax.ShapeDtypeStruct((B, T, GROUP_W), F32)
    bfo = jax.ShapeDtypeStruct((B, T, GROUP_W), BF16)
    o_spec = tok(GROUP_W)
    ho = jax.ShapeDtypeStruct((B, T, N_HEADS, HEAD_W), F32)
    h_spec = pl.BlockSpec((nb, tm, N_HEADS, HEAD_W), lambda b, t: (b, t, 0, 0))
    vt_spec = pl.BlockSpec((nb, GROUP_W, tm), lambda b, t: (b, 0, t))
    return pl.pallas_call(
        functools.partial(_in_proj_kernel, nb=nb, tm=tm, split=split),
        grid=(B // nb, T // tm),
        in_specs=[tok(D_MODEL), fix((1, D_MODEL)), fix((D_MODEL, N_PROJ * GROUP_W)), fix((1, GROUP_W))],
        out_specs=[o_spec, h_spec, h_spec, o_spec, vt_spec] + [o_spec] * 5,
        out_shape=[bfo, ho, ho, bfo, jax.ShapeDtypeStruct((B, GROUP_W, T), BF16),
                   bfo, f32o, bfo, bfo, bfo],
        compiler_params=_cparams(("parallel", "parallel")),
        name="in_proj",
    )(x, g.reshape(1, D_MODEL), w_bf, lb.reshape(1, GROUP_W))


def _attn_kernel(slope_ref, lam_ref, q_ref, k_ref, vt_ref, g_ref, o_ref,
                 m_ref, l_ref, acc_ref, base_ref, t_ref, mb_ref, *, tq, tk):
    h = pl.program_id(1)
    qi = pl.program_id(2)
    slope2 = slope_ref[h] * LOG2E
    qt = q_ref[0].astype(F32).T.astype(BF16)
    zeros = jnp.zeros((QK_DIM, tq), BF16)
    qt_maps = (jnp.concatenate([qt[:QK_DIM], zeros], axis=0),
               jnp.concatenate([zeros, qt[QK_DIM:]], axis=0))
    m_ref[...] = jnp.full(m_ref.shape, NEG_INF, F32)
    l_ref[...] = jnp.zeros(l_ref.shape, F32)
    acc_ref[...] = jnp.zeros(acc_ref.shape, F32)
    q0 = qi * tq

    def lane_pad(x, lo, fill):
        if lo == 0:
            return x
        return jnp.concatenate([jnp.full((x.shape[0], lo), fill, x.dtype), x], axis=1)


    def score_phase(slot, n_keys, get_k, get_bias, shift, q_lo=lambda u: 0):
        sub = min(n_keys, SUB_KEYS)
        mx = [None, None]
        for u in range(n_keys // sub):
            lo = q_lo(u)
            bias, visible = get_bias(u)
            k = get_k(u)
            for c in range(2):
                t = _dot(k, qt_maps[c][:, lo:]) + bias
                if visible is not None:
                    w = visible.shape[1]
                    head = jnp.where(visible, t[:, :w], NEG_INF)
                    t = head if w == t.shape[1] else jnp.concatenate([head, t[:, w:]], axis=1)
                t_ref[slot, c, u * sub:(u + 1) * sub, lo:] = t
                mu = lane_pad(jnp.max(t, axis=0, keepdims=True), lo, NEG_INF)
                mx[c] = mu if mx[c] is None else jnp.maximum(mx[c], mu)
            yield
        for c in range(2):
            mb_ref[slot, c] = mx[c] + shift
        yield

    def prob_phase(slot, n_keys, get_vt, shift, q_lo=lambda u: 0, pv_group=1):
        sub = min(n_keys, SUB_KEYS)
        n_sub = n_keys // sub
        off, alpha = [], []
        for c in range(2):
            m_old = m_ref[c]
            m_new = jnp.maximum(m_old, mb_ref[slot, c])
            off.append(m_new - shift)
            alpha.append(jnp.exp2(m_old - m_new))
            m_ref[c] = m_new
        yield
        lsum = [None, None]
        pv = [None, None]
        grp = pv_group if n_sub % pv_group == 0 else 1
        for u in range(0, n_sub, grp):
            lo = q_lo(u)
            assert all(q_lo(u + i) == lo for i in range(grp))
            parts = [get_vt(u + i) for i in range(grp)]
            vt = parts[0] if grp == 1 else jnp.concatenate(parts, axis=1)
            for c in range(2):
                p = jnp.exp2(t_ref[slot, c, u * sub:(u + grp) * sub, lo:] - off[c][:, lo:])
                ls = lane_pad(jnp.sum(p, axis=0, keepdims=True), lo, 0.0)
                pu = lane_pad(_dot(vt, p.astype(BF16)), lo, 0.0)
                lsum[c] = ls if lsum[c] is None else lsum[c] + ls
                pv[c] = pu if pv[c] is None else pv[c] + pu
            yield
        for c in range(2):
            l_ref[c] = alpha[c] * l_ref[c] + lsum[c]
            acc_ref[c] = alpha[c] * acc_ref[c] + pv[c]
        yield

    def run(*gens):
        live = list(gens)
        while live:
            for gen in list(live):
                if next(gen, "done") == "done":
                    live.remove(gen)

    def update(n_keys, get_k, get_vt, get_bias, shift, q_lo=lambda u: 0, pv_group=1):
        run(score_phase(0, n_keys, get_k, get_bias, shift, q_lo))
        run(prob_phase(0, n_keys, get_vt, shift, q_lo, pv_group))

    def rel_bias(n_keys, key0):
        key = lax.broadcasted_iota(jnp.int32, (n_keys, tq), 0) + key0
        qry = lax.broadcasted_iota(jnp.int32, (n_keys, tq), 1)
        return (qry - key).astype(F32) * (-slope2)

    def block_shift(k0):
        return (q0 - k0).astype(F32) * (-slope2)

    n_q = k_ref.shape[1] // tq
    if n_q > 1 or tq > SUB_KEYS:
        @pl.when(qi == 0)
        def _():
            base_ref[...] = rel_bias(base_ref.shape[0], 0)

    kd = 0 if n_q == 1 else pl.multiple_of(qi * tq, tq)
    sub_d = min(tq, SUB_KEYS)
    key = lax.broadcasted_iota(jnp.int32, (sub_d, sub_d), 0)
    qry = lax.broadcasted_iota(jnp.int32, (sub_d, sub_d), 1)
    strip_bias = jnp.abs(qry - key).astype(F32) * (-slope2)
    strip_visible = (key // CHUNK) <= (qry // CHUNK)

    def diag_bias(u):
        lo = (u + 1) * sub_d
        if lo == tq:
            return strip_bias, strip_visible
        later = base_ref[u * sub_d:(u + 1) * sub_d, lo:]
        return jnp.concatenate([strip_bias, later], axis=1), strip_visible

    diag_k = lambda u: k_ref[0, pl.ds(kd + u * sub_d, sub_d), :]
    diag_vt = lambda u: vt_ref[0, :, pl.ds(kd + u * sub_d, sub_d)]
    diag_lo = lambda u: u * sub_d

    if n_q == 1:
        update(tq, diag_k, diag_vt, diag_bias, 0.0, q_lo=diag_lo)
    else:
        sub_k = min(tk, SUB_KEYS)
        n_prev = qi * (tq // tk)
        assert (tq // tk) % 2 == 0, "earlier blocks are consumed in pairs"

        def prev_scores(slot, j):
            k0 = pl.multiple_of(j * tk, tk)
            return score_phase(slot, tk, lambda u: k_ref[0, pl.ds(k0 + u * sub_k, sub_k), :],
                               lambda u: (base_ref[u * sub_k:(u + 1) * sub_k, :], None),
                               block_shift(k0))

        def prev_probs(slot, j):
            k0 = pl.multiple_of(j * tk, tk)
            return prob_phase(slot, tk, lambda u: vt_ref[0, :, pl.ds(k0 + u * sub_k, sub_k)],
                              block_shift(k0), pv_group=2)

        run(score_phase(0, tq, diag_k, diag_bias, 0.0, diag_lo))
        run(prob_phase(0, tq, diag_vt, 0.0, diag_lo), prev_scores(1, 0))

        def pair_body(i, carry):
            j = 2 * i
            run(prev_probs(1, j), prev_scores(2, j + 1))
            run(prev_probs(2, j + 1), prev_scores(1, jnp.minimum(j + 2, n_prev - 1)))
            return carry
        lax.fori_loop(0, n_prev // 2, pair_body, 0)

    lam = lam_ref[0]
    ot = acc_ref[0] / l_ref[0] - lam * (acc_ref[1] / l_ref[1])
    ot = ot * lax.rsqrt(jnp.mean(ot * ot, axis=0, keepdims=True) + EPS)
    o_ref[0] = (ot.T * g_ref[...] * (1.0 - LAM_INIT)).astype(o_ref.dtype)


def _diff_attn(q, k, vt, slopes, lam, subln_g, tq, tk):
    B, T, _ = q.shape
    smem = pl.BlockSpec(memory_space=pltpu.SMEM)
    qspec = pl.BlockSpec((1, tq, HEAD_W), lambda b, h, i: (b, i, h))
    return pl.pallas_call(
        functools.partial(_attn_kernel, tq=tq, tk=tk),
        grid=(B, N_HEADS, T // tq),
        in_specs=[smem, smem, qspec,
                  pl.BlockSpec((1, T, HEAD_W), lambda b, h, i: (b, 0, h)),
                  pl.BlockSpec((1, HEAD_W, T), lambda b, h, i: (b, h, 0)),
                  pl.BlockSpec((1, HEAD_W), lambda b, h, i: (0, 0))],
        out_specs=qspec,
        out_shape=jax.ShapeDtypeStruct((B, T, GROUP_W), BF16),
        scratch_shapes=[pltpu.VMEM((2, 1, tq), F32), pltpu.VMEM((2, 1, tq), F32),
                        pltpu.VMEM((2, HEAD_W, tq), F32), pltpu.VMEM((max(tk, tq), tq), F32),
                        pltpu.VMEM((N_SCORE_SLOTS, 2, max(tk, tq), tq), F32),
                        pltpu.VMEM((N_SCORE_SLOTS, 2, 1, tq), F32)],
        compiler_params=_cparams(("parallel", "parallel", "arbitrary")),
        name="diff_attn",
    )(slopes, lam, q, k, vt, subln_g.reshape(1, HEAD_W))


def _attn_step_kernel(slope_ref, lam_ref, q_ref, k_ref, vt_ref, pk_ref, pv_ref, g_ref, o_ref, *,
                      t_new, past_len):
    assert 2 * t_new <= HEAD_W and t_new % CHUNK == 0 and past_len % CHUNK == 0
    pk = pk_ref[0].reshape(past_len, GROUP_W)
    pv = pv_ref[0].reshape(past_len, GROUP_W)
    lam = lam_ref[0]
    qry2 = lax.broadcasted_iota(jnp.int32, (1, 2 * t_new), 1) % t_new
    key_p = lax.broadcasted_iota(jnp.int32, (past_len, 2 * t_new), 0)
    key_n = lax.broadcasted_iota(jnp.int32, (t_new, 2 * t_new), 0)
    dist_p = (qry2 + past_len - key_p).astype(F32)
    dist_n = jnp.abs(qry2 - key_n).astype(F32)
    visible = (key_n // CHUNK) <= (qry2 // CHUNK)
    zeros = jnp.zeros((QK_DIM, t_new), BF16)
    for h in range(N_HEADS):
        hs = slice(h * HEAD_W, (h + 1) * HEAD_W)
        slope2 = slope_ref[h] * LOG2E
        qt = q_ref[0, :, hs].astype(F32).T.astype(BF16)
        qt2 = jnp.concatenate([jnp.concatenate([qt[:QK_DIM], zeros], axis=0),
                               jnp.concatenate([zeros, qt[QK_DIM:]], axis=0)], axis=1)
        s_p = _dot(pk[:, hs].astype(BF16), qt2) - slope2 * dist_p
        s_n = jnp.where(visible, _dot(k_ref[0, :, hs], qt2) - slope2 * dist_n, NEG_INF)
        m = jnp.maximum(jnp.max(s_p, axis=0, keepdims=True), jnp.max(s_n, axis=0, keepdims=True))
        p_p = jnp.exp2(s_p - m)
        p_n = jnp.exp2(s_n - m)
        l = jnp.sum(p_p, axis=0, keepdims=True) + jnp.sum(p_n, axis=0, keepdims=True)
        acc = (_dot_tn(pv[:, hs].astype(BF16), p_p.astype(BF16))
               + _dot(vt_ref[0, hs, :], p_n.astype(BF16))) / l
        ot = acc[:, :t_new] - lam * acc[:, t_new:]
        ot = ot * lax.rsqrt(jnp.mean(ot * ot, axis=0, keepdims=True) + EPS)
        o_ref[0, :, hs] = (ot.T * g_ref[...] * (1.0 - LAM_INIT)).astype(o_ref.dtype)


def _diff_attn_step(q, k, vt, past_k, past_v, slopes, lam, subln_g):
    B, T, _ = q.shape
    past_len = past_k.shape[1]
    smem = pl.BlockSpec(memory_space=pltpu.SMEM)
    tok = pl.BlockSpec((1, T, GROUP_W), lambda b: (b, 0, 0))
    pspec = pl.BlockSpec((1, past_len, N_HEADS, HEAD_W), lambda b: (b, 0, 0, 0))
    return pl.pallas_call(
        functools.partial(_attn_step_kernel, t_new=T, past_len=past_len),
        grid=(B,),
        in_specs=[smem, smem, tok, tok, pl.BlockSpec((1, GROUP_W, T), lambda b: (b, 0, 0)),
                  pspec, pspec, pl.BlockSpec((1, HEAD_W), lambda b: (0, 0))],
        out_specs=tok,
        out_shape=jax.ShapeDtypeStruct((B, T, GROUP_W), BF16),
        compiler_params=_cparams(("parallel",)),
        name="diff_attn_step",
    )(slopes, lam, q, k, vt, past_k, past_v, subln_g.reshape(1, HEAD_W))


def _split3(x):
    hi = x.astype(BF16)
    r = x - hi.astype(F32)
    mid = r.astype(BF16)
    lo = (r - mid.astype(F32)).astype(BF16)
    return hi, mid, lo


def _hgrn_kernel(q_ref, k_ref, lf_ref, v_ref, sg_ref, gn_ref, s0_ref, o_ref, sfin_ref,
                 st_ref, b_ref, *, tb):
    t = pl.program_id(1)
    n_chunks = tb // CHUNK

    @pl.when(t == 0)
    def _():
        for h in range(N_HEADS):
            st_ref[h] = s0_ref[0, h].T

    row = lax.broadcasted_iota(jnp.int32, (tb, tb), 0)
    col = lax.broadcasted_iota(jnp.int32, (tb, tb), 1)
    tri = jnp.where((col <= row) & (col // CHUNK == row // CHUNK), 1.0, 0.0).astype(BF16)
    hi, mid, lo = _split3(lf_ref[0] * LOG2E)
    b_ref[...] = _dot(tri, hi) + _dot(tri, mid) + _dot(tri, lo)

    sub_row = lax.broadcasted_iota(jnp.int32, (8, HEAD_W), 0)
    zeros16 = jnp.zeros((SUB, HEAD_W), F32)

    def pad_rows(x, r0):
        parts = []
        if r0:
            parts.append(jnp.zeros((r0, HEAD_W), F32))
        parts.append(x)
        rest = CHUNK - r0 - x.shape[0]
        if rest:
            parts.append(jnp.zeros((rest, HEAD_W), F32))
        return jnp.concatenate(parts, axis=0) if len(parts) > 1 else x

    def chunk_body(c, carry):
        r0 = pl.multiple_of(c * CHUNK, CHUNK)
        def head(h):
            hs = slice(h * HEAD_W, (h + 1) * HEAD_W)
            q = q_ref[0, pl.ds(r0, CHUNK), hs].astype(F32)
            k = k_ref[0, pl.ds(r0, CHUNK), hs].astype(F32)
            v_bf = v_ref[0, pl.ds(r0, CHUNK), hs]
            v = v_bf.astype(F32)
            b = b_ref[pl.ds(r0, CHUNK), hs]
            b_last = b[CHUNK - 1:CHUNK, :]
            st = st_ref[h]

            o = _dot_nt((q * jnp.exp2(b)).astype(BF16), st.astype(BF16))

            q_segs, k_segs = [], []
            for i in range(1, CHUNK // SUB):
                lo_r, hi_r = i * SUB, (i + 1) * SUB
                ref_b = b[lo_r - 1:lo_r, :]
                q_segs.append(pad_rows(q[lo_r:hi_r] * jnp.exp2(b[lo_r:hi_r] - ref_b), lo_r))
                k_segs.append(pad_rows(k[:lo_r] * jnp.exp2(ref_b - b[:lo_r]), 0))
            a_off = _dot_nt(jnp.concatenate(q_segs, axis=1).astype(BF16),
                            jnp.concatenate(k_segs, axis=1).astype(BF16))
            o = o + _dot(a_off.astype(BF16), v_bf)
            yield

            diag = []
            for blk in range(CHUNK // SUB):
                base = blk * SUB
                qa, qb = q[base:base + 8], q[base + 8:base + SUB]
                ba, bb = b[base:base + 8], b[base + 8:base + SUB]
                oa = jnp.zeros((8, HEAD_W), F32)
                ob = jnp.zeros((8, HEAD_W), F32)
                for s in range(SUB):
                    ks = k[base + s:base + s + 1]
                    bs = b[base + s:base + s + 1]
                    vs = v[base + s:base + s + 1]
                    if s < 8:
                        e = jnp.exp2(ba - bs)
                        if s:
                            e = jnp.where(sub_row >= s, e, 0.0)
                        oa = oa + jnp.sum(qa * ks * e, axis=-1, keepdims=True) * vs
                        ob = ob + jnp.sum(qb * ks * jnp.exp2(bb - bs), axis=-1, keepdims=True) * vs
                    else:
                        e = jnp.exp2(bb - bs)
                        if s > 8:
                            e = jnp.where(sub_row >= s - 8, e, 0.0)
                        ob = ob + jnp.sum(qb * ks * e, axis=-1, keepdims=True) * vs
                diag += [oa, ob]
                yield
            o = o + jnp.concatenate(diag, axis=0)

            k_dec = (k * jnp.exp2(b_last - b)).astype(BF16)
            st_ref[h] = st * jnp.exp2(b_last) + _dot_tn(v_bf, k_dec)

            out = _rms(o, gn_ref[...]) * sg_ref[0, pl.ds(r0, CHUNK), hs]
            o_ref[0, pl.ds(r0, CHUNK), hs] = out.astype(o_ref.dtype)
            yield

        _run_skewed([head(h) for h in range(N_HEADS)])
        return carry

    lax.fori_loop(0, n_chunks, chunk_body, 0)

    @pl.when(t == pl.num_programs(1) - 1)
    def _():
        for h in range(N_HEADS):
            sfin_ref[0, h] = st_ref[h].T


def _hgrn(qh, kk, lf, hi, sg, gnorm_g, s0, tb):
    B, T, _ = qh.shape
    tspec = pl.BlockSpec((1, tb, GROUP_W), lambda b, t: (b, t, 0))
    sspec = pl.BlockSpec((1, N_HEADS, HEAD_W, HEAD_W), lambda b, t: (b, 0, 0, 0))
    return pl.pallas_call(
        functools.partial(_hgrn_kernel, tb=tb),
        grid=(B, T // tb),
        in_specs=[tspec] * 5 + [pl.BlockSpec((1, HEAD_W), lambda b, t: (0, 0)), sspec],
        out_specs=[tspec, sspec],
        out_shape=[jax.ShapeDtypeStruct((B, T, GROUP_W), BF16),
                   jax.ShapeDtypeStruct((B, N_HEADS, HEAD_W, HEAD_W), F32)],
        scratch_shapes=[pltpu.VMEM((N_HEADS, HEAD_W, HEAD_W), F32), pltpu.VMEM((tb, GROUP_W), F32)],
        compiler_params=_cparams(("parallel", "arbitrary")),
        name="hgrn",
    )(qh, kk, lf, hi, sg, gnorm_g.reshape(1, HEAD_W), s0)


def _post_kernel(x_ref, oda_ref, ohg_ref, mk_ref, mv_ref, wout_ref, lnm_ref, wq_ref, wo_ref,
                 lnf_ref, wr_ref, br_ref, x2_ref, h3_ref, route_ref, om_ref, *, nb, tm, split):
    gt = tm // split
    groups = [_post_group(slice(g * gt, (g + 1) * gt), g * nb * gt, x_ref, oda_ref, ohg_ref,
                          mk_ref, mv_ref, wout_ref, lnm_ref, wq_ref, wo_ref, lnf_ref, wr_ref,
                          br_ref, x2_ref, h3_ref, route_ref, om_ref, nb=nb, gt=gt)
              for g in range(split)]
    _run_skewed(groups)


def _post_group(ts, om0, x_ref, oda_ref, ohg_ref, mk_ref, mv_ref, wout_ref, lnm_ref, wq_ref, wo_ref,
                lnf_ref, wr_ref, br_ref, x2_ref, h3_ref, route_ref, om_ref, *, nb, gt):
    rows = nb * gt
    x = x_ref[:, ts, :].reshape(rows, D_MODEL)
    mixed = (_dot(oda_ref[:, ts, :].reshape(rows, GROUP_W), wout_ref[:GROUP_W, :])
             + _dot(ohg_ref[:, ts, :].reshape(rows, GROUP_W), wout_ref[GROUP_W:, :]))
    x1 = x + mixed
    yield

    hm = _rms(x1, lnm_ref[...]).astype(BF16)
    yield
    qm = (_dot(hm, wq_ref[...]) * MEM_SCALE).astype(BF16)
    yield
    for b in range(nb):
        if len(mk_ref.shape) == 4:
            mk_b = mk_ref[b].reshape(N_MEM, D_MODEL).astype(BF16)
            mv_b = mv_ref[b].reshape(N_MEM, D_MODEL).astype(BF16)
        for h in range(MEM_HEADS):
            hs = slice(h * MEM_HEAD_DIM, (h + 1) * MEM_HEAD_DIM)
            if len(mk_ref.shape) == 4:
                mk, mv = mk_b[:, hs], mv_b[:, hs]
            else:
                mk, mv = mk_ref[b, :, hs], mv_ref[b, :, hs]
            s = _dot_nt(qm[b * gt:(b + 1) * gt, hs], mk.astype(BF16))
            e = jnp.exp(s - jnp.max(s, axis=-1, keepdims=True))
            p = e / jnp.sum(e, axis=-1, keepdims=True)
            om_ref[om0 + b * gt:om0 + (b + 1) * gt, hs] = _dot(
                p.astype(BF16), mv.astype(BF16)).astype(BF16)
    yield
    x2 = x1 + _dot(om_ref[om0:om0 + rows, :], wo_ref[...])
    x2_ref[:, ts, :] = x2.reshape(nb, gt, D_MODEL)
    yield

    h3 = _rms(x2, lnf_ref[...])
    h3_ref[:, ts] = h3.reshape(nb, gt, N_SLAB, HEAD_W)
    yield

    r = _dot(h3.astype(BF16), wr_ref[...]) + br_ref[...]
    lane = lax.broadcasted_iota(jnp.int32, r.shape, 1).astype(F32)
    big = float(4 * HEAD_W)
    g_mask = lane < N_GROUPS
    gl = jnp.where(g_mask, r, NEG_INF)
    g_max = jnp.max(gl, axis=-1, keepdims=True)
    g_idx = jnp.min(jnp.where(gl == g_max, lane, big), axis=-1, keepdims=True)
    g_w = 1.0 / jnp.sum(jnp.where(g_mask, jnp.exp(r - g_max), 0.0), axis=-1, keepdims=True)
    e_lo = N_GROUPS + EXPERTS_PER_GROUP * g_idx
    el = jnp.where((lane >= e_lo) & (lane < e_lo + EXPERTS_PER_GROUP), r, NEG_INF)
    v1 = jnp.max(el, axis=-1, keepdims=True)
    i1 = jnp.min(jnp.where(el == v1, lane, big), axis=-1, keepdims=True)
    el2 = jnp.where(lane == i1, NEG_INF, el)
    v2 = jnp.max(el2, axis=-1, keepdims=True)
    i2 = jnp.min(jnp.where(el2 == v2, lane, big), axis=-1, keepdims=True)
    t = jnp.exp(v2 - v1)
    p1 = 1.0 / (1.0 + t)
    rec = jnp.where(lane == 0.0, i1 - N_GROUPS,
          jnp.where(lane == 1.0, i2 - N_GROUPS,
          jnp.where(lane == 2.0, p1 * g_w,
          jnp.where(lane == 3.0, t * p1 * g_w, 0.0))))
    route_ref[:, ts, :] = rec[:, :ROUTE_W].reshape(nb, gt, ROUTE_W)


def _post_mix(x, oda, ohg, mem_k, mem_v, w_out, ln_mem, w_q, w_o, ln_ffn, w_r, b_r, nb, tm, split):
    B, T, _ = x.shape
    tok = lambda w: pl.BlockSpec((nb, tm, w), lambda b, t: (b, t, 0))
    if mem_k.ndim == 4:
        memspec = pl.BlockSpec((nb, N_MEM, MEM_HEADS, MEM_HEAD_DIM), lambda b, t: (b, 0, 0, 0))
    else:
        memspec = pl.BlockSpec((nb, N_MEM, D_MODEL), lambda b, t: (b, 0, 0))
    fix = lambda s: pl.BlockSpec(s, lambda b, t: (0, 0))
    return pl.pallas_call(
        functools.partial(_post_kernel, nb=nb, tm=tm, split=split),
        grid=(B // nb, T // tm),
        in_specs=[tok(D_MODEL), tok(GROUP_W), tok(GROUP_W), memspec, memspec,
                  fix((D_MODEL, D_MODEL)), fix((1, D_MODEL)), fix((D_MODEL, D_MODEL)),
                  fix((D_MODEL, D_MODEL)), fix((1, D_MODEL)), fix((D_MODEL, HEAD_W)),
                  fix((1, HEAD_W))],
        out_specs=[tok(D_MODEL),
                   pl.BlockSpec((nb, tm, N_SLAB, HEAD_W), lambda b, t: (b, t, 0, 0)),
                   tok(ROUTE_W)],
        out_shape=[jax.ShapeDtypeStruct((B, T, D_MODEL), F32),
                   jax.ShapeDtypeStruct((B, T, N_SLAB, HEAD_W), F32),
                   jax.ShapeDtypeStruct((B, T, ROUTE_W), F32)],
        scratch_shapes=[pltpu.VMEM((nb * tm, D_MODEL), BF16)],
        compiler_params=_cparams(("parallel", "parallel")),
        name="post_mix",
    )(x, oda, ohg, mem_k, mem_v, w_out, ln_mem.reshape(1, D_MODEL), w_q, w_o,
      ln_ffn.reshape(1, D_MODEL), w_r, b_r)


def _experts_kernel(blk_e_ref, n_used_ref, src_next_ref, dst_prev_ref, src0_ref, dst_ref, h_hbm,
                    wg_ref, wu_ref, wd_ref, y_hbm, xbuf, ybuf, gsem, ssem, *, bm, m_tot):
    del blk_e_ref
    i = pl.program_id(0)
    n_used = n_used_ref[0]
    slot = i % 2

    def gather_row(tab_ref, r, s, priority=0):
        pltpu.make_async_copy(h_hbm.at[tab_ref[0, 0, r]], xbuf.at[s, r],
                              gsem.at[s]).start(priority=priority)

    def scatter_row(tab_ref, r, s, priority=0):
        pltpu.make_async_copy(ybuf.at[s, r], y_hbm.at[tab_ref[0, 0, r]],
                              ssem.at[s]).start(priority=priority)

    def wait_block(buf, sem, s):
        pltpu.make_async_copy(buf.at[s], buf.at[s], sem.at[s]).wait()

    def spare_fill(s):
        return pltpu.make_async_copy(ybuf.at[s], y_hbm.at[pl.ds(m_tot + s * bm, bm)], ssem.at[s])

    @pl.when(i == 0)
    def _():
        ybuf[...] = jnp.zeros(ybuf.shape, F32)
        spare_fill(0).start()
        spare_fill(1).start()
        spare_fill(1).wait()

        def body(r, c):
            gather_row(src0_ref, r, 0)
            return c
        lax.fori_loop(0, bm, body, 0)

    @pl.when(i < n_used)
    def _():
        wait_block(xbuf, gsem, slot)
        wait_block(ybuf, ssem, slot)
        x = xbuf[slot].reshape(bm, D_MODEL).astype(BF16)
        hg = _dot(x, wg_ref[0].astype(BF16))
        hu = _dot(x, wu_ref[0].astype(BF16))
        hb = (hg * jax.nn.sigmoid(hg) * hu).astype(BF16)
        ybuf[slot] = _dot(hb, wd_ref[0].astype(BF16)).reshape(bm, N_SLAB, HEAD_W)
        for r in range(bm):
            gather_row(src_next_ref, r, 1 - slot, priority=r % 2)
            scatter_row(dst_prev_ref, r, 1 - slot, priority=(r + 1) % 2)

    @pl.when(i == n_used - 1)
    def _():
        def body(r, c):
            scatter_row(dst_ref, r, slot)
            return c
        lax.fori_loop(0, bm, body, 0)
        wait_block(xbuf, gsem, 1 - slot)
        wait_block(ybuf, ssem, 1 - slot)
        wait_block(ybuf, ssem, slot)


def _experts(h3, blk_e, n_used, src_tab, dst_tab, dst_prev_tab, wg, wu, wd, bm):
    n_blocks = src_tab.shape[0]
    m_tot = TOP_K * h3.shape[0]
    tab = lambda f: pl.BlockSpec((1, 1, bm), f, memory_space=pltpu.SMEM)
    cur = lambda i, e, n: (i, 0, 0)
    nxt = lambda i, e, n: (jnp.minimum(i + 1, n_blocks - 1), 0, 0)
    first = lambda i, e, n: (0, 0, 0)
    wspec = lambda a, b: pl.BlockSpec((1, a, b), lambda i, e, n: (e[i], 0, 0))
    grid_spec = pltpu.PrefetchScalarGridSpec(
        num_scalar_prefetch=2,
        grid=(n_blocks,),
        in_specs=[tab(nxt), tab(cur), tab(first), tab(cur), pl.BlockSpec(memory_space=pl.ANY),
                  wspec(D_MODEL, EXPERT_FF), wspec(D_MODEL, EXPERT_FF), wspec(EXPERT_FF, D_MODEL)],
        out_specs=pl.BlockSpec(memory_space=pl.ANY),
        scratch_shapes=[pltpu.VMEM((2, bm, N_SLAB, HEAD_W), F32), pltpu.VMEM((2, bm, N_SLAB, HEAD_W), F32),
                        pltpu.SemaphoreType.DMA((2,)), pltpu.SemaphoreType.DMA((2,))],
    )
    return pl.pallas_call(
        functools.partial(_experts_kernel, bm=bm, m_tot=m_tot),
        grid_spec=grid_spec,
        out_shape=jax.ShapeDtypeStruct((m_tot + 2 * bm, N_SLAB, HEAD_W), F32),
        compiler_params=_cparams(("arbitrary",)),
        name="experts",
    )(blk_e, n_used, src_tab, dst_prev_tab, src_tab, dst_tab, h3, wg, wu, wd)


def _combine_kernel(x_ref, y0_ref, y1_ref, route_ref, g_ref, o_ref):
    rt = route_ref[...]
    tm = x_ref.shape[0]
    y0 = y0_ref[...].reshape(tm, D_MODEL)
    y1 = y1_ref[...].reshape(tm, D_MODEL)
    y = x_ref[...] + rt[:, 2:3] * y0 + rt[:, 3:4] * y1
    o_ref[...] = _rms(y, g_ref[...])


def _combine(x2, y, route, final_g, tm):
    n = x2.shape[0]
    nt = n // tm
    return pl.pallas_call(
        _combine_kernel,
        grid=(nt,),
        in_specs=[pl.BlockSpec((tm, D_MODEL), lambda i: (i, 0)),
                  pl.BlockSpec((tm, N_SLAB, HEAD_W), lambda i: (i, 0, 0)),
                  pl.BlockSpec((tm, N_SLAB, HEAD_W), lambda i: (i + nt, 0, 0)),
                  pl.BlockSpec((tm, ROUTE_W), lambda i: (i, 0)),
                  pl.BlockSpec((1, D_MODEL), lambda i: (0, 0))],
        out_specs=pl.BlockSpec((tm, D_MODEL), lambda i: (i, 0)),
        out_shape=jax.ShapeDtypeStruct((n, D_MODEL), F32),
        compiler_params=_cparams(("parallel",)),
        name="combine",
    )(x2, y, y, route, final_g.reshape(1, D_MODEL))


def _routing_tables(route, bm):
    n = route.shape[0]
    m_tot = TOP_K * n
    flat_e = jnp.concatenate([route[:, 0], route[:, 1]]).astype(jnp.int32)
    experts = jnp.arange(N_EXPERTS, dtype=jnp.int32)
    counts = jnp.sum(flat_e[:, None] == experts[None, :], axis=0, dtype=jnp.int32)
    n_blk_e = (counts + bm - 1) // bm
    blk_end = jnp.cumsum(n_blk_e)
    n_blocks = m_tot // bm + N_EXPERTS
    blk = jnp.arange(n_blocks, dtype=jnp.int32)
    blk_e = jnp.minimum(jnp.sum(blk[:, None] >= blk_end[None, :], axis=1, dtype=jnp.int32),
                        N_EXPERTS - 1)
    r = jnp.arange(bm, dtype=jnp.int32)[None, :]
    n_pad = (n_blk_e * bm - counts)[:, None]
    pad_keys = jnp.where(r < n_pad, 2 * experts[:, None] + 1, 2 * N_EXPERTS + 1)
    payload_bits = max(m_tot.bit_length(), 1)
    assert (2 * N_EXPERTS + 2) << payload_bits < 2 ** 31
    packed = jnp.concatenate([(2 * flat_e << payload_bits) + jnp.arange(1, m_tot + 1, dtype=jnp.int32),
                              pad_keys.reshape(-1) << payload_bits])
    m = ((lax.sort(packed) & ((1 << payload_bits) - 1)) - 1).reshape(n_blocks, bm)
    real = m >= 0
    spare = m_tot + (blk[:, None] % 2) * bm + r
    src = jnp.where(real, m % n, 0)
    dst = jnp.where(real, m, spare)
    dst_prev = jnp.concatenate([m_tot + bm + r, dst[:-1]], axis=0)
    shape = (n_blocks, 1, bm)
    return (blk_e, blk_end[N_EXPERTS - 1:].astype(jnp.int32), src.reshape(shape), dst.reshape(shape),
            dst_prev.reshape(shape))


def _layer(x, past_k, past_v, s0, mem_k, mem_v, p, cfg):
    B, T, _ = x.shape
    n = B * T
    (q, kf, vf, kb, vt, qh, lf, kk, hi, sg) = _in_proj(
        x, p["ln_mix"], p["w_in"], p["lb"], cfg["nb"], cfg["tm_proj"], cfg["split_proj"])
    if past_k is None:
        oda = _diff_attn(q, kb, vt, p["slopes"], p["lam"], p["da_subln"], cfg["tq"], cfg["tk"])
    else:
        oda = _diff_attn_step(q, kb, vt, past_k, past_v, p["slopes"], p["lam"], p["da_subln"])
    ohg, s_new = _hgrn(qh, kk, lf, hi, sg, p["hg_gnorm"], s0, cfg["tb"])
    x2, h3, route = _post_mix(x, oda, ohg, mem_k, mem_v, p["w_out"], p["ln_mem"], p["w_mem_q"],
                              p["w_mem_o"], p["ln_ffn"], p["w_r"], p["b_r"], cfg["nb"], cfg["tm_post"],
                              cfg["split"])
    route2 = route.reshape(n, ROUTE_W)
    blk_e, n_used, src_tab, dst_tab, dst_prev_tab = _routing_tables(route2, cfg["bm"])
    y = _experts(h3.reshape(n, N_SLAB, HEAD_W), blk_e, n_used, src_tab, dst_tab, dst_prev_tab,
                 p["e_gate"], p["e_up"], p["e_down"], cfg["bm"])
    out = _combine(x2.reshape(n, D_MODEL), y, route2, p["final_g"], cfg["tm_in"])
    return out.reshape(B, T, D_MODEL), kf[None], vf[None], s_new[None]


PROMPT_CFG = dict(tm_in=256, tq=512, tk=256, tb=256, nb=1, tm_proj=512, split_proj=2, tm_post=1024, split=4,
                  bm=512)
SAMPLE_CFG = dict(tm_in=256, tq=64, tk=64, tb=64, nb=4, tm_proj=64, split_proj=1, tm_post=64, split=1,
                  bm=128)


def kernel(x_prompt, x_sample, mem_prompt, cache_diff_k, cache_diff_v, state_hgrn, cache_mem_k, cache_mem_v, ln_mix_g, w_in, da_lambda, da_subln_g, hg_lb_logits, hg_gnorm_g, w_out, ln_mem_g, mem_norm_g, w_mem_q, w_mem_k, w_mem_v, w_mem_o, ln_ffn_g, router_group_w, router_group_b, router_expert_w, router_expert_b, exp_w_gate, exp_w_up, exp_w_down, final_norm_g):
    assert w_in.shape[0] == 1, "single-layer configuration"
    lb_all = jnp.cumsum(jax.nn.softmax(hg_lb_logits.astype(F32), axis=0), axis=0)
    lp = da_lambda[0].astype(F32)
    lam = jnp.exp(jnp.sum(lp[0] * lp[1])) - jnp.exp(jnp.sum(lp[2] * lp[3])) + LAM_INIT
    w_r = jnp.zeros((D_MODEL, HEAD_W), F32)
    w_r = w_r.at[:, :N_GROUPS].set(router_group_w[0]).at[:, N_GROUPS:N_GROUPS + N_EXPERTS].set(router_expert_w[0])
    b_r = jnp.zeros((1, HEAD_W), F32)
    b_r = b_r.at[0, :N_GROUPS].set(router_group_b[0]).at[0, N_GROUPS:N_GROUPS + N_EXPERTS].set(router_expert_b[0])
    p = {
        "ln_mix": ln_mix_g[0], "w_in": w_in[0].astype(BF16), "lb": lb_all[0],
        "slopes": jnp.exp2(-8.0 * jnp.arange(1, N_HEADS + 1, dtype=F32) / N_HEADS),
        "lam": lam.reshape(1), "da_subln": da_subln_g[0], "hg_gnorm": hg_gnorm_g[0],
        "w_out": w_out[0].astype(BF16), "ln_mem": ln_mem_g[0], "w_mem_q": w_mem_q[0].astype(BF16),
        "w_mem_o": w_mem_o[0].astype(BF16), "ln_ffn": ln_ffn_g[0], "w_r": w_r.astype(BF16), "b_r": b_r,
        "e_gate": exp_w_gate[0], "e_up": exp_w_up[0],
        "e_down": exp_w_down[0], "final_g": final_norm_g,
    }
    Bp, Tp, _ = x_prompt.shape
    Bs, Ts, _ = x_sample.shape

    mkf, mvf, mkb, mvb = _memory_kv(mem_prompt, mem_norm_g[0], w_mem_k[0].astype(BF16),
                                    w_mem_v[0].astype(BF16))
    zero_state = jnp.zeros((Bp, N_HEADS, HEAD_W, HEAD_W), F32)
    yp, kp, vp, sp = _layer(x_prompt, None, None, zero_state, mkb, mvb, p, PROMPT_CFG)
    ys, ks, vs, ss = _layer(
        x_sample, cache_diff_k[0], cache_diff_v[0], state_hgrn[0], cache_mem_k[0], cache_mem_v[0],
        p, SAMPLE_CFG)
    return (yp, ys, kp, vp, sp, mkf[None], mvf[None], ks, vs, ss)
```

```python
import functools
import math

import jax
import jax.numpy as jnp
from jax import lax
from jax.experimental import pallas as pl
from jax.experimental.pallas import tpu as pltpu

F32 = jnp.float32
BF16 = jnp.bfloat16

D_MODEL = 1024
EPS = 1e-5
CHUNK = 64
N_HEADS = 4
HEAD_W = 128
QK_DIM = 64
GROUP_W = N_HEADS * HEAD_W
N_PROJ = 7
DA_SCALE = QK_DIM ** -0.5
LOG2E = 1.4426950408889634
LAM_INIT = 0.8 - 0.6 * math.exp(-0.3 * 0)
N_MEM = 256
MEM_HEADS = 4
MEM_HEAD_DIM = D_MODEL // MEM_HEADS
MEM_SCALE = MEM_HEAD_DIM ** -0.5
N_GROUPS = 4
EXPERTS_PER_GROUP = 8
N_EXPERTS = N_GROUPS * EXPERTS_PER_GROUP
TOP_K = 2
EXPERT_FF = 512
ROUTE_W = 8
SUB = 16
SUB_KEYS = 256
N_SCORE_SLOTS = 3
N_SLAB = D_MODEL // HEAD_W
NEG_INF = float("-inf")

VMEM_LIMIT = 48 * 1024 * 1024


def _cparams(sem):
    return pltpu.CompilerParams(dimension_semantics=sem, vmem_limit_bytes=VMEM_LIMIT)


def _rms(x, g):
    return x * lax.rsqrt(jnp.mean(x * x, axis=-1, keepdims=True) + EPS) * g


def _dot(a, b):
    return jnp.dot(a, b, preferred_element_type=F32)


def _dot_nt(a, b):
    return lax.dot_general(a, b, (((1,), (1,)), ((), ())), preferred_element_type=F32)


def _dot_tn(a, b):
    return lax.dot_general(a, b, (((0,), (0,)), ((), ())), preferred_element_type=F32)


def _run_skewed(gens):
    waiting, live = list(gens), []
    while waiting or live:
        if waiting:
            live.append(waiting.pop(0))
        for gen in list(live):
            if next(gen, "done") == "done":
                live.remove(gen)


def _memkv_kernel(m_ref, g_ref, wk_ref, wv_ref, kf_ref, vf_ref, kb_ref, vb_ref):
    mn = _rms(m_ref[0], g_ref[...]).astype(BF16)
    k = _dot(mn, wk_ref[...])
    v = _dot(mn, wv_ref[...])
    kf_ref[0] = k.reshape(N_MEM, MEM_HEADS, MEM_HEAD_DIM)
    vf_ref[0] = v.reshape(N_MEM, MEM_HEADS, MEM_HEAD_DIM)
    kb_ref[0] = k.astype(BF16)
    vb_ref[0] = v.astype(BF16)


def _memory_kv(mem, g, wk, wv):
    B = mem.shape[0]
    blk = pl.BlockSpec((1, N_MEM, D_MODEL), lambda b: (b, 0, 0))
    hblk = pl.BlockSpec((1, N_MEM, MEM_HEADS, MEM_HEAD_DIM), lambda b: (b, 0, 0, 0))
    wspec = pl.BlockSpec((D_MODEL, D_MODEL), lambda b: (0, 0))
    return pl.pallas_call(
        _memkv_kernel,
        grid=(B,),
        in_specs=[blk, pl.BlockSpec((1, D_MODEL), lambda b: (0, 0)), wspec, wspec],
        out_specs=[hblk, hblk, blk, blk],
        out_shape=[jax.ShapeDtypeStruct((B, N_MEM, MEM_HEADS, MEM_HEAD_DIM), F32)] * 2
                  + [jax.ShapeDtypeStruct(mem.shape, BF16)] * 2,
        compiler_params=_cparams(("parallel",)),
        name="memory_kv",
    )(mem, g.reshape(1, D_MODEL), wk, wv)


def _in_proj_kernel(x_ref, g_ref, w_ref, lb_ref, q_ref, kf_ref, vf_ref, kb_ref, vt_ref,
                    qh_ref, lf_ref, kk_ref, hi_ref, sg_ref, *, nb, tm, split):
    gt = tm // split
    out_refs = (q_ref, kf_ref, vf_ref, kb_ref, vt_ref, qh_ref, lf_ref, kk_ref, hi_ref, sg_ref)
    _run_skewed([_in_proj_group(slice(g * gt, (g + 1) * gt), x_ref, g_ref, w_ref, lb_ref, out_refs,
                                nb=nb, gt=gt) for g in range(split)])


def _in_proj_group(ts, x_ref, g_ref, w_ref, lb_ref, out_refs, *, nb, gt):
    q_ref, kf_ref, vf_ref, kb_ref, vt_ref, qh_ref, lf_ref, kk_ref, hi_ref, sg_ref = out_refs
    rows = nb * gt
    h = _rms(x_ref[:, ts, :].reshape(rows, D_MODEL), g_ref[...]).astype(BF16)
    yield

    def proj(i):
        return _dot(h, w_ref[:, i * GROUP_W:(i + 1) * GROUP_W])

    def put(ref, val):
        ref[:, ts, :] = val.reshape(nb, gt, GROUP_W).astype(ref.dtype)

    def put_heads(ref, val):
        ref[:, ts] = val.reshape(nb, gt, N_HEADS, HEAD_W)

    put(q_ref, proj(0) * (DA_SCALE * LOG2E))
    yield
    dk = proj(1)
    put_heads(kf_ref, dk)
    put(kb_ref, dk)
    yield
    dv = proj(2)
    put_heads(vf_ref, dv)
    for b in range(nb):
        vt_ref[b, :, ts] = dv[b * gt:(b + 1) * gt].T.astype(BF16)
    yield
    hq = proj(3)
    put(qh_ref, hq * jax.nn.sigmoid(hq))
    yield
    hf = proj(4)
    lb = lb_ref[...]
    put(lf_ref, jnp.log(lb + (1.0 - lb) * jax.nn.sigmoid(hf)))
    put(kk_ref, (1.0 - lb) * jax.nn.sigmoid(-hf))
    yield
    put(hi_ref, proj(5))
    yield
    hg = proj(6)
    put(sg_ref, hg * jax.nn.sigmoid(hg))
    yield


def _in_proj(x, g, w_bf, lb, nb, tm, split):
    B, T, _ = x.shape
    tok = lambda w: pl.BlockSpec((nb, tm, w), lambda b, t: (b, t, 0))
    fix = lambda s: pl.BlockSpec(s, lambda b, t: (0, 0))
    f32o = jax.ShapeDtypeStruct((B, T, GROUP_W), F32)
    bfo = jax.ShapeDtypeStruct((B, T, GROUP_W), BF16)
    o_spec = tok(GROUP_W)
    ho = jax.ShapeDtypeStruct((B, T, N_HEADS, HEAD_W), F32)
    h_spec = pl.BlockSpec((nb, tm, N_HEADS, HEAD_W), lambda b, t: (b, t, 0, 0))
    vt_spec = pl.BlockSpec((nb, GROUP_W, tm), lambda b, t: (b, 0, t))
    return pl.pallas_call(
        functools.partial(_in_proj_kernel, nb=nb, tm=tm, split=split),
        grid=(B // nb, T // tm),
        in_specs=[tok(D_MODEL), fix((1, D_MODEL)), fix((D_MODEL, N_PROJ * GROUP_W)), fix((1, GROUP_W))],
        out_specs=[o_spec, h_spec, h_spec, o_spec, vt_spec] + [o_spec] * 5,
        out_shape=[bfo, ho, ho, bfo, jax.ShapeDtypeStruct((B, GROUP_W, T), BF16),
                   bfo, f32o, bfo, bfo, bfo],
        compiler_params=_cparams(("parallel", "parallel")),
        name="in_proj",
    )(x, g.reshape(1, D_MODEL), w_bf, lb.reshape(1, GROUP_W))


def _attn_kernel(slope_ref, lam_ref, q_ref, k_ref, vt_ref, g_ref, o_ref,
                 m_ref, l_ref, acc_ref, base_ref, t_ref, mb_ref, *, tq, tk):
    h = pl.program_id(1)
    qi = pl.program_id(2)
    slope2 = slope_ref[h] * LOG2E
    qt = q_ref[0].astype(F32).T.astype(BF16)
    zeros = jnp.zeros((QK_DIM, tq), BF16)
    qt_maps = (jnp.concatenate([qt[:QK_DIM], zeros], axis=0),
               jnp.concatenate([zeros, qt[QK_DIM:]], axis=0))
    m_ref[...] = jnp.full(m_ref.shape, NEG_INF, F32)
    l_ref[...] = jnp.zeros(l_ref.shape, F32)
    acc_ref[...] = jnp.zeros(acc_ref.shape, F32)
    q0 = qi * tq

    def lane_pad(x, lo, fill):
        if lo == 0:
            return x
        return jnp.concatenate([jnp.full((x.shape[0], lo), fill, x.dtype), x], axis=1)


    def score_phase(slot, n_keys, get_k, get_bias, shift, q_lo=lambda u: 0):
        sub = min(n_keys, SUB_KEYS)
        mx = [None, None]
        for u in range(n_keys // sub):
            lo = q_lo(u)
            bias, visible = get_bias(u)
            k = get_k(u)
            for c in range(2):
                t = _dot(k, qt_maps[c][:, lo:]) + bias
                if visible is not None:
                    w = visible.shape[1]
                    head = jnp.where(visible, t[:, :w], NEG_INF)
                    t = head if w == t.shape[1] else jnp.concatenate([head, t[:, w:]], axis=1)
                t_ref[slot, c, u * sub:(u + 1) * sub, lo:] = t
                mu = lane_pad(jnp.max(t, axis=0, keepdims=True), lo, NEG_INF)
                mx[c] = mu if mx[c] is None else jnp.maximum(mx[c], mu)
            yield
        for c in range(2):
            mb_ref[slot, c] = mx[c] + shift
        yield

    def prob_phase(slot, n_keys, get_vt, shift, q_lo=lambda u: 0, pv_group=1):
        sub = min(n_keys, SUB_KEYS)
        n_sub = n_keys // sub
        off, alpha = [], []
        for c in range(2):
            m_old = m_ref[c]
            m_new = jnp.maximum(m_old, mb_ref[slot, c])
            off.append(m_new - shift)
            alpha.append(jnp.exp2(m_old - m_new))
            m_ref[c] = m_new
        yield
        lsum = [None, None]
        pv = [None, None]
        grp = pv_group if n_sub % pv_group == 0 else 1
        for u in range(0, n_sub, grp):
            lo = q_lo(u)
            assert all(q_lo(u + i) == lo for i in range(grp))
            parts = [get_vt(u + i) for i in range(grp)]
            vt = parts[0] if grp == 1 else jnp.concatenate(parts, axis=1)
            for c in range(2):
                p = jnp.exp2(t_ref[slot, c, u * sub:(u + grp) * sub, lo:] - off[c][:, lo:])
                ls = lane_pad(jnp.sum(p, axis=0, keepdims=True), lo, 0.0)
                pu = lane_pad(_dot(vt, p.astype(BF16)), lo, 0.0)
                lsum[c] = ls if lsum[c] is None else lsum[c] + ls
                pv[c] = pu if pv[c] is None else pv[c] + pu
            yield
        for c in range(2):
            l_ref[c] = alpha[c] * l_ref[c] + lsum[c]
            acc_ref[c] = alpha[c] * acc_ref[c] + pv[c]
        yield

    def run(*gens):
        live = list(gens)
        while live:
            for gen in list(live):
                if next(gen, "done") == "done":
                    live.remove(gen)

    def update(n_keys, get_k, get_vt, get_bias, shift, q_lo=lambda u: 0, pv_group=1):
        run(score_phase(0, n_keys, get_k, get_bias, shift, q_lo))
        run(prob_phase(0, n_keys, get_vt, shift, q_lo, pv_group))

    def rel_bias(n_keys, key0):
        key = lax.broadcasted_iota(jnp.int32, (n_keys, tq), 0) + key0
        qry = lax.broadcasted_iota(jnp.int32, (n_keys, tq), 1)
        return (qry - key).astype(F32) * (-slope2)

    def block_shift(k0):
        return (q0 - k0).astype(F32) * (-slope2)

    n_q = k_ref.shape[1] // tq
    if n_q > 1 or tq > SUB_KEYS:
        @pl.when(qi == 0)
        def _():
            base_ref[...] = rel_bias(base_ref.shape[0], 0)

    kd = 0 if n_q == 1 else pl.multiple_of(qi * tq, tq)
    sub_d = min(tq, SUB_KEYS)
    key = lax.broadcasted_iota(jnp.int32, (sub_d, sub_d), 0)
    qry = lax.broadcasted_iota(jnp.int32, (sub_d, sub_d), 1)
    strip_bias = jnp.abs(qry - key).astype(F32) * (-slope2)
    strip_visible = (key // CHUNK) <= (qry // CHUNK)

    def diag_bias(u):
        lo = (u + 1) * sub_d
        if lo == tq:
            return strip_bias, strip_visible
        later = base_ref[u * sub_d:(u + 1) * sub_d, lo:]
        return jnp.concatenate([strip_bias, later], axis=1), strip_visible

    diag_k = lambda u: k_ref[0, pl.ds(kd + u * sub_d, sub_d), :]
    diag_vt = lambda u: vt_ref[0, :, pl.ds(kd + u * sub_d, sub_d)]
    diag_lo = lambda u: u * sub_d

    if n_q == 1:
        update(tq, diag_k, diag_vt, diag_bias, 0.0, q_lo=diag_lo)
    else:
        sub_k = min(tk, SUB_KEYS)
        n_prev = qi * (tq // tk)
        assert (tq // tk) % 2 == 0, "earlier blocks are consumed in pairs"

        def prev_scores(slot, j):
            k0 = pl.multiple_of(j * tk, tk)
            return score_phase(slot, tk, lambda u: k_ref[0, pl.ds(k0 + u * sub_k, sub_k), :],
                               lambda u: (base_ref[u * sub_k:(u + 1) * sub_k, :], None),
                               block_shift(k0))

        def prev_probs(slot, j):
            k0 = pl.multiple_of(j * tk, tk)
            return prob_phase(slot, tk, lambda u: vt_ref[0, :, pl.ds(k0 + u * sub_k, sub_k)],
                              block_shift(k0), pv_group=2)

        run(score_phase(0, tq, diag_k, diag_bias, 0.0, diag_lo))
        run(prob_phase(0, tq, diag_vt, 0.0, diag_lo), prev_scores(1, 0))

        def pair_body(i, carry):
            j = 2 * i
            run(prev_probs(1, j), prev_scores(2, j + 1))
            run(prev_probs(2, j + 1), prev_scores(1, jnp.minimum(j + 2, n_prev - 1)))
            return carry
        lax.fori_loop(0, n_prev // 2, pair_body, 0)

    lam = lam_ref[0]
    ot = acc_ref[0] / l_ref[0] - lam * (acc_ref[1] / l_ref[1])
    ot = ot * lax.rsqrt(jnp.mean(ot * ot, axis=0, keepdims=True) + EPS)
    o_ref[0] = (ot.T * g_ref[...] * (1.0 - LAM_INIT)).astype(o_ref.dtype)


def _diff_attn(q, k, vt, slopes, lam, subln_g, tq, tk):
    B, T, _ = q.shape
    smem = pl.BlockSpec(memory_space=pltpu.SMEM)
    qspec = pl.BlockSpec((1, tq, HEAD_W), lambda b, h, i: (b, i, h))
    return pl.pallas_call(
        functools.partial(_attn_kernel, tq=tq, tk=tk),
        grid=(B, N_HEADS, T // tq),
        in_specs=[smem, smem, qspec,
                  pl.BlockSpec((1, T, HEAD_W), lambda b, h, i: (b, 0, h)),
                  pl.BlockSpec((1, HEAD_W, T), lambda b, h, i: (b, h, 0)),
                  pl.BlockSpec((1, HEAD_W), lambda b, h, i: (0, 0))],
        out_specs=qspec,
        out_shape=jax.ShapeDtypeStruct((B, T, GROUP_W), BF16),
        scratch_shapes=[pltpu.VMEM((2, 1, tq), F32), pltpu.VMEM((2, 1, tq), F32),
                        pltpu.VMEM((2, HEAD_W, tq), F32), pltpu.VMEM((max(tk, tq), tq), F32),
                        pltpu.VMEM((N_SCORE_SLOTS, 2, max(tk, tq), tq), F32),
                        pltpu.VMEM((N_SCORE_SLOTS, 2, 1, tq), F32)],
        compiler_params=_cparams(("parallel", "parallel", "arbitrary")),
        name="diff_attn",
    )(slopes, lam, q, k, vt, subln_g.reshape(1, HEAD_W))


def _attn_step_kernel(slope_ref, lam_ref, q_ref, k_ref, vt_ref, pk_ref, pv_ref, g_ref, o_ref, *,
                      t_new, past_len):
    assert 2 * t_new <= HEAD_W and t_new % CHUNK == 0 and past_len % CHUNK == 0
    pk = pk_ref[0].reshape(past_len, GROUP_W)
    pv = pv_ref[0].reshape(past_len, GROUP_W)
    lam = lam_ref[0]
    qry2 = lax.broadcasted_iota(jnp.int32, (1, 2 * t_new), 1) % t_new
    key_p = lax.broadcasted_iota(jnp.int32, (past_len, 2 * t_new), 0)
    key_n = lax.broadcasted_iota(jnp.int32, (t_new, 2 * t_new), 0)
    dist_p = (qry2 + past_len - key_p).astype(F32)
    dist_n = jnp.abs(qry2 - key_n).astype(F32)
    visible = (key_n // CHUNK) <= (qry2 // CHUNK)
    zeros = jnp.zeros((QK_DIM, t_new), BF16)
    for h in range(N_HEADS):
        hs = slice(h * HEAD_W, (h + 1) * HEAD_W)
        slope2 = slope_ref[h] * LOG2E
        qt = q_ref[0, :, hs].astype(F32).T.astype(BF16)
        qt2 = jnp.concatenate([jnp.concatenate([qt[:QK_DIM], zeros], axis=0),
                               jnp.concatenate([zeros, qt[QK_DIM:]], axis=0)], axis=1)
        s_p = _dot(pk[:, hs].astype(BF16), qt2) - slope2 * dist_p
        s_n = jnp.where(visible, _dot(k_ref[0, :, hs], qt2) - slope2 * dist_n, NEG_INF)
        m = jnp.maximum(jnp.max(s_p, axis=0, keepdims=True), jnp.max(s_n, axis=0, keepdims=True))
        p_p = jnp.exp2(s_p - m)
        p_n = jnp.exp2(s_n - m)
        l = jnp.sum(p_p, axis=0, keepdims=True) + jnp.sum(p_n, axis=0, keepdims=True)
        acc = (_dot_tn(pv[:, hs].astype(BF16), p_p.astype(BF16))
               + _dot(vt_ref[0, hs, :], p_n.astype(BF16))) / l
        ot = acc[:, :t_new] - lam * acc[:, t_new:]
        ot = ot * lax.rsqrt(jnp.mean(ot * ot, axis=0, keepdims=True) + EPS)
        o_ref[0, :, hs] = (ot.T * g_ref[...] * (1.0 - LAM_INIT)).astype(o_ref.dtype)


def _diff_attn_step(q, k, vt, past_k, past_v, slopes, lam, subln_g):
    B, T, _ = q.shape
    past_len = past_k.shape[1]
    smem = pl.BlockSpec(memory_space=pltpu.SMEM)
    tok = pl.BlockSpec((1, T, GROUP_W), lambda b: (b, 0, 0))
    pspec = pl.BlockSpec((1, past_len, N_HEADS, HEAD_W), lambda b: (b, 0, 0, 0))
    return pl.pallas_call(
        functools.partial(_attn_step_kernel, t_new=T, past_len=past_len),
        grid=(B,),
        in_specs=[smem, smem, tok, tok, pl.BlockSpec((1, GROUP_W, T), lambda b: (b, 0, 0)),
                  pspec, pspec, pl.BlockSpec((1, HEAD_W), lambda b: (0, 0))],
        out_specs=tok,
        out_shape=jax.ShapeDtypeStruct((B, T, GROUP_W), BF16),
        compiler_params=_cparams(("parallel",)),
        name="diff_attn_step",
    )(slopes, lam, q, k, vt, past_k, past_v, subln_g.reshape(1, HEAD_W))


def _split3(x):
    hi = x.astype(BF16)
    r = x - hi.astype(F32)
    mid = r.astype(BF16)
    lo = (r - mid.astype(F32)).astype(BF16)
    return hi, mid, lo


def _hgrn_kernel(q_ref, k_ref, lf_ref, v_ref, sg_ref, gn_ref, s0_ref, o_ref, sfin_ref,
                 st_ref, b_ref, *, tb):
    t = pl.program_id(1)
    n_chunks = tb // CHUNK

    @pl.when(t == 0)
    def _():
        for h in range(N_HEADS):
            st_ref[h] = s0_ref[0, h].T

    row = lax.broadcasted_iota(jnp.int32, (tb, tb), 0)
    col = lax.broadcasted_iota(jnp.int32, (tb, tb), 1)
    tri = jnp.where((col <= row) & (col // CHUNK == row // CHUNK), 1.0, 0.0).astype(BF16)
    hi, mid, lo = _split3(lf_ref[0] * LOG2E)
    b_ref[...] = _dot(tri, hi) + _dot(tri, mid) + _dot(tri, lo)

    sub_row = lax.broadcasted_iota(jnp.int32, (8, HEAD_W), 0)
    zeros16 = jnp.zeros((SUB, HEAD_W), F32)

    def pad_rows(x, r0):
        parts = []
        if r0:
            parts.append(jnp.zeros((r0, HEAD_W), F32))
        parts.append(x)
        rest = CHUNK - r0 - x.shape[0]
        if rest:
            parts.append(jnp.zeros((rest, HEAD_W), F32))
        return jnp.concatenate(parts, axis=0) if len(parts) > 1 else x

    def chunk_body(c, carry):
        r0 = pl.multiple_of(c * CHUNK, CHUNK)
        def head(h):
            hs = slice(h * HEAD_W, (h + 1) * HEAD_W)
            q = q_ref[0, pl.ds(r0, CHUNK), hs].astype(F32)
            k = k_ref[0, pl.ds(r0, CHUNK), hs].astype(F32)
            v_bf = v_ref[0, pl.ds(r0, CHUNK), hs]
            v = v_bf.astype(F32)
            b = b_ref[pl.ds(r0, CHUNK), hs]
            b_last = b[CHUNK - 1:CHUNK, :]
            st = st_ref[h]

            o = _dot_nt((q * jnp.exp2(b)).astype(BF16), st.astype(BF16))

            q_segs, k_segs = [], []
            for i in range(1, CHUNK // SUB):
                lo_r, hi_r = i * SUB, (i + 1) * SUB
                ref_b = b[lo_r - 1:lo_r, :]
                q_segs.append(pad_rows(q[lo_r:hi_r] * jnp.exp2(b[lo_r:hi_r] - ref_b), lo_r))
                k_segs.append(pad_rows(k[:lo_r] * jnp.exp2(ref_b - b[:lo_r]), 0))
            a_off = _dot_nt(jnp.concatenate(q_segs, axis=1).astype(BF16),
                            jnp.concatenate(k_segs, axis=1).astype(BF16))
            o = o + _dot(a_off.astype(BF16), v_bf)
            yield

            diag = []
            for blk in range(CHUNK // SUB):
                base = blk * SUB
                qa, qb = q[base:base + 8], q[base + 8:base + SUB]
                ba, bb = b[base:base + 8], b[base + 8:base + SUB]
                oa = jnp.zeros((8, HEAD_W), F32)
                ob = jnp.zeros((8, HEAD_W), F32)
                for s in range(SUB):
                    ks = k[base + s:base + s + 1]
                    bs = b[base + s:base + s + 1]
                    vs = v[base + s:base + s + 1]
                    if s < 8:
                        e = jnp.exp2(ba - bs)
                        if s:
                            e = jnp.where(sub_row >= s, e, 0.0)
                        oa = oa + jnp.sum(qa * ks * e, axis=-1, keepdims=True) * vs
                        ob = ob + jnp.sum(qb * ks * jnp.exp2(bb - bs), axis=-1, keepdims=True) * vs
                    else:
                        e = jnp.exp2(bb - bs)
                        if s > 8:
                            e = jnp.where(sub_row >= s - 8, e, 0.0)
                        ob = ob + jnp.sum(qb * ks * e, axis=-1, keepdims=True) * vs
                diag += [oa, ob]
                yield
            o = o + jnp.concatenate(diag, axis=0)

            k_dec = (k * jnp.exp2(b_last - b)).astype(BF16)
            st_ref[h] = st * jnp.exp2(b_last) + _dot_tn(v_bf, k_dec)

            out = _rms(o, gn_ref[...]) * sg_ref[0, pl.ds(r0, CHUNK), hs]
            o_ref[0, pl.ds(r0, CHUNK), hs] = out.astype(o_ref.dtype)
            yield

        _run_skewed([head(h) for h in range(N_HEADS)])
        return carry

    lax.fori_loop(0, n_chunks, chunk_body, 0)

    @pl.when(t == pl.num_programs(1) - 1)
    def _():
        for h in range(N_HEADS):
            sfin_ref[0, h] = st_ref[h].T


def _hgrn(qh, kk, lf, hi, sg, gnorm_g, s0, tb):
    B, T, _ = qh.shape
    tspec = pl.BlockSpec((1, tb, GROUP_W), lambda b, t: (b, t, 0))
    sspec = pl.BlockSpec((1, N_HEADS, HEAD_W, HEAD_W), lambda b, t: (b, 0, 0, 0))
    return pl.pallas_call(
        functools.partial(_hgrn_kernel, tb=tb),
        grid=(B, T // tb),
        in_specs=[tspec] * 5 + [pl.BlockSpec((1, HEAD_W), lambda b, t: (0, 0)), sspec],
        out_specs=[tspec, sspec],
        out_shape=[jax.ShapeDtypeStruct((B, T, GROUP_W), BF16),
                   jax.ShapeDtypeStruct((B, N_HEADS, HEAD_W, HEAD_W), F32)],
        scratch_shapes=[pltpu.VMEM((N_HEADS, HEAD_W, HEAD_W), F32), pltpu.VMEM((tb, GROUP_W), F32)],
        compiler_params=_cparams(("parallel", "arbitrary")),
        name="hgrn",
    )(qh, kk, lf, hi, sg, gnorm_g.reshape(1, HEAD_W), s0)


def _post_kernel(x_ref, oda_ref, ohg_ref, mk_ref, mv_ref, wout_ref, lnm_ref, wq_ref, wo_ref,
                 lnf_ref, wr_ref, br_ref, x2_ref, h3_ref, route_ref, om_ref, *, nb, tm, split):
    gt = tm // split
    groups = [_post_group(slice(g * gt, (g + 1) * gt), g * nb * gt, x_ref, oda_ref, ohg_ref,
                          mk_ref, mv_ref, wout_ref, lnm_ref, wq_ref, wo_ref, lnf_ref, wr_ref,
                          br_ref, x2_ref, h3_ref, route_ref, om_ref, nb=nb, gt=gt)
              for g in range(split)]
    _run_skewed(groups)


def _post_group(ts, om0, x_ref, oda_ref, ohg_ref, mk_ref, mv_ref, wout_ref, lnm_ref, wq_ref, wo_ref,
                lnf_ref, wr_ref, br_ref, x2_ref, h3_ref, route_ref, om_ref, *, nb, gt):
    rows = nb * gt
    x = x_ref[:, ts, :].reshape(rows, D_MODEL)
    mixed = (_dot(oda_ref[:, ts, :].reshape(rows, GROUP_W), wout_ref[:GROUP_W, :])
             + _dot(ohg_ref[:, ts, :].reshape(rows, GROUP_W), wout_ref[GROUP_W:, :]))
    x1 = x + mixed
    yield

    hm = _rms(x1, lnm_ref[...]).astype(BF16)
    yield
    qm = (_dot(hm, wq_ref[...]) * MEM_SCALE).astype(BF16)
    yield
    for b in range(nb):
        if len(mk_ref.shape) == 4:
            mk_b = mk_ref[b].reshape(N_MEM, D_MODEL).astype(BF16)
            mv_b = mv_ref[b].reshape(N_MEM, D_MODEL).astype(BF16)
        for h in range(MEM_HEADS):
            hs = slice(h * MEM_HEAD_DIM, (h + 1) * MEM_HEAD_DIM)
            if len(mk_ref.shape) == 4:
                mk, mv = mk_b[:, hs], mv_b[:, hs]
            else:
                mk, mv = mk_ref[b, :, hs], mv_ref[b, :, hs]
            s = _dot_nt(qm[b * gt:(b + 1) * gt, hs], mk.astype(BF16))
            e = jnp.exp(s - jnp.max(s, axis=-1, keepdims=True))
            p = e / jnp.sum(e, axis=-1, keepdims=True)
            om_ref[om0 + b * gt:om0 + (b + 1) * gt, hs] = _dot(
                p.astype(BF16), mv.astype(BF16)).astype(BF16)
    yield
    x2 = x1 + _dot(om_ref[om0:om0 + rows, :], wo_ref[...])
    x2_ref[:, ts, :] = x2.reshape(nb, gt, D_MODEL)
    yield

    h3 = _rms(x2, lnf_ref[...])
    h3_ref[:, ts] = h3.reshape(nb, gt, N_SLAB, HEAD_W)
    yield

    r = _dot(h3.astype(BF16), wr_ref[...]) + br_ref[...]
    lane = lax.broadcasted_iota(jnp.int32, r.shape, 1).astype(F32)
    big = float(4 * HEAD_W)
    g_mask = lane < N_GROUPS
    gl = jnp.where(g_mask, r, NEG_INF)
    g_max = jnp.max(gl, axis=-1, keepdims=True)
    g_idx = jnp.min(jnp.where(gl == g_max, lane, big), axis=-1, keepdims=True)
    g_w = 1.0 / jnp.sum(jnp.where(g_mask, jnp.exp(r - g_max), 0.0), axis=-1, keepdims=True)
    e_lo = N_GROUPS + EXPERTS_PER_GROUP * g_idx
    el = jnp.where((lane >= e_lo) & (lane < e_lo + EXPERTS_PER_GROUP), r, NEG_INF)
    v1 = jnp.max(el, axis=-1, keepdims=True)
    i1 = jnp.min(jnp.where(el == v1, lane, big), axis=-1, keepdims=True)
    el2 = jnp.where(lane == i1, NEG_INF, el)
    v2 = jnp.max(el2, axis=-1, keepdims=True)
    i2 = jnp.min(jnp.where(el2 == v2, lane, big), axis=-1, keepdims=True)
    t = jnp.exp(v2 - v1)
    p1 = 1.0 / (1.0 + t)
    rec = jnp.where(lane == 0.0, i1 - N_GROUPS,
          jnp.where(lane == 1.0, i2 - N_GROUPS,
          jnp.where(lane == 2.0, p1 * g_w,
          jnp.where(lane == 3.0, t * p1 * g_w, 0.0))))
    route_ref[:, ts, :] = rec[:, :ROUTE_W].reshape(nb, gt, ROUTE_W)


def _post_mix(x, oda, ohg, mem_k, mem_v, w_out, ln_mem, w_q, w_o, ln_ffn, w_r, b_r, nb, tm, split):
    B, T, _ = x.shape
    tok = lambda w: pl.BlockSpec((nb, tm, w), lambda b, t: (b, t, 0))
    if mem_k.ndim == 4:
        memspec = pl.BlockSpec((nb, N_MEM, MEM_HEADS, MEM_HEAD_DIM), lambda b, t: (b, 0, 0, 0))
    else:
        memspec = pl.BlockSpec((nb, N_MEM, D_MODEL), lambda b, t: (b, 0, 0))
    fix = lambda s: pl.BlockSpec(s, lambda b, t: (0, 0))
    return pl.pallas_call(
        functools.partial(_post_kernel, nb=nb, tm=tm, split=split),
        grid=(B // nb, T // tm),
        in_specs=[tok(D_MODEL), tok(GROUP_W), tok(GROUP_W), memspec, memspec,
                  fix((D_MODEL, D_MODEL)), fix((1, D_MODEL)), fix((D_MODEL, D_MODEL)),
                  fix((D_MODEL, D_MODEL)), fix((1, D_MODEL)), fix((D_MODEL, HEAD_W)),
                  fix((1, HEAD_W))],
        out_specs=[tok(D_MODEL),
                   pl.BlockSpec((nb, tm, N_SLAB, HEAD_W), lambda b, t: (b, t, 0, 0)),
                   tok(ROUTE_W)],
        out_shape=[jax.ShapeDtypeStruct((B, T, D_MODEL), F32),
                   jax.ShapeDtypeStruct((B, T, N_SLAB, HEAD_W), F32),
                   jax.ShapeDtypeStruct((B, T, ROUTE_W), F32)],
        scratch_shapes=[pltpu.VMEM((nb * tm, D_MODEL), BF16)],
        compiler_params=_cparams(("parallel", "parallel")),
        name="post_mix",
    )(x, oda, ohg, mem_k, mem_v, w_out, ln_mem.reshape(1, D_MODEL), w_q, w_o,
      ln_ffn.reshape(1, D_MODEL), w_r, b_r)


def _experts_kernel(blk_e_ref, n_used_ref, src_next_ref, dst_prev_ref, src0_ref, dst_ref, h_hbm,
                    wg_ref, wu_ref, wd_ref, y_hbm, xbuf, ybuf, gsem, ssem, *, bm, m_tot):
    del blk_e_ref
    i = pl.program_id(0)
    n_used = n_used_ref[0]
    slot = i % 2

    def gather_row(tab_ref, r, s, priority=0):
        pltpu.make_async_copy(h_hbm.at[tab_ref[0, 0, r]], xbuf.at[s, r],
                              gsem.at[s]).start(priority=priority)

    def scatter_row(tab_ref, r, s, priority=0):
        pltpu.make_async_copy(ybuf.at[s, r], y_hbm.at[tab_ref[0, 0, r]],
                              ssem.at[s]).start(priority=priority)

    def wait_block(buf, sem, s):
        pltpu.make_async_copy(buf.at[s], buf.at[s], sem.at[s]).wait()

    def spare_fill(s):
        return pltpu.make_async_copy(ybuf.at[s], y_hbm.at[pl.ds(m_tot + s * bm, bm)], ssem.at[s])

    @pl.when(i == 0)
    def _():
        ybuf[...] = jnp.zeros(ybuf.shape, F32)
        spare_fill(0).start()
        spare_fill(1).start()
        spare_fill(1).wait()

        def body(r, c):
            gather_row(src0_ref, r, 0)
            return c
        lax.fori_loop(0, bm, body, 0)

    @pl.when(i < n_used)
    def _():
        wait_block(xbuf, gsem, slot)
        wait_block(ybuf, ssem, slot)
        x = xbuf[slot].reshape(bm, D_MODEL).astype(BF16)
        hg = _dot(x, wg_ref[0].astype(BF16))
        hu = _dot(x, wu_ref[0].astype(BF16))
        hb = (hg * jax.nn.sigmoid(hg) * hu).astype(BF16)
        ybuf[slot] = _dot(hb, wd_ref[0].astype(BF16)).reshape(bm, N_SLAB, HEAD_W)
        for r in range(bm):
            gather_row(src_next_ref, r, 1 - slot, priority=0)
            scatter_row(dst_prev_ref, r, 1 - slot, priority=1)

    @pl.when(i == n_used - 1)
    def _():
        def body(r, c):
            scatter_row(dst_ref, r, slot)
            return c
        lax.fori_loop(0, bm, body, 0)
        wait_block(xbuf, gsem, 1 - slot)
        wait_block(ybuf, ssem, 1 - slot)
        wait_block(ybuf, ssem, slot)


def _experts(h3, blk_e, n_used, src_tab, dst_tab, dst_prev_tab, wg, wu, wd, bm):
    n_blocks = src_tab.shape[0]
    m_tot = TOP_K * h3.shape[0]
    tab = lambda f: pl.BlockSpec((1, 1, bm), f, memory_space=pltpu.SMEM)
    cur = lambda i, e, n: (i, 0, 0)
    nxt = lambda i, e, n: (jnp.minimum(i + 1, n_blocks - 1), 0, 0)
    first = lambda i, e, n: (0, 0, 0)
    wspec = lambda a, b: pl.BlockSpec((1, a, b), lambda i, e, n: (e[i], 0, 0))
    grid_spec = pltpu.PrefetchScalarGridSpec(
        num_scalar_prefetch=2,
        grid=(n_blocks,),
        in_specs=[tab(nxt), tab(cur), tab(first), tab(cur), pl.BlockSpec(memory_space=pl.ANY),
                  wspec(D_MODEL, EXPERT_FF), wspec(D_MODEL, EXPERT_FF), wspec(EXPERT_FF, D_MODEL)],
        out_specs=pl.BlockSpec(memory_space=pl.ANY),
        scratch_shapes=[pltpu.VMEM((2, bm, N_SLAB, HEAD_W), F32), pltpu.VMEM((2, bm, N_SLAB, HEAD_W), F32),
                        pltpu.SemaphoreType.DMA((2,)), pltpu.SemaphoreType.DMA((2,))],
    )
    return pl.pallas_call(
        functools.partial(_experts_kernel, bm=bm, m_tot=m_tot),
        grid_spec=grid_spec,
        out_shape=jax.ShapeDtypeStruct((m_tot + 2 * bm, N_SLAB, HEAD_W), F32),
        compiler_params=_cparams(("arbitrary",)),
        name="experts",
    )(blk_e, n_used, src_tab, dst_prev_tab, src_tab, dst_tab, h3, wg, wu, wd)


def _combine_kernel(x_ref, y0_ref, y1_ref, route_ref, g_ref, o_ref):
    rt = route_ref[...]
    tm = x_ref.shape[0]
    y0 = y0_ref[...].reshape(tm, D_MODEL)
    y1 = y1_ref[...].reshape(tm, D_MODEL)
    y = x_ref[...] + rt[:, 2:3] * y0 + rt[:, 3:4] * y1
    o_ref[...] = _rms(y, g_ref[...])


def _combine(x2, y, route, final_g, tm):
    n = x2.shape[0]
    nt = n // tm
    return pl.pallas_call(
        _combine_kernel,
        grid=(nt,),
        in_specs=[pl.BlockSpec((tm, D_MODEL), lambda i: (i, 0)),
                  pl.BlockSpec((tm, N_SLAB, HEAD_W), lambda i: (i, 0, 0)),
                  pl.BlockSpec((tm, N_SLAB, HEAD_W), lambda i: (i + nt, 0, 0)),
                  pl.BlockSpec((tm, ROUTE_W), lambda i: (i, 0)),
                  pl.BlockSpec((1, D_MODEL), lambda i: (0, 0))],
        out_specs=pl.BlockSpec((tm, D_MODEL), lambda i: (i, 0)),
        out_shape=jax.ShapeDtypeStruct((n, D_MODEL), F32),
        compiler_params=_cparams(("parallel",)),
        name="combine",
    )(x2, y, y, route, final_g.reshape(1, D_MODEL))


def _routing_tables(route, bm):
    n = route.shape[0]
    m_tot = TOP_K * n
    flat_e = jnp.concatenate([route[:, 0], route[:, 1]]).astype(jnp.int32)
    experts = jnp.arange(N_EXPERTS, dtype=jnp.int32)
    counts = jnp.sum(flat_e[:, None] == experts[None, :], axis=0, dtype=jnp.int32)
    n_blk_e = (counts + bm - 1) // bm
    blk_end = jnp.cumsum(n_blk_e)
    n_blocks = m_tot // bm + N_EXPERTS
    blk = jnp.arange(n_blocks, dtype=jnp.int32)
    blk_e = jnp.minimum(jnp.sum(blk[:, None] >= blk_end[None, :], axis=1, dtype=jnp.int32),
                        N_EXPERTS - 1)
    r = jnp.arange(bm, dtype=jnp.int32)[None, :]
    n_pad = (n_blk_e * bm - counts)[:, None]
    pad_keys = jnp.where(r < n_pad, 2 * experts[:, None] + 1, 2 * N_EXPERTS + 1)
    payload_bits = max(m_tot.bit_length(), 1)
    assert (2 * N_EXPERTS + 2) << payload_bits < 2 ** 31
    packed = jnp.concatenate([(2 * flat_e << payload_bits) + jnp.arange(1, m_tot + 1, dtype=jnp.int32),
                              pad_keys.reshape(-1) << payload_bits])
    m = ((lax.sort(packed) & ((1 << payload_bits) - 1)) - 1).reshape(n_blocks, bm)
    real = m >= 0
    spare = m_tot + (blk[:, None] % 2) * bm + r
    src = jnp.where(real, m % n, 0)
    dst = jnp.where(real, m, spare)
    dst_prev = jnp.concatenate([m_tot + bm + r, dst[:-1]], axis=0)
    shape = (n_blocks, 1, bm)
    return (blk_e, blk_end[N_EXPERTS - 1:].astype(jnp.int32), src.reshape(shape), dst.reshape(shape),
            dst_prev.reshape(shape))


def _layer(x, past_k, past_v, s0, mem_k, mem_v, p, cfg):
    B, T, _ = x.shape
    n = B * T
    (q, kf, vf, kb, vt, qh, lf, kk, hi, sg) = _in_proj(
        x, p["ln_mix"], p["w_in"], p["lb"], cfg["nb"], cfg["tm_proj"], cfg["split_proj"])
    if past_k is None:
        oda = _diff_attn(q, kb, vt, p["slopes"], p["lam"], p["da_subln"], cfg["tq"], cfg["tk"])
    else:
        oda = _diff_attn_step(q, kb, vt, past_k, past_v, p["slopes"], p["lam"], p["da_subln"])
    ohg, s_new = _hgrn(qh, kk, lf, hi, sg, p["hg_gnorm"], s0, cfg["tb"])
    x2, h3, route = _post_mix(x, oda, ohg, mem_k, mem_v, p["w_out"], p["ln_mem"], p["w_mem_q"],
                              p["w_mem_o"], p["ln_ffn"], p["w_r"], p["b_r"], cfg["nb"], cfg["tm_post"],
                              cfg["split"])
    route2 = route.reshape(n, ROUTE_W)
    blk_e, n_used, src_tab, dst_tab, dst_prev_tab = _routing_tables(route2, cfg["bm"])
    y = _experts(h3.reshape(n, N_SLAB, HEAD_W), blk_e, n_used, src_tab, dst_tab, dst_prev_tab,
                 p["e_gate"], p["e_up"], p["e_down"], cfg["bm"])
    out = _combine(x2.reshape(n, D_MODEL), y, route2, p["final_g"], cfg["tm_in"])
    return out.reshape(B, T, D_MODEL), kf[None], vf[None], s_new[None]


PROMPT_CFG = dict(tm_in=256, tq=512, tk=256, tb=256, nb=1, tm_proj=512, split_proj=2, tm_post=1024, split=4,
                  bm=512)
SAMPLE_CFG = dict(tm_in=256, tq=64, tk=64, tb=64, nb=4, tm_proj=64, split_proj=1, tm_post=64, split=1,
                  bm=256)


def kernel(x_prompt, x_sample, mem_prompt, cache_diff_k, cache_diff_v, state_hgrn, cache_mem_k, cache_mem_v, ln_mix_g, w_in, da_lambda, da_subln_g, hg_lb_logits, hg_gnorm_g, w_out, ln_mem_g, mem_norm_g, w_mem_q, w_mem_k, w_mem_v, w_mem_o, ln_ffn_g, router_group_w, router_group_b, router_expert_w, router_expert_b, exp_w_gate, exp_w_up, exp_w_down, final_norm_g):
    assert w_in.shape[0] == 1, "single-layer configuration"
    lb_all = jnp.cumsum(jax.nn.softmax(hg_lb_logits.astype(F32), axis=0), axis=0)
    lp = da_lambda[0].astype(F32)
    lam = jnp.exp(jnp.sum(lp[0] * lp[1])) - jnp.exp(jnp.sum(lp[2] * lp[3])) + LAM_INIT
    w_r = jnp.zeros((D_MODEL, HEAD_W), F32)
    w_r = w_r.at[:, :N_GROUPS].set(router_group_w[0]).at[:, N_GROUPS:N_GROUPS + N_EXPERTS].set(router_expert_w[0])
    b_r = jnp.zeros((1, HEAD_W), F32)
    b_r = b_r.at[0, :N_GROUPS].set(router_group_b[0]).at[0, N_GROUPS:N_GROUPS + N_EXPERTS].set(router_expert_b[0])
    p = {
        "ln_mix": ln_mix_g[0], "w_in": w_in[0].astype(BF16), "lb": lb_all[0],
        "slopes": jnp.exp2(-8.0 * jnp.arange(1, N_HEADS + 1, dtype=F32) / N_HEADS),
        "lam": lam.reshape(1), "da_subln": da_subln_g[0], "hg_gnorm": hg_gnorm_g[0],
        "w_out": w_out[0].astype(BF16), "ln_mem": ln_mem_g[0], "w_mem_q": w_mem_q[0].astype(BF16),
        "w_mem_o": w_mem_o[0].astype(BF16), "ln_ffn": ln_ffn_g[0], "w_r": w_r.astype(BF16), "b_r": b_r,
        "e_gate": exp_w_gate[0], "e_up": exp_w_up[0],
        "e_down": exp_w_down[0], "final_g": final_norm_g,
    }
    Bp, Tp, _ = x_prompt.shape
    Bs, Ts, _ = x_sample.shape

    mkf, mvf, mkb, mvb = _memory_kv(mem_prompt, mem_norm_g[0], w_mem_k[0].astype(BF16),
                                    w_mem_v[0].astype(BF16))
    zero_state = jnp.zeros((Bp, N_HEADS, HEAD_W, HEAD_W), F32)
    yp, kp, vp, sp = _layer(x_prompt, None, None, zero_state, mkb, mvb, p, PROMPT_CFG)
    ys, ks, vs, ss = _layer(
        x_sample, cache_diff_k[0], cache_diff_v[0], state_hgrn[0], cache_mem_k[0], cache_mem_v[0],
        p, SAMPLE_CFG)
    return (yp, ys, kp, vp, sp, mkf[None], mvf[None], ks, vs, ss)
```

```python
import functools
import math

import jax
import jax.numpy as jnp
from jax import lax
from jax.experimental import pallas as pl
from jax.experimental.pallas import tpu as pltpu

F32 = jnp.float32
BF16 = jnp.bfloat16

D_MODEL = 1024
EPS = 1e-5
CHUNK = 64
N_HEADS = 4
HEAD_W = 128
QK_DIM = 64
GROUP_W = N_HEADS * HEAD_W
N_PROJ = 7
DA_SCALE = QK_DIM ** -0.5
LOG2E = 1.4426950408889634
LAM_INIT = 0.8 - 0.6 * math.exp(-0.3 * 0)
N_MEM = 256
MEM_HEADS = 4
MEM_HEAD_DIM = D_MODEL // MEM_HEADS
MEM_SCALE = MEM_HEAD_DIM ** -0.5
N_GROUPS = 4
EXPERTS_PER_GROUP = 8
N_EXPERTS = N_GROUPS * EXPERTS_PER_GROUP
TOP_K = 2
EXPERT_FF = 512
ROUTE_W = 8
SUB = 16
SUB_KEYS = 256
N_SCORE_SLOTS = 3
N_SLAB = D_MODEL // HEAD_W
NEG_INF = float("-inf")

VMEM_LIMIT = 48 * 1024 * 1024


def _cparams(sem):
    return pltpu.CompilerParams(dimension_semantics=sem, vmem_limit_bytes=VMEM_LIMIT)


def _rms(x, g):
    return x * lax.rsqrt(jnp.mean(x * x, axis=-1, keepdims=True) + EPS) * g


def _dot(a, b):
    return jnp.dot(a, b, preferred_element_type=F32)


def _dot_nt(a, b):
    return lax.dot_general(a, b, (((1,), (1,)), ((), ())), preferred_element_type=F32)


def _dot_tn(a, b):
    return lax.dot_general(a, b, (((0,), (0,)), ((), ())), preferred_element_type=F32)


def _run_skewed(gens):
    waiting, live = list(gens), []
    while waiting or live:
        if waiting:
            live.append(waiting.pop(0))
        for gen in list(live):
            if next(gen, "done") == "done":
                live.remove(gen)


def _memkv_kernel(m_ref, g_ref, wk_ref, wv_ref, kf_ref, vf_ref, kb_ref, vb_ref):
    mn = _rms(m_ref[0], g_ref[...]).astype(BF16)
    k = _dot(mn, wk_ref[...])
    v = _dot(mn, wv_ref[...])
    kf_ref[0] = k.reshape(N_MEM, MEM_HEADS, MEM_HEAD_DIM)
    vf_ref[0] = v.reshape(N_MEM, MEM_HEADS, MEM_HEAD_DIM)
    kb_ref[0] = k.astype(BF16)
    vb_ref[0] = v.astype(BF16)


def _memory_kv(mem, g, wk, wv):
    B = mem.shape[0]
    blk = pl.BlockSpec((1, N_MEM, D_MODEL), lambda b: (b, 0, 0))
    hblk = pl.BlockSpec((1, N_MEM, MEM_HEADS, MEM_HEAD_DIM), lambda b: (b, 0, 0, 0))
    wspec = pl.BlockSpec((D_MODEL, D_MODEL), lambda b: (0, 0))
    return pl.pallas_call(
        _memkv_kernel,
        grid=(B,),
        in_specs=[blk, pl.BlockSpec((1, D_MODEL), lambda b: (0, 0)), wspec, wspec],
        out_specs=[hblk, hblk, blk, blk],
        out_shape=[jax.ShapeDtypeStruct((B, N_MEM, MEM_HEADS, MEM_HEAD_DIM), F32)] * 2
                  + [jax.ShapeDtypeStruct(mem.shape, BF16)] * 2,
        compiler_params=_cparams(("parallel",)),
        name="memory_kv",
    )(mem, g.reshape(1, D_MODEL), wk, wv)


def _in_proj_kernel(x_ref, g_ref, w_ref, lb_ref, q_ref, kf_ref, vf_ref, kb_ref, vt_ref,
                    qh_ref, lf_ref, kk_ref, hi_ref, sg_ref, *, nb, tm, split):
    gt = tm // split
    out_refs = (q_ref, kf_ref, vf_ref, kb_ref, vt_ref, qh_ref, lf_ref, kk_ref, hi_ref, sg_ref)
    _run_skewed([_in_proj_group(slice(g * gt, (g + 1) * gt), x_ref, g_ref, w_ref, lb_ref, out_refs,
                                nb=nb, gt=gt) for g in range(split)])


def _in_proj_group(ts, x_ref, g_ref, w_ref, lb_ref, out_refs, *, nb, gt):
    q_ref, kf_ref, vf_ref, kb_ref, vt_ref, qh_ref, lf_ref, kk_ref, hi_ref, sg_ref = out_refs
    rows = nb * gt
    h = _rms(x_ref[:, ts, :].reshape(rows, D_MODEL), g_ref[...]).astype(BF16)
    yield

    def proj(i):
        return _dot(h, w_ref[:, i * GROUP_W:(i + 1) * GROUP_W])

    def put(ref, val):
        ref[:, ts, :] = val.reshape(nb, gt, GROUP_W).astype(ref.dtype)

    def put_heads(ref, val):
        ref[:, ts] = val.reshape(nb, gt, N_HEADS, HEAD_W)

    put(q_ref, proj(0) * (DA_SCALE * LOG2E))
    yield
    dk = proj(1)
    put_heads(kf_ref, dk)
    put(kb_ref, dk)
    yield
    dv = proj(2)
    put_heads(vf_ref, dv)
    for b in range(nb):
        vt_ref[b, :, ts] = dv[b * gt:(b + 1) * gt].T.astype(BF16)
    yield
    hq = proj(3)
    put(qh_ref, hq * jax.nn.sigmoid(hq))
    yield
    hf = proj(4)
    lb = lb_ref[...]
    put(lf_ref, jnp.log(lb + (1.0 - lb) * jax.nn.sigmoid(hf)))
    put(kk_ref, (1.0 - lb) * jax.nn.sigmoid(-hf))
    yield
    put(hi_ref, proj(5))
    yield
    hg = proj(6)
    put(sg_ref, hg * jax.nn.sigmoid(hg))
    yield


def _in_proj(x, g, w_bf, lb, nb, tm, split):
    B, T, _ = x.shape
    tok = lambda w: pl.BlockSpec((nb, tm, w), lambda b, t: (b, t, 0))
    fix = lambda s: pl.BlockSpec(s, lambda b, t: (0, 0))
    f32o = jax.ShapeDtypeStruct((B, T, GROUP_W), F32)
    bfo = jax.ShapeDtypeStruct((B, T, GROUP_W), BF16)
    o_spec = tok(GROUP_W)
    ho = jax.ShapeDtypeStruct((B, T, N_HEADS, HEAD_W), F32)
    h_spec = pl.BlockSpec((nb, tm, N_HEADS, HEAD_W), lambda b, t: (b, t, 0, 0))
    vt_spec = pl.BlockSpec((nb, GROUP_W, tm), lambda b, t: (b, 0, t))
    return pl.pallas_call(
        functools.partial(_in_proj_kernel, nb=nb, tm=tm, split=split),
        grid=(B // nb, T // tm),
        in_specs=[tok(D_MODEL), fix((1, D_MODEL)), fix((D_MODEL, N_PROJ * GROUP_W)), fix((1, GROUP_W))],
        out_specs=[o_spec, h_spec, h_spec, o_spec, vt_spec] + [o_spec] * 5,
        out_shape=[bfo, ho, ho, bfo, jax.ShapeDtypeStruct((B, GROUP_W, T), BF16),
                   bfo, f32o, bfo, bfo, bfo],
        compiler_params=_cparams(("parallel", "parallel")),
        name="in_proj",
    )(x, g.reshape(1, D_MODEL), w_bf, lb.reshape(1, GROUP_W))


def _attn_kernel(slope_ref, lam_ref, q_ref, k_ref, vt_ref, g_ref, o_ref,
                 m_ref, l_ref, acc_ref, base_ref, t_ref, mb_ref, *, tq, tk):
    h = pl.program_id(1)
    qi = pl.program_id(2)
    slope2 = slope_ref[h] * LOG2E
    qt = q_ref[0].astype(F32).T.astype(BF16)
    zeros = jnp.zeros((QK_DIM, tq), BF16)
    qt_maps = (jnp.concatenate([qt[:QK_DIM], zeros], axis=0),
               jnp.concatenate([zeros, qt[QK_DIM:]], axis=0))
    m_ref[...] = jnp.full(m_ref.shape, NEG_INF, F32)
    l_ref[...] = jnp.zeros(l_ref.shape, F32)
    acc_ref[...] = jnp.zeros(acc_ref.shape, F32)
    q0 = qi * tq

    def lane_pad(x, lo, fill):
        if lo == 0:
            return x
        return jnp.concatenate([jnp.full((x.shape[0], lo), fill, x.dtype), x], axis=1)


    def score_phase(slot, n_keys, get_k, get_bias, shift, q_lo=lambda u: 0):
        sub = min(n_keys, SUB_KEYS)
        mx = [None, None]
        for u in range(n_keys // sub):
            lo = q_lo(u)
            bias, visible = get_bias(u)
            k = get_k(u)
            for c in range(2):
                t = _dot(k, qt_maps[c][:, lo:]) + bias
                if visible is not None:
                    w = visible.shape[1]
                    head = jnp.where(visible, t[:, :w], NEG_INF)
                    t = head if w == t.shape[1] else jnp.concatenate([head, t[:, w:]], axis=1)
                t_ref[slot, c, u * sub:(u + 1) * sub, lo:] = t
                mu = lane_pad(jnp.max(t, axis=0, keepdims=True), lo, NEG_INF)
                mx[c] = mu if mx[c] is None else jnp.maximum(mx[c], mu)
            yield
        for c in range(2):
            mb_ref[slot, c] = mx[c] + shift
        yield

    def prob_phase(slot, n_keys, get_vt, shift, q_lo=lambda u: 0, pv_group=1):
        sub = min(n_keys, SUB_KEYS)
        n_sub = n_keys // sub
        off, alpha = [], []
        for c in range(2):
            m_old = m_ref[c]
            m_new = jnp.maximum(m_old, mb_ref[slot, c])
            off.append(m_new - shift)
            alpha.append(jnp.exp2(m_old - m_new))
            m_ref[c] = m_new
        yield
        lsum = [None, None]
        pv = [None, None]
        grp = pv_group if n_sub % pv_group == 0 else 1
        for u in range(0, n_sub, grp):
            lo = q_lo(u)
            assert all(q_lo(u + i) == lo for i in range(grp))
            parts = [get_vt(u + i) for i in range(grp)]
            vt = parts[0] if grp == 1 else jnp.concatenate(parts, axis=1)
            for c in range(2):
                p = jnp.exp2(t_ref[slot, c, u * sub:(u + grp) * sub, lo:] - off[c][:, lo:])
                ls = lane_pad(jnp.sum(p, axis=0, keepdims=True), lo, 0.0)
                pu = lane_pad(_dot(vt, p.astype(BF16)), lo, 0.0)
                lsum[c] = ls if lsum[c] is None else lsum[c] + ls
                pv[c] = pu if pv[c] is None else pv[c] + pu
            yield
        for c in range(2):
            l_ref[c] = alpha[c] * l_ref[c] + lsum[c]
            acc_ref[c] = alpha[c] * acc_ref[c] + pv[c]
        yield

    def run(*gens):
        live = list(gens)
        while live:
            for gen in list(live):
                if next(gen, "done") == "done":
                    live.remove(gen)

    def update(n_keys, get_k, get_vt, get_bias, shift, q_lo=lambda u: 0, pv_group=1):
        run(score_phase(0, n_keys, get_k, get_bias, shift, q_lo))
        run(prob_phase(0, n_keys, get_vt, shift, q_lo, pv_group))

    def rel_bias(n_keys, key0):
        key = lax.broadcasted_iota(jnp.int32, (n_keys, tq), 0) + key0
        qry = lax.broadcasted_iota(jnp.int32, (n_keys, tq), 1)
        return (qry - key).astype(F32) * (-slope2)

    def block_shift(k0):
        return (q0 - k0).astype(F32) * (-slope2)

    n_q = k_ref.shape[1] // tq
    if n_q > 1 or tq > SUB_KEYS:
        @pl.when(qi == 0)
        def _():
            base_ref[...] = rel_bias(base_ref.shape[0], 0)

    kd = 0 if n_q == 1 else pl.multiple_of(qi * tq, tq)
    sub_d = min(tq, SUB_KEYS)
    key = lax.broadcasted_iota(jnp.int32, (sub_d, sub_d), 0)
    qry = lax.broadcasted_iota(jnp.int32, (sub_d, sub_d), 1)
    strip_bias = jnp.abs(qry - key).astype(F32) * (-slope2)
    strip_visible = (key // CHUNK) <= (qry // CHUNK)

    def diag_bias(u):
        lo = (u + 1) * sub_d
        if lo == tq:
            return strip_bias, strip_visible
        later = base_ref[u * sub_d:(u + 1) * sub_d, lo:]
        return jnp.concatenate([strip_bias, later], axis=1), strip_visible

    diag_k = lambda u: k_ref[0, pl.ds(kd + u * sub_d, sub_d), :]
    diag_vt = lambda u: vt_ref[0, :, pl.ds(kd + u * sub_d, sub_d)]
    diag_lo = lambda u: u * sub_d

    if n_q == 1:
        update(tq, diag_k, diag_vt, diag_bias, 0.0, q_lo=diag_lo)
    else:
        sub_k = min(tk, SUB_KEYS)
        n_prev = qi * (tq // tk)
        assert (tq // tk) % 2 == 0, "earlier blocks are consumed in pairs"

        def prev_scores(slot, j):
            k0 = pl.multiple_of(j * tk, tk)
            return score_phase(slot, tk, lambda u: k_ref[0, pl.ds(k0 + u * sub_k, sub_k), :],
                               lambda u: (base_ref[u * sub_k:(u + 1) * sub_k, :], None),
                               block_shift(k0))

        def prev_probs(slot, j):
            k0 = pl.multiple_of(j * tk, tk)
            return prob_phase(slot, tk, lambda u: vt_ref[0, :, pl.ds(k0 + u * sub_k, sub_k)],
                              block_shift(k0), pv_group=2)

        run(score_phase(0, tq, diag_k, diag_bias, 0.0, diag_lo))
        run(prob_phase(0, tq, diag_vt, 0.0, diag_lo), prev_scores(1, 0))

        def pair_body(i, carry):
            j = 2 * i
            run(prev_probs(1, j), prev_scores(2, j + 1))
            run(prev_probs(2, j + 1), prev_scores(1, jnp.minimum(j + 2, n_prev - 1)))
            return carry
        lax.fori_loop(0, n_prev // 2, pair_body, 0)

    lam = lam_ref[0]
    ot = acc_ref[0] / l_ref[0] - lam * (acc_ref[1] / l_ref[1])
    ot = ot * lax.rsqrt(jnp.mean(ot * ot, axis=0, keepdims=True) + EPS)
    o_ref[0] = (ot.T * g_ref[...] * (1.0 - LAM_INIT)).astype(o_ref.dtype)


def _diff_attn(q, k, vt, slopes, lam, subln_g, tq, tk):
    B, T, _ = q.shape
    smem = pl.BlockSpec(memory_space=pltpu.SMEM)
    qspec = pl.BlockSpec((1, tq, HEAD_W), lambda b, h, i: (b, i, h))
    return pl.pallas_call(
        functools.partial(_attn_kernel, tq=tq, tk=tk),
        grid=(B, N_HEADS, T // tq),
        in_specs=[smem, smem, qspec,
                  pl.BlockSpec((1, T, HEAD_W), lambda b, h, i: (b, 0, h)),
                  pl.BlockSpec((1, HEAD_W, T), lambda b, h, i: (b, h, 0)),
                  pl.BlockSpec((1, HEAD_W), lambda b, h, i: (0, 0))],
        out_specs=qspec,
        out_shape=jax.ShapeDtypeStruct((B, T, GROUP_W), BF16),
        scratch_shapes=[pltpu.VMEM((2, 1, tq), F32), pltpu.VMEM((2, 1, tq), F32),
                        pltpu.VMEM((2, HEAD_W, tq), F32), pltpu.VMEM((max(tk, tq), tq), F32),
                        pltpu.VMEM((N_SCORE_SLOTS, 2, max(tk, tq), tq), F32),
                        pltpu.VMEM((N_SCORE_SLOTS, 2, 1, tq), F32)],
        compiler_params=_cparams(("parallel", "parallel", "arbitrary")),
        name="diff_attn",
    )(slopes, lam, q, k, vt, subln_g.reshape(1, HEAD_W))


def _attn_step_kernel(slope_ref, lam_ref, q_ref, k_ref, vt_ref, pk_ref, pv_ref, g_ref, o_ref, *,
                      t_new, past_len):
    assert 2 * t_new <= HEAD_W and t_new % CHUNK == 0 and past_len % CHUNK == 0
    pk = pk_ref[0].reshape(past_len, GROUP_W)
    pv = pv_ref[0].reshape(past_len, GROUP_W)
    lam = lam_ref[0]
    qry2 = lax.broadcasted_iota(jnp.int32, (1, 2 * t_new), 1) % t_new
    key_p = lax.broadcasted_iota(jnp.int32, (past_len, 2 * t_new), 0)
    key_n = lax.broadcasted_iota(jnp.int32, (t_new, 2 * t_new), 0)
    dist_p = (qry2 + past_len - key_p).astype(F32)
    dist_n = jnp.abs(qry2 - key_n).astype(F32)
    visible = (key_n // CHUNK) <= (qry2 // CHUNK)
    zeros = jnp.zeros((QK_DIM, t_new), BF16)
    for h in range(N_HEADS):
        hs = slice(h * HEAD_W, (h + 1) * HEAD_W)
        slope2 = slope_ref[h] * LOG2E
        qt = q_ref[0, :, hs].astype(F32).T.astype(BF16)
        qt2 = jnp.concatenate([jnp.concatenate([qt[:QK_DIM], zeros], axis=0),
                               jnp.concatenate([zeros, qt[QK_DIM:]], axis=0)], axis=1)
        s_p = _dot(pk[:, hs].astype(BF16), qt2) - slope2 * dist_p
        s_n = jnp.where(visible, _dot(k_ref[0, :, hs], qt2) - slope2 * dist_n, NEG_INF)
        m = jnp.maximum(jnp.max(s_p, axis=0, keepdims=True), jnp.max(s_n, axis=0, keepdims=True))
        p_p = jnp.exp2(s_p - m)
        p_n = jnp.exp2(s_n - m)
        l = jnp.sum(p_p, axis=0, keepdims=True) + jnp.sum(p_n, axis=0, keepdims=True)
        acc = (_dot_tn(pv[:, hs].astype(BF16), p_p.astype(BF16))
               + _dot(vt_ref[0, hs, :], p_n.astype(BF16))) / l
        ot = acc[:, :t_new] - lam * acc[:, t_new:]
        ot = ot * lax.rsqrt(jnp.mean(ot * ot, axis=0, keepdims=True) + EPS)
        o_ref[0, :, hs] = (ot.T * g_ref[...] * (1.0 - LAM_INIT)).astype(o_ref.dtype)


def _diff_attn_step(q, k, vt, past_k, past_v, slopes, lam, subln_g):
    B, T, _ = q.shape
    past_len = past_k.shape[1]
    smem = pl.BlockSpec(memory_space=pltpu.SMEM)
    tok = pl.BlockSpec((1, T, GROUP_W), lambda b: (b, 0, 0))
    pspec = pl.BlockSpec((1, past_len, N_HEADS, HEAD_W), lambda b: (b, 0, 0, 0))
    return pl.pallas_call(
        functools.partial(_attn_step_kernel, t_new=T, past_len=past_len),
        grid=(B,),
        in_specs=[smem, smem, tok, tok, pl.BlockSpec((1, GROUP_W, T), lambda b: (b, 0, 0)),
                  pspec, pspec, pl.BlockSpec((1, HEAD_W), lambda b: (0, 0))],
        out_specs=tok,
        out_shape=jax.ShapeDtypeStruct((B, T, GROUP_W), BF16),
        compiler_params=_cparams(("parallel",)),
        name="diff_attn_step",
    )(slopes, lam, q, k, vt, past_k, past_v, subln_g.reshape(1, HEAD_W))


def _split3(x):
    hi = x.astype(BF16)
    r = x - hi.astype(F32)
    mid = r.astype(BF16)
    lo = (r - mid.astype(F32)).astype(BF16)
    return hi, mid, lo


def _hgrn_kernel(q_ref, k_ref, lf_ref, v_ref, sg_ref, gn_ref, s0_ref, o_ref, sfin_ref,
                 st_ref, b_ref, *, tb):
    t = pl.program_id(1)
    n_chunks = tb // CHUNK

    @pl.when(t == 0)
    def _():
        for h in range(N_HEADS):
            st_ref[h] = s0_ref[0, h].T

    row = lax.broadcasted_iota(jnp.int32, (tb, tb), 0)
    col = lax.broadcasted_iota(jnp.int32, (tb, tb), 1)
    tri = jnp.where((col <= row) & (col // CHUNK == row // CHUNK), 1.0, 0.0).astype(BF16)
    hi, mid, lo = _split3(lf_ref[0] * LOG2E)
    b_ref[...] = _dot(tri, hi) + _dot(tri, mid) + _dot(tri, lo)

    sub_row = lax.broadcasted_iota(jnp.int32, (8, HEAD_W), 0)

    def pad_rows(x, r0):
        parts = []
        if r0:
            parts.append(jnp.zeros((r0, HEAD_W), F32))
        parts.append(x)
        rest = CHUNK - r0 - x.shape[0]
        if rest:
            parts.append(jnp.zeros((rest, HEAD_W), F32))
        return jnp.concatenate(parts, axis=0) if len(parts) > 1 else x

    def chunk_body(c, carry):
        r0 = pl.multiple_of(c * CHUNK, CHUNK)
        def head(h):
            hs = slice(h * HEAD_W, (h + 1) * HEAD_W)
            q = q_ref[0, pl.ds(r0, CHUNK), hs].astype(F32)
            k = k_ref[0, pl.ds(r0, CHUNK), hs].astype(F32)
            v_bf = v_ref[0, pl.ds(r0, CHUNK), hs]
            v = v_bf.astype(F32)
            b = b_ref[pl.ds(r0, CHUNK), hs]
            b_last = b[CHUNK - 1:CHUNK, :]
            st = st_ref[h]

            o = _dot_nt((q * jnp.exp2(b)).astype(BF16), st.astype(BF16))

            q_segs, k_segs = [], []
            for i in range(1, CHUNK // SUB):
                lo_r, hi_r = i * SUB, (i + 1) * SUB
                ref_b = b[lo_r - 1:lo_r, :]
                q_segs.append(pad_rows(q[lo_r:hi_r] * jnp.exp2(b[lo_r:hi_r] - ref_b), lo_r))
                k_segs.append(pad_rows(k[:lo_r] * jnp.exp2(ref_b - b[:lo_r]), 0))
            a_off = _dot_nt(jnp.concatenate(q_segs, axis=1).astype(BF16),
                            jnp.concatenate(k_segs, axis=1).astype(BF16))
            o = o + _dot(a_off.astype(BF16), v_bf)
            yield

            diag = []
            for blk in range(CHUNK // SUB):
                base = blk * SUB
                qa, qb = q[base:base + 8], q[base + 8:base + SUB]
                ba, bb = b[base:base + 8], b[base + 8:base + SUB]
                oa = jnp.zeros((8, HEAD_W), F32)
                ob = jnp.zeros((8, HEAD_W), F32)
                for s in range(SUB):
                    ks = k[base + s:base + s + 1]
                    bs = b[base + s:base + s + 1]
                    vs = v[base + s:base + s + 1]
                    if s < 8:
                        e = jnp.exp2(ba - bs)
                        if s:
                            e = jnp.where(sub_row >= s, e, 0.0)
                        oa = oa + jnp.sum(qa * ks * e, axis=-1, keepdims=True) * vs
                        ob = ob + jnp.sum(qb * ks * jnp.exp2(bb - bs), axis=-1, keepdims=True) * vs
                    else:
                        e = jnp.exp2(bb - bs)
                        if s > 8:
                            e = jnp.where(sub_row >= s - 8, e, 0.0)
                        ob = ob + jnp.sum(qb * ks * e, axis=-1, keepdims=True) * vs
                diag += [oa, ob]
                yield
            o = o + jnp.concatenate(diag, axis=0)

            k_dec = (k * jnp.exp2(b_last - b)).astype(BF16)
            st_ref[h] = st * jnp.exp2(b_last) + _dot_tn(v_bf, k_dec)

            out = _rms(o, gn_ref[...]) * sg_ref[0, pl.ds(r0, CHUNK), hs]
            o_ref[0, pl.ds(r0, CHUNK), hs] = out.astype(o_ref.dtype)
            yield

        _run_skewed([head(h) for h in range(N_HEADS)])
        return carry

    lax.fori_loop(0, n_chunks, chunk_body, 0)

    @pl.when(t == pl.num_programs(1) - 1)
    def _():
        for h in range(N_HEADS):
            sfin_ref[0, h] = st_ref[h].T


def _hgrn(qh, kk, lf, hi, sg, gnorm_g, s0, tb):
    B, T, _ = qh.shape
    tspec = pl.BlockSpec((1, tb, GROUP_W), lambda b, t: (b, t, 0))
    sspec = pl.BlockSpec((1, N_HEADS, HEAD_W, HEAD_W), lambda b, t: (b, 0, 0, 0))
    return pl.pallas_call(
        functools.partial(_hgrn_kernel, tb=tb),
        grid=(B, T // tb),
        in_specs=[tspec] * 5 + [pl.BlockSpec((1, HEAD_W), lambda b, t: (0, 0)), sspec],
        out_specs=[tspec, sspec],
        out_shape=[jax.ShapeDtypeStruct((B, T, GROUP_W), BF16),
                   jax.ShapeDtypeStruct((B, N_HEADS, HEAD_W, HEAD_W), F32)],
        scratch_shapes=[pltpu.VMEM((N_HEADS, HEAD_W, HEAD_W), F32), pltpu.VMEM((tb, GROUP_W), F32)],
        compiler_params=_cparams(("parallel", "arbitrary")),
        name="hgrn",
    )(qh, kk, lf, hi, sg, gnorm_g.reshape(1, HEAD_W), s0)


def _post_kernel(x_ref, oda_ref, ohg_ref, mk_ref, mv_ref, wout_ref, lnm_ref, wq_ref, wo_ref,
                 lnf_ref, wr_ref, br_ref, x2_ref, h3_ref, route_ref, om_ref, *, nb, tm, split):
    gt = tm // split
    groups = [_post_group(slice(g * gt, (g + 1) * gt), g * nb * gt, x_ref, oda_ref, ohg_ref,
                          mk_ref, mv_ref, wout_ref, lnm_ref, wq_ref, wo_ref, lnf_ref, wr_ref,
                          br_ref, x2_ref, h3_ref, route_ref, om_ref, nb=nb, gt=gt)
              for g in range(split)]
    _run_skewed(groups)


def _post_group(ts, om0, x_ref, oda_ref, ohg_ref, mk_ref, mv_ref, wout_ref, lnm_ref, wq_ref, wo_ref,
                lnf_ref, wr_ref, br_ref, x2_ref, h3_ref, route_ref, om_ref, *, nb, gt):
    rows = nb * gt
    x = x_ref[:, ts, :].reshape(rows, D_MODEL)
    mixed = (_dot(oda_ref[:, ts, :].reshape(rows, GROUP_W), wout_ref[:GROUP_W, :])
             + _dot(ohg_ref[:, ts, :].reshape(rows, GROUP_W), wout_ref[GROUP_W:, :]))
    x1 = x + mixed
    yield

    hm = _rms(x1, lnm_ref[...]).astype(BF16)
    yield
    qm = (_dot(hm, wq_ref[...]) * MEM_SCALE).astype(BF16)
    yield
    for b in range(nb):
        if len(mk_ref.shape) == 4:
            mk_b = mk_ref[b].reshape(N_MEM, D_MODEL).astype(BF16)
            mv_b = mv_ref[b].reshape(N_MEM, D_MODEL).astype(BF16)
        for h in range(MEM_HEADS):
            hs = slice(h * MEM_HEAD_DIM, (h + 1) * MEM_HEAD_DIM)
            if len(mk_ref.shape) == 4:
                mk, mv = mk_b[:, hs], mv_b[:, hs]
            else:
                mk, mv = mk_ref[b, :, hs], mv_ref[b, :, hs]
            s = _dot_nt(qm[b * gt:(b + 1) * gt, hs], mk.astype(BF16))
            e = jnp.exp(s - jnp.max(s, axis=-1, keepdims=True))
            p = e / jnp.sum(e, axis=-1, keepdims=True)
            om_ref[om0 + b * gt:om0 + (b + 1) * gt, hs] = _dot(
                p.astype(BF16), mv.astype(BF16)).astype(BF16)
    yield
    x2 = x1 + _dot(om_ref[om0:om0 + rows, :], wo_ref[...])
    x2_ref[:, ts, :] = x2.reshape(nb, gt, D_MODEL)
    yield

    h3 = _rms(x2, lnf_ref[...])
    h3_ref[:, ts] = h3.reshape(nb, gt, N_SLAB, HEAD_W)
    yield

    r = _dot(h3.astype(BF16), wr_ref[...]) + br_ref[...]
    lane = lax.broadcasted_iota(jnp.int32, r.shape, 1).astype(F32)
    big = float(4 * HEAD_W)
    g_mask = lane < N_GROUPS
    gl = jnp.where(g_mask, r, NEG_INF)
    g_max = jnp.max(gl, axis=-1, keepdims=True)
    g_idx = jnp.min(jnp.where(gl == g_max, lane, big), axis=-1, keepdims=True)
    g_w = 1.0 / jnp.sum(jnp.where(g_mask, jnp.exp(r - g_max), 0.0), axis=-1, keepdims=True)
    e_lo = N_GROUPS + EXPERTS_PER_GROUP * g_idx
    el = jnp.where((lane >= e_lo) & (lane < e_lo + EXPERTS_PER_GROUP), r, NEG_INF)
    v1 = jnp.max(el, axis=-1, keepdims=True)
    i1 = jnp.min(jnp.where(el == v1, lane, big), axis=-1, keepdims=True)
    el2 = jnp.where(lane == i1, NEG_INF, el)
    v2 = jnp.max(el2, axis=-1, keepdims=True)
    i2 = jnp.min(jnp.where(el2 == v2, lane, big), axis=-1, keepdims=True)
    t = jnp.exp(v2 - v1)
    p1 = 1.0 / (1.0 + t)
    rec = jnp.where(lane == 0.0, i1 - N_GROUPS,
          jnp.where(lane == 1.0, i2 - N_GROUPS,
          jnp.where(lane == 2.0, p1 * g_w,
          jnp.where(lane == 3.0, t * p1 * g_w, 0.0))))
    route_ref[:, ts, :] = rec[:, :ROUTE_W].reshape(nb, gt, ROUTE_W)


def _post_mix(x, oda, ohg, mem_k, mem_v, w_out, ln_mem, w_q, w_o, ln_ffn, w_r, b_r, nb, tm, split):
    B, T, _ = x.shape
    tok = lambda w: pl.BlockSpec((nb, tm, w), lambda b, t: (b, t, 0))
    if mem_k.ndim == 4:
        memspec = pl.BlockSpec((nb, N_MEM, MEM_HEADS, MEM_HEAD_DIM), lambda b, t: (b, 0, 0, 0))
    else:
        memspec = pl.BlockSpec((nb, N_MEM, D_MODEL), lambda b, t: (b, 0, 0))
    fix = lambda s: pl.BlockSpec(s, lambda b, t: (0, 0))
    return pl.pallas_call(
        functools.partial(_post_kernel, nb=nb, tm=tm, split=split),
        grid=(B // nb, T // tm),
        in_specs=[tok(D_MODEL), tok(GROUP_W), tok(GROUP_W), memspec, memspec,
                  fix((D_MODEL, D_MODEL)), fix((1, D_MODEL)), fix((D_MODEL, D_MODEL)),
                  fix((D_MODEL, D_MODEL)), fix((1, D_MODEL)), fix((D_MODEL, HEAD_W)),
                  fix((1, HEAD_W))],
        out_specs=[tok(D_MODEL),
                   pl.BlockSpec((nb, tm, N_SLAB, HEAD_W), lambda b, t: (b, t, 0, 0)),
                   tok(ROUTE_W)],
        out_shape=[jax.ShapeDtypeStruct((B, T, D_MODEL), F32),
                   jax.ShapeDtypeStruct((B, T, N_SLAB, HEAD_W), F32),
                   jax.ShapeDtypeStruct((B, T, ROUTE_W), F32)],
        scratch_shapes=[pltpu.VMEM((nb * tm, D_MODEL), BF16)],
        compiler_params=_cparams(("parallel", "parallel")),
        name="post_mix",
    )(x, oda, ohg, mem_k, mem_v, w_out, ln_mem.reshape(1, D_MODEL), w_q, w_o,
      ln_ffn.reshape(1, D_MODEL), w_r, b_r)


def _experts_kernel(blk_e_ref, n_used_ref, src_next_ref, dst_prev_ref, src0_ref, dst_ref, h_hbm,
                    wg_ref, wu_ref, wd_ref, y_hbm, xbuf, ybuf, gsem, ssem, *, bm, m_tot):
    del blk_e_ref
    i = pl.program_id(0)
    n_used = n_used_ref[0]
    slot = i % 2

    def gather_row(tab_ref, r, s, priority=0):
        pltpu.make_async_copy(h_hbm.at[tab_ref[0, 0, r]], xbuf.at[s, r],
                              gsem.at[s]).start(priority=priority)

    def scatter_row(tab_ref, r, s, priority=0):
        pltpu.make_async_copy(ybuf.at[s, r], y_hbm.at[tab_ref[0, 0, r]],
                              ssem.at[s]).start(priority=priority)

    def wait_block(buf, sem, s):
        pltpu.make_async_copy(buf.at[s], buf.at[s], sem.at[s]).wait()

    def spare_fill(s):
        return pltpu.make_async_copy(ybuf.at[s], y_hbm.at[pl.ds(m_tot + s * bm, bm)], ssem.at[s])

    @pl.when(i == 0)
    def _():
        ybuf[...] = jnp.zeros(ybuf.shape, F32)
        spare_fill(0).start()
        spare_fill(1).start()
        spare_fill(1).wait()

        def body(r, c):
            gather_row(src0_ref, r, 0)
            return c
        lax.fori_loop(0, bm, body, 0)

    @pl.when(i < n_used)
    def _():
        wait_block(xbuf, gsem, slot)
        wait_block(ybuf, ssem, slot)
        x = xbuf[slot].reshape(bm, D_MODEL).astype(BF16)
        hg = _dot(x, wg_ref[0].astype(BF16))
        hu = _dot(x, wu_ref[0].astype(BF16))
        hb = (hg * jax.nn.sigmoid(hg) * hu).astype(BF16)
        ybuf[slot] = _dot(hb, wd_ref[0].astype(BF16)).reshape(bm, N_SLAB, HEAD_W)
        for r in range(bm):
            gather_row(src_next_ref, r, 1 - slot, priority=r % 2)
            scatter_row(dst_prev_ref, r, 1 - slot, priority=(r + 1) % 2)

    @pl.when(i == n_used - 1)
    def _():
        def body(r, c):
            scatter_row(dst_ref, r, slot)
            return c
        lax.fori_loop(0, bm, body, 0)
        wait_block(xbuf, gsem, 1 - slot)
        wait_block(ybuf, ssem, 1 - slot)
        wait_block(ybuf, ssem, slot)


def _experts(h3, blk_e, n_used, src_tab, dst_tab, dst_prev_tab, wg, wu, wd, bm):
    n_blocks = src_tab.shape[0]
    m_tot = TOP_K * h3.shape[0]
    tab = lambda f: pl.BlockSpec((1, 1, bm), f, memory_space=pltpu.SMEM)
    cur = lambda i, e, n: (i, 0, 0)
    nxt = lambda i, e, n: (jnp.minimum(i + 1, n_blocks - 1), 0, 0)
    first = lambda i, e, n: (0, 0, 0)
    wspec = lambda a, b: pl.BlockSpec((1, a, b), lambda i, e, n: (e[i], 0, 0))
    grid_spec = pltpu.PrefetchScalarGridSpec(
        num_scalar_prefetch=2,
        grid=(n_blocks,),
        in_specs=[tab(nxt), tab(cur), tab(first), tab(cur), pl.BlockSpec(memory_space=pl.ANY),
                  wspec(D_MODEL, EXPERT_FF), wspec(D_MODEL, EXPERT_FF), wspec(EXPERT_FF, D_MODEL)],
        out_specs=pl.BlockSpec(memory_space=pl.ANY),
        scratch_shapes=[pltpu.VMEM((2, bm, N_SLAB, HEAD_W), F32), pltpu.VMEM((2, bm, N_SLAB, HEAD_W), F32),
                        pltpu.SemaphoreType.DMA((2,)), pltpu.SemaphoreType.DMA((2,))],
    )
    return pl.pallas_call(
        functools.partial(_experts_kernel, bm=bm, m_tot=m_tot),
        grid_spec=grid_spec,
        out_shape=jax.ShapeDtypeStruct((m_tot + 2 * bm, N_SLAB, HEAD_W), F32),
        compiler_params=_cparams(("arbitrary",)),
        name="experts",
    )(blk_e, n_used, src_tab, dst_prev_tab, src_tab, dst_tab, h3, wg, wu, wd)


def _combine_kernel(x_ref, y0_ref, y1_ref, route_ref, g_ref, o_ref):
    rt = route_ref[...]
    tm = x_ref.shape[0]
    y0 = y0_ref[...].reshape(tm, D_MODEL)
    y1 = y1_ref[...].reshape(tm, D_MODEL)
    y = x_ref[...] + rt[:, 2:3] * y0 + rt[:, 3:4] * y1
    o_ref[...] = _rms(y, g_ref[...])


def _combine(x2, y, route, final_g, tm):
    n = x2.shape[0]
    nt = n // tm
    return pl.pallas_call(
        _combine_kernel,
        grid=(nt,),
        in_specs=[pl.BlockSpec((tm, D_MODEL), lambda i: (i, 0)),
                  pl.BlockSpec((tm, N_SLAB, HEAD_W), lambda i: (i, 0, 0)),
                  pl.BlockSpec((tm, N_SLAB, HEAD_W), lambda i: (i + nt, 0, 0)),
                  pl.BlockSpec((tm, ROUTE_W), lambda i: (i, 0)),
                  pl.BlockSpec((1, D_MODEL), lambda i: (0, 0))],
        out_specs=pl.BlockSpec((tm, D_MODEL), lambda i: (i, 0)),
        out_shape=jax.ShapeDtypeStruct((n, D_MODEL), F32),
        compiler_params=_cparams(("parallel",)),
        name="combine",
    )(x2, y, y, route, final_g.reshape(1, D_MODEL))


def _routing_tables(route, bm):
    n = route.shape[0]
    m_tot = TOP_K * n
    flat_e = jnp.concatenate([route[:, 0], route[:, 1]]).astype(jnp.int32)
    experts = jnp.arange(N_EXPERTS, dtype=jnp.int32)
    counts = jnp.sum(flat_e[:, None] == experts[None, :], axis=0, dtype=jnp.int32)
    n_blk_e = (counts + bm - 1) // bm
    blk_end = jnp.cumsum(n_blk_e)
    n_blocks = m_tot // bm + N_EXPERTS
    blk = jnp.arange(n_blocks, dtype=jnp.int32)
    blk_e = jnp.minimum(jnp.sum(blk[:, None] >= blk_end[None, :], axis=1, dtype=jnp.int32),
                        N_EXPERTS - 1)
    r = jnp.arange(bm, dtype=jnp.int32)[None, :]
    n_pad = (n_blk_e * bm - counts)[:, None]
    pad_keys = jnp.where(r < n_pad, 2 * experts[:, None] + 1, 2 * N_EXPERTS + 1)
    payload_bits = max(m_tot.bit_length(), 1)
    assert (2 * N_EXPERTS + 2) << payload_bits < 2 ** 31
    packed = jnp.concatenate([(2 * flat_e << payload_bits) + jnp.arange(1, m_tot + 1, dtype=jnp.int32),
                              pad_keys.reshape(-1) << payload_bits])
    m = ((lax.sort(packed) & ((1 << payload_bits) - 1)) - 1).reshape(n_blocks, bm)
    real = m >= 0
    spare = m_tot + (blk[:, None] % 2) * bm + r
    src = jnp.where(real, m % n, 0)
    dst = jnp.where(real, m, spare)
    dst_prev = jnp.concatenate([m_tot + bm + r, dst[:-1]], axis=0)
    shape = (n_blocks, 1, bm)
    return (blk_e, blk_end[N_EXPERTS - 1:].astype(jnp.int32), src.reshape(shape), dst.reshape(shape),
            dst_prev.reshape(shape))


def _layer(x, past_k, past_v, s0, mem_k, mem_v, p, cfg):
    B, T, _ = x.shape
    n = B * T
    (q, kf, vf, kb, vt, qh, lf, kk, hi, sg) = _in_proj(
        x, p["ln_mix"], p["w_in"], p["lb"], cfg["nb"], cfg["tm_proj"], cfg["split_proj"])
    if past_k is None:
        oda = _diff_attn(q, kb, vt, p["slopes"], p["lam"], p["da_subln"], cfg["tq"], cfg["tk"])
    else:
        oda = _diff_attn_step(q, kb, vt, past_k, past_v, p["slopes"], p["lam"], p["da_subln"])
    ohg, s_new = _hgrn(qh, kk, lf, hi, sg, p["hg_gnorm"], s0, cfg["tb"])
    x2, h3, route = _post_mix(x, oda, ohg, mem_k, mem_v, p["w_out"], p["ln_mem"], p["w_mem_q"],
                              p["w_mem_o"], p["ln_ffn"], p["w_r"], p["b_r"], cfg["nb"], cfg["tm_post"],
                              cfg["split"])
    route2 = route.reshape(n, ROUTE_W)
    blk_e, n_used, src_tab, dst_tab, dst_prev_tab = _routing_tables(route2, cfg["bm"])
    y = _experts(h3.reshape(n, N_SLAB, HEAD_W), blk_e, n_used, src_tab, dst_tab, dst_prev_tab,
                 p["e_gate"], p["e_up"], p["e_down"], cfg["bm"])
    out = _combine(x2.reshape(n, D_MODEL), y, route2, p["final_g"], cfg["tm_in"])
    return out.reshape(B, T, D_MODEL), kf[None], vf[None], s_new[None]


PROMPT_CFG = dict(tm_in=256, tq=512, tk=256, tb=256, nb=1, tm_proj=512, split_proj=2, tm_post=1024, split=4,
                  bm=512)
SAMPLE_CFG = dict(tm_in=256, tb=64, nb=4, tm_proj=64, split_proj=1, tm_post=64, split=1, bm=128)


def kernel(x_prompt, x_sample, mem_prompt, cache_diff_k, cache_diff_v, state_hgrn, cache_mem_k, cache_mem_v, ln_mix_g, w_in, da_lambda, da_subln_g, hg_lb_logits, hg_gnorm_g, w_out, ln_mem_g, mem_norm_g, w_mem_q, w_mem_k, w_mem_v, w_mem_o, ln_ffn_g, router_group_w, router_group_b, router_expert_w, router_expert_b, exp_w_gate, exp_w_up, exp_w_down, final_norm_g):
    assert w_in.shape[0] == 1, "single-layer configuration"
    lb_all = jnp.cumsum(jax.nn.softmax(hg_lb_logits.astype(F32), axis=0), axis=0)
    lp = da_lambda[0].astype(F32)
    lam = jnp.exp(jnp.sum(lp[0] * lp[1])) - jnp.exp(jnp.sum(lp[2] * lp[3])) + LAM_INIT
    w_r = jnp.zeros((D_MODEL, HEAD_W), F32)
    w_r = w_r.at[:, :N_GROUPS].set(router_group_w[0]).at[:, N_GROUPS:N_GROUPS + N_EXPERTS].set(router_expert_w[0])
    b_r = jnp.zeros((1, HEAD_W), F32)
    b_r = b_r.at[0, :N_GROUPS].set(router_group_b[0]).at[0, N_GROUPS:N_GROUPS + N_EXPERTS].set(router_expert_b[0])
    p = {
        "ln_mix": ln_mix_g[0], "w_in": w_in[0].astype(BF16), "lb": lb_all[0],
        "slopes": jnp.exp2(-8.0 * jnp.arange(1, N_HEADS + 1, dtype=F32) / N_HEADS),
        "lam": lam.reshape(1), "da_subln": da_subln_g[0], "hg_gnorm": hg_gnorm_g[0],
        "w_out": w_out[0].astype(BF16), "ln_mem": ln_mem_g[0], "w_mem_q": w_mem_q[0].astype(BF16),
        "w_mem_o": w_mem_o[0].astype(BF16), "ln_ffn": ln_ffn_g[0], "w_r": w_r.astype(BF16), "b_r": b_r,
        "e_gate": exp_w_gate[0], "e_up": exp_w_up[0],
        "e_down": exp_w_down[0], "final_g": final_norm_g,
    }
    mkf, mvf, mkb, mvb = _memory_kv(mem_prompt, mem_norm_g[0], w_mem_k[0].astype(BF16),
                                    w_mem_v[0].astype(BF16))
    zero_state = jnp.zeros((x_prompt.shape[0], N_HEADS, HEAD_W, HEAD_W), F32)
    yp, kp, vp, sp = _layer(x_prompt, None, None, zero_state, mkb, mvb, p, PROMPT_CFG)
    ys, ks, vs, ss = _layer(
        x_sample, cache_diff_k[0], cache_diff_v[0], state_hgrn[0], cache_mem_k[0], cache_mem_v[0],
        p, SAMPLE_CFG)
    return (yp, ys, kp, vp, sp, mkf[None], mvf[None], ks, vs, ss)
```

```python
import functools
import math

import jax
import jax.numpy as jnp
from jax import lax
from jax.experimental import pallas as pl
from jax.experimental.pallas import tpu as pltpu

F32 = jnp.float32
BF16 = jnp.bfloat16

D_MODEL = 1024
EPS = 1e-5
CHUNK = 64
N_HEADS = 4
HEAD_W = 128
QK_DIM = 64
GROUP_W = N_HEADS * HEAD_W
N_PROJ = 7
DA_SCALE = QK_DIM ** -0.5
LOG2E = 1.4426950408889634
LAM_INIT = 0.8 - 0.6 * math.exp(-0.3 * 0)
N_MEM = 256
MEM_HEADS = 4
MEM_HEAD_DIM = D_MODEL // MEM_HEADS
MEM_SCALE = MEM_HEAD_DIM ** -0.5
N_GROUPS = 4
EXPERTS_PER_GROUP = 8
N_EXPERTS = N_GROUPS * EXPERTS_PER_GROUP
TOP_K = 2
EXPERT_FF = 512
ROUTE_W = 8
SUB = 16
SUB_KEYS = 256
N_SCORE_SLOTS = 3
N_SLAB = D_MODEL // HEAD_W
NEG_INF = float("-inf")

VMEM_LIMIT = 48 * 1024 * 1024


def _cparams(sem):
    return pltpu.CompilerParams(dimension_semantics=sem, vmem_limit_bytes=VMEM_LIMIT)


def _rms(x, g):
    return x * lax.rsqrt(jnp.mean(x * x, axis=-1, keepdims=True) + EPS) * g


def _dot(a, b):
    return jnp.dot(a, b, preferred_element_type=F32)


def _dot_nt(a, b):
    return lax.dot_general(a, b, (((1,), (1,)), ((), ())), preferred_element_type=F32)


def _dot_tn(a, b):
    return lax.dot_general(a, b, (((0,), (0,)), ((), ())), preferred_element_type=F32)


def _run_skewed(gens):
    waiting, live = list(gens), []
    while waiting or live:
        if waiting:
            live.append(waiting.pop(0))
        for gen in list(live):
            if next(gen, "done") == "done":
                live.remove(gen)


def _memkv_kernel(m_ref, g_ref, wk_ref, wv_ref, kf_ref, vf_ref, kb_ref, vb_ref):
    mn = _rms(m_ref[0], g_ref[...]).astype(BF16)
    k = _dot(mn, wk_ref[...])
    v = _dot(mn, wv_ref[...])
    kf_ref[0] = k.reshape(N_MEM, MEM_HEADS, MEM_HEAD_DIM)
    vf_ref[0] = v.reshape(N_MEM, MEM_HEADS, MEM_HEAD_DIM)
    kb_ref[0] = k.astype(BF16)
    vb_ref[0] = v.astype(BF16)


def _memory_kv(mem, g, wk, wv):
    B = mem.shape[0]
    blk = pl.BlockSpec((1, N_MEM, D_MODEL), lambda b: (b, 0, 0))
    hblk = pl.BlockSpec((1, N_MEM, MEM_HEADS, MEM_HEAD_DIM), lambda b: (b, 0, 0, 0))
    wspec = pl.BlockSpec((D_MODEL, D_MODEL), lambda b: (0, 0))
    return pl.pallas_call(
        _memkv_kernel,
        grid=(B,),
        in_specs=[blk, pl.BlockSpec((1, D_MODEL), lambda b: (0, 0)), wspec, wspec],
        out_specs=[hblk, hblk, blk, blk],
        out_shape=[jax.ShapeDtypeStruct((B, N_MEM, MEM_HEADS, MEM_HEAD_DIM), F32)] * 2
                  + [jax.ShapeDtypeStruct(mem.shape, BF16)] * 2,
        compiler_params=_cparams(("parallel",)),
        name="memory_kv",
    )(mem, g.reshape(1, D_MODEL), wk, wv)


def _in_proj_kernel(x_ref, g_ref, w_ref, lb_ref, q_ref, kf_ref, vf_ref, kb_ref, vt_ref,
                    qh_ref, lf_ref, kk_ref, hi_ref, sg_ref, *, nb, tm, split):
    gt = tm // split
    out_refs = (q_ref, kf_ref, vf_ref, kb_ref, vt_ref, qh_ref, lf_ref, kk_ref, hi_ref, sg_ref)
    _run_skewed([_in_proj_group(slice(g * gt, (g + 1) * gt), x_ref, g_ref, w_ref, lb_ref, out_refs,
                                nb=nb, gt=gt) for g in range(split)])


def _in_proj_group(ts, x_ref, g_ref, w_ref, lb_ref, out_refs, *, nb, gt):
    q_ref, kf_ref, vf_ref, kb_ref, vt_ref, qh_ref, lf_ref, kk_ref, hi_ref, sg_ref = out_refs
    rows = nb * gt
    h = _rms(x_ref[:, ts, :].reshape(rows, D_MODEL), g_ref[...]).astype(BF16)
    yield

    def proj(i):
        return _dot(h, w_ref[:, i * GROUP_W:(i + 1) * GROUP_W])

    def put(ref, val):
        ref[:, ts, :] = val.reshape(nb, gt, GROUP_W).astype(ref.dtype)

    def put_heads(ref, val):
        ref[:, ts] = val.reshape(nb, gt, N_HEADS, HEAD_W)

    put(q_ref, proj(0) * (DA_SCALE * LOG2E))
    yield
    dk = proj(1)
    put_heads(kf_ref, dk)
    put(kb_ref, dk)
    yield
    dv = proj(2)
    put_heads(vf_ref, dv)
    for b in range(nb):
        vt_ref[b, :, ts] = dv[b * gt:(b + 1) * gt].T.astype(BF16)
    yield
    hq = proj(3)
    put(qh_ref, hq * jax.nn.sigmoid(hq))
    yield
    hf = proj(4)
    lb = lb_ref[...]
    put(lf_ref, jnp.log(lb + (1.0 - lb) * jax.nn.sigmoid(hf)))
    put(kk_ref, (1.0 - lb) * jax.nn.sigmoid(-hf))
    yield
    put(hi_ref, proj(5))
    yield
    hg = proj(6)
    put(sg_ref, hg * jax.nn.sigmoid(hg))
    yield


def _in_proj(x, g, w_bf, lb, nb, tm, split):
    B, T, _ = x.shape
    tok = lambda w: pl.BlockSpec((nb, tm, w), lambda b, t: (b, t, 0))
    fix = lambda s: pl.BlockSpec(s, lambda b, t: (0, 0))
    f32o = jax.ShapeDtypeStruct((B, T, GROUP_W), F32)
    bfo = jax.ShapeDtypeStruct((B, T, GROUP_W), BF16)
    o_spec = tok(GROUP_W)
    ho = jax.ShapeDtypeStruct((B, T, N_HEADS, HEAD_W), F32)
    h_spec = pl.BlockSpec((nb, tm, N_HEADS, HEAD_W), lambda b, t: (b, t, 0, 0))
    vt_spec = pl.BlockSpec((nb, GROUP_W, tm), lambda b, t: (b, 0, t))
    return pl.pallas_call(
        functools.partial(_in_proj_kernel, nb=nb, tm=tm, split=split),
        grid=(B // nb, T // tm),
        in_specs=[tok(D_MODEL), fix((1, D_MODEL)), fix((D_MODEL, N_PROJ * GROUP_W)), fix((1, GROUP_W))],
        out_specs=[o_spec, h_spec, h_spec, o_spec, vt_spec] + [o_spec] * 5,
        out_shape=[bfo, ho, ho, bfo, jax.ShapeDtypeStruct((B, GROUP_W, T), BF16),
                   bfo, f32o, bfo, bfo, bfo],
        compiler_params=_cparams(("parallel", "parallel")),
        name="in_proj",
    )(x, g.reshape(1, D_MODEL), w_bf, lb.reshape(1, GROUP_W))


def _attn_kernel(slope_ref, lam_ref, q_ref, k_ref, vt_ref, g_ref, o_ref,
                 m_ref, l_ref, acc_ref, base_ref, t_ref, mb_ref, *, tq, tk):
    h = pl.program_id(1)
    qi = pl.program_id(2)
    slope2 = slope_ref[h] * LOG2E
    qt = q_ref[0].astype(F32).T.astype(BF16)
    zeros = jnp.zeros((QK_DIM, tq), BF16)
    qt_maps = (jnp.concatenate([qt[:QK_DIM], zeros], axis=0),
               jnp.concatenate([zeros, qt[QK_DIM:]], axis=0))
    m_ref[...] = jnp.full(m_ref.shape, NEG_INF, F32)
    l_ref[...] = jnp.zeros(l_ref.shape, F32)
    acc_ref[...] = jnp.zeros(acc_ref.shape, F32)
    q0 = qi * tq

    def lane_pad(x, lo, fill):
        if lo == 0:
            return x
        return jnp.concatenate([jnp.full((x.shape[0], lo), fill, x.dtype), x], axis=1)


    def score_phase(slot, n_keys, get_k, get_bias, shift, q_lo=lambda u: 0):
        sub = min(n_keys, SUB_KEYS)
        mx = [None, None]
        for u in range(n_keys // sub):
            lo = q_lo(u)
            bias, visible = get_bias(u)
            k = get_k(u)
            for c in range(2):
                t = _dot(k, qt_maps[c][:, lo:]) + bias
                if visible is not None:
                    w = visible.shape[1]
                    head = jnp.where(visible, t[:, :w], NEG_INF)
                    t = head if w == t.shape[1] else jnp.concatenate([head, t[:, w:]], axis=1)
                t_ref[slot, c, u * sub:(u + 1) * sub, lo:] = t
                mu = lane_pad(jnp.max(t, axis=0, keepdims=True), lo, NEG_INF)
                mx[c] = mu if mx[c] is None else jnp.maximum(mx[c], mu)
            yield
        for c in range(2):
            mb_ref[slot, c] = mx[c] + shift
        yield

    def prob_phase(slot, n_keys, get_vt, shift, q_lo=lambda u: 0, pv_group=1):
        sub = min(n_keys, SUB_KEYS)
        n_sub = n_keys // sub
        off, alpha = [], []
        for c in range(2):
            m_old = m_ref[c]
            m_new = jnp.maximum(m_old, mb_ref[slot, c])
            off.append(m_new - shift)
            alpha.append(jnp.exp2(m_old - m_new))
            m_ref[c] = m_new
        yield
        lsum = [None, None]
        pv = [None, None]
        grp = pv_group if n_sub % pv_group == 0 else 1
        for u in range(0, n_sub, grp):
            lo = q_lo(u)
            assert all(q_lo(u + i) == lo for i in range(grp))
            parts = [get_vt(u + i) for i in range(grp)]
            vt = parts[0] if grp == 1 else jnp.concatenate(parts, axis=1)
            for c in range(2):
                p = jnp.exp2(t_ref[slot, c, u * sub:(u + grp) * sub, lo:] - off[c][:, lo:])
                ls = lane_pad(jnp.sum(p, axis=0, keepdims=True), lo, 0.0)
                pu = lane_pad(_dot(vt, p.astype(BF16)), lo, 0.0)
                lsum[c] = ls if lsum[c] is None else lsum[c] + ls
                pv[c] = pu if pv[c] is None else pv[c] + pu
            yield
        for c in range(2):
            l_ref[c] = alpha[c] * l_ref[c] + lsum[c]
            acc_ref[c] = alpha[c] * acc_ref[c] + pv[c]
        yield

    def run(*gens):
        live = list(gens)
        while live:
            for gen in list(live):
                if next(gen, "done") == "done":
                    live.remove(gen)

    def update(n_keys, get_k, get_vt, get_bias, shift, q_lo=lambda u: 0, pv_group=1):
        run(score_phase(0, n_keys, get_k, get_bias, shift, q_lo))
        run(prob_phase(0, n_keys, get_vt, shift, q_lo, pv_group))

    def rel_bias(n_keys, key0):
        key = lax.broadcasted_iota(jnp.int32, (n_keys, tq), 0) + key0
        qry = lax.broadcasted_iota(jnp.int32, (n_keys, tq), 1)
        return (qry - key).astype(F32) * (-slope2)

    def block_shift(k0):
        return (q0 - k0).astype(F32) * (-slope2)

    n_q = k_ref.shape[1] // tq
    if n_q > 1 or tq > SUB_KEYS:
        @pl.when(qi == 0)
        def _():
            base_ref[...] = rel_bias(base_ref.shape[0], 0)

    kd = 0 if n_q == 1 else pl.multiple_of(qi * tq, tq)
    sub_d = min(tq, SUB_KEYS)
    key = lax.broadcasted_iota(jnp.int32, (sub_d, sub_d), 0)
    qry = lax.broadcasted_iota(jnp.int32, (sub_d, sub_d), 1)
    strip_bias = jnp.abs(qry - key).astype(F32) * (-slope2)
    strip_visible = (key // CHUNK) <= (qry // CHUNK)

    def diag_bias(u):
        lo = (u + 1) * sub_d
        if lo == tq:
            return strip_bias, strip_visible
        later = base_ref[u * sub_d:(u + 1) * sub_d, lo:]
        return jnp.concatenate([strip_bias, later], axis=1), strip_visible

    diag_k = lambda u: k_ref[0, pl.ds(kd + u * sub_d, sub_d), :]
    diag_vt = lambda u: vt_ref[0, :, pl.ds(kd + u * sub_d, sub_d)]
    diag_lo = lambda u: u * sub_d

    if n_q == 1:
        update(tq, diag_k, diag_vt, diag_bias, 0.0, q_lo=diag_lo)
    else:
        sub_k = min(tk, SUB_KEYS)
        n_prev = qi * (tq // tk)
        assert (tq // tk) % 2 == 0, "earlier blocks are consumed in pairs"

        def prev_scores(slot, j):
            k0 = pl.multiple_of(j * tk, tk)
            return score_phase(slot, tk, lambda u: k_ref[0, pl.ds(k0 + u * sub_k, sub_k), :],
                               lambda u: (base_ref[u * sub_k:(u + 1) * sub_k, :], None),
                               block_shift(k0))

        def prev_probs(slot, j):
            k0 = pl.multiple_of(j * tk, tk)
            return prob_phase(slot, tk, lambda u: vt_ref[0, :, pl.ds(k0 + u * sub_k, sub_k)],
                              block_shift(k0), pv_group=2)

        run(score_phase(0, tq, diag_k, diag_bias, 0.0, diag_lo))
        run(prob_phase(0, tq, diag_vt, 0.0, diag_lo), prev_scores(1, 0))

        def pair_body(i, carry):
            j = 2 * i
            run(prev_probs(1, j), prev_scores(2, j + 1))
            run(prev_probs(2, j + 1), prev_scores(1, jnp.minimum(j + 2, n_prev - 1)))
            return carry
        lax.fori_loop(0, n_prev // 2, pair_body, 0)

    lam = lam_ref[0]
    ot = acc_ref[0] / l_ref[0] - lam * (acc_ref[1] / l_ref[1])
    ot = ot * lax.rsqrt(jnp.mean(ot * ot, axis=0, keepdims=True) + EPS)
    o_ref[0] = (ot.T * g_ref[...] * (1.0 - LAM_INIT)).astype(o_ref.dtype)


def _diff_attn(q, k, vt, slopes, lam, subln_g, tq, tk):
    B, T, _ = q.shape
    smem = pl.BlockSpec(memory_space=pltpu.SMEM)
    qspec = pl.BlockSpec((1, tq, HEAD_W), lambda b, h, i: (b, i, h))
    return pl.pallas_call(
        functools.partial(_attn_kernel, tq=tq, tk=tk),
        grid=(B, N_HEADS, T // tq),
        in_specs=[smem, smem, qspec,
                  pl.BlockSpec((1, T, HEAD_W), lambda b, h, i: (b, 0, h)),
                  pl.BlockSpec((1, HEAD_W, T), lambda b, h, i: (b, h, 0)),
                  pl.BlockSpec((1, HEAD_W), lambda b, h, i: (0, 0))],
        out_specs=qspec,
        out_shape=jax.ShapeDtypeStruct((B, T, GROUP_W), BF16),
        scratch_shapes=[pltpu.VMEM((2, 1, tq), F32), pltpu.VMEM((2, 1, tq), F32),
                        pltpu.VMEM((2, HEAD_W, tq), F32), pltpu.VMEM((max(tk, tq), tq), F32),
                        pltpu.VMEM((N_SCORE_SLOTS, 2, max(tk, tq), tq), F32),
                        pltpu.VMEM((N_SCORE_SLOTS, 2, 1, tq), F32)],
        compiler_params=_cparams(("parallel", "parallel", "arbitrary")),
        name="diff_attn",
    )(slopes, lam, q, k, vt, subln_g.reshape(1, HEAD_W))


def _attn_step_kernel(slope_ref, lam_ref, q_ref, k_ref, vt_ref, pk_ref, pv_ref, g_ref, o_ref, *,
                      t_new, past_len):
    assert 2 * t_new <= HEAD_W and t_new % CHUNK == 0 and past_len % CHUNK == 0
    pk = pk_ref[0].reshape(past_len, GROUP_W)
    pv = pv_ref[0].reshape(past_len, GROUP_W)
    lam = lam_ref[0]
    qry2 = lax.broadcasted_iota(jnp.int32, (1, 2 * t_new), 1) % t_new
    key_p = lax.broadcasted_iota(jnp.int32, (past_len, 2 * t_new), 0)
    key_n = lax.broadcasted_iota(jnp.int32, (t_new, 2 * t_new), 0)
    dist_p = (qry2 + past_len - key_p).astype(F32)
    dist_n = jnp.abs(qry2 - key_n).astype(F32)
    visible = (key_n // CHUNK) <= (qry2 // CHUNK)
    zeros = jnp.zeros((QK_DIM, t_new), BF16)
    for h in range(N_HEADS):
        hs = slice(h * HEAD_W, (h + 1) * HEAD_W)
        slope2 = slope_ref[h] * LOG2E
        qt = q_ref[0, :, hs].astype(F32).T.astype(BF16)
        qt2 = jnp.concatenate([jnp.concatenate([qt[:QK_DIM], zeros], axis=0),
                               jnp.concatenate([zeros, qt[QK_DIM:]], axis=0)], axis=1)
        s_p = _dot(pk[:, hs].astype(BF16), qt2) - slope2 * dist_p
        s_n = jnp.where(visible, _dot(k_ref[0, :, hs], qt2) - slope2 * dist_n, NEG_INF)
        m = jnp.maximum(jnp.max(s_p, axis=0, keepdims=True), jnp.max(s_n, axis=0, keepdims=True))
        p_p = jnp.exp2(s_p - m)
        p_n = jnp.exp2(s_n - m)
        l = jnp.sum(p_p, axis=0, keepdims=True) + jnp.sum(p_n, axis=0, keepdims=True)
        acc = (_dot_tn(pv[:, hs].astype(BF16), p_p.astype(BF16))
               + _dot(vt_ref[0, hs, :], p_n.astype(BF16))) / l
        ot = acc[:, :t_new] - lam * acc[:, t_new:]
        ot = ot * lax.rsqrt(jnp.mean(ot * ot, axis=0, keepdims=True) + EPS)
        o_ref[0, :, hs] = (ot.T * g_ref[...] * (1.0 - LAM_INIT)).astype(o_ref.dtype)


def _diff_attn_step(q, k, vt, past_k, past_v, slopes, lam, subln_g):
    B, T, _ = q.shape
    past_len = past_k.shape[1]
    smem = pl.BlockSpec(memory_space=pltpu.SMEM)
    tok = pl.BlockSpec((1, T, GROUP_W), lambda b: (b, 0, 0))
    pspec = pl.BlockSpec((1, past_len, N_HEADS, HEAD_W), lambda b: (b, 0, 0, 0))
    return pl.pallas_call(
        functools.partial(_attn_step_kernel, t_new=T, past_len=past_len),
        grid=(B,),
        in_specs=[smem, smem, tok, tok, pl.BlockSpec((1, GROUP_W, T), lambda b: (b, 0, 0)),
                  pspec, pspec, pl.BlockSpec((1, HEAD_W), lambda b: (0, 0))],
        out_specs=tok,
        out_shape=jax.ShapeDtypeStruct((B, T, GROUP_W), BF16),
        compiler_params=_cparams(("parallel",)),
        name="diff_attn_step",
    )(slopes, lam, q, k, vt, past_k, past_v, subln_g.reshape(1, HEAD_W))


def _split3(x):
    hi = x.astype(BF16)
    r = x - hi.astype(F32)
    mid = r.astype(BF16)
    lo = (r - mid.astype(F32)).astype(BF16)
    return hi, mid, lo


def _hgrn_kernel(q_ref, k_ref, lf_ref, v_ref, sg_ref, gn_ref, s0_ref, o_ref, sfin_ref,
                 st_ref, b_ref, *, tb):
    t = pl.program_id(1)
    n_chunks = tb // CHUNK

    @pl.when(t == 0)
    def _():
        for h in range(N_HEADS):
            st_ref[h] = s0_ref[0, h].T

    row = lax.broadcasted_iota(jnp.int32, (tb, tb), 0)
    col = lax.broadcasted_iota(jnp.int32, (tb, tb), 1)
    tri = jnp.where((col <= row) & (col // CHUNK == row // CHUNK), 1.0, 0.0).astype(BF16)
    hi, mid, lo = _split3(lf_ref[0] * LOG2E)
    b_ref[...] = _dot(tri, hi) + _dot(tri, mid) + _dot(tri, lo)

    sub_row = lax.broadcasted_iota(jnp.int32, (8, HEAD_W), 0)

    def pad_rows(x, r0):
        parts = []
        if r0:
            parts.append(jnp.zeros((r0, HEAD_W), F32))
        parts.append(x)
        rest = CHUNK - r0 - x.shape[0]
        if rest:
            parts.append(jnp.zeros((rest, HEAD_W), F32))
        return jnp.concatenate(parts, axis=0) if len(parts) > 1 else x

    def chunk_body(c, carry):
        r0 = pl.multiple_of(c * CHUNK, CHUNK)
        def head(h):
            hs = slice(h * HEAD_W, (h + 1) * HEAD_W)
            q = q_ref[0, pl.ds(r0, CHUNK), hs].astype(F32)
            k = k_ref[0, pl.ds(r0, CHUNK), hs].astype(F32)
            v_bf = v_ref[0, pl.ds(r0, CHUNK), hs]
            v = v_bf.astype(F32)
            b = b_ref[pl.ds(r0, CHUNK), hs]
            b_last = b[CHUNK - 1:CHUNK, :]
            st = st_ref[h]

            o = _dot_nt((q * jnp.exp2(b)).astype(BF16), st.astype(BF16))

            q_segs, k_segs = [], []
            for i in range(1, CHUNK // SUB):
                lo_r, hi_r = i * SUB, (i + 1) * SUB
                ref_b = b[lo_r - 1:lo_r, :]
                q_segs.append(pad_rows(q[lo_r:hi_r] * jnp.exp2(b[lo_r:hi_r] - ref_b), lo_r))
                k_segs.append(pad_rows(k[:lo_r] * jnp.exp2(ref_b - b[:lo_r]), 0))
            a_off = _dot_nt(jnp.concatenate(q_segs, axis=1).astype(BF16),
                            jnp.concatenate(k_segs, axis=1).astype(BF16))
            o = o + _dot(a_off.astype(BF16), v_bf)
            yield

            diag = []
            for blk in range(CHUNK // SUB):
                base = blk * SUB
                qa, qb = q[base:base + 8], q[base + 8:base + SUB]
                ba, bb = b[base:base + 8], b[base + 8:base + SUB]
                oa = jnp.zeros((8, HEAD_W), F32)
                ob = jnp.zeros((8, HEAD_W), F32)
                for s in range(SUB):
                    ks = k[base + s:base + s + 1]
                    bs = b[base + s:base + s + 1]
                    vs = v[base + s:base + s + 1]
                    if s < 8:
                        e = jnp.exp2(ba - bs)
                        if s:
                            e = jnp.where(sub_row >= s, e, 0.0)
                        oa = oa + jnp.sum(qa * ks * e, axis=-1, keepdims=True) * vs
                        ob = ob + jnp.sum(qb * ks * jnp.exp2(bb - bs), axis=-1, keepdims=True) * vs
                    else:
                        e = jnp.exp2(bb - bs)
                        if s > 8:
                            e = jnp.where(sub_row >= s - 8, e, 0.0)
                        ob = ob + jnp.sum(qb * ks * e, axis=-1, keepdims=True) * vs
                diag += [oa, ob]
                yield
            o = o + jnp.concatenate(diag, axis=0)

            k_dec = (k * jnp.exp2(b_last - b)).astype(BF16)
            st_ref[h] = st * jnp.exp2(b_last) + _dot_tn(v_bf, k_dec)

            out = _rms(o, gn_ref[...]) * sg_ref[0, pl.ds(r0, CHUNK), hs]
            o_ref[0, pl.ds(r0, CHUNK), hs] = out.astype(o_ref.dtype)
            yield

        _run_skewed([head(h) for h in range(N_HEADS)])
        return carry

    lax.fori_loop(0, n_chunks, chunk_body, 0)

    @pl.when(t == pl.num_programs(1) - 1)
    def _():
        for h in range(N_HEADS):
            sfin_ref[0, h] = st_ref[h].T


def _hgrn(qh, kk, lf, hi, sg, gnorm_g, s0, tb):
    B, T, _ = qh.shape
    tspec = pl.BlockSpec((1, tb, GROUP_W), lambda b, t: (b, t, 0))
    sspec = pl.BlockSpec((1, N_HEADS, HEAD_W, HEAD_W), lambda b, t: (b, 0, 0, 0))
    return pl.pallas_call(
        functools.partial(_hgrn_kernel, tb=tb),
        grid=(B, T // tb),
        in_specs=[tspec] * 5 + [pl.BlockSpec((1, HEAD_W), lambda b, t: (0, 0)), sspec],
        out_specs=[tspec, sspec],
        out_shape=[jax.ShapeDtypeStruct((B, T, GROUP_W), BF16),
                   jax.ShapeDtypeStruct((B, N_HEADS, HEAD_W, HEAD_W), F32)],
        scratch_shapes=[pltpu.VMEM((N_HEADS, HEAD_W, HEAD_W), F32), pltpu.VMEM((tb, GROUP_W), F32)],
        compiler_params=_cparams(("parallel", "arbitrary")),
        name="hgrn",
    )(qh, kk, lf, hi, sg, gnorm_g.reshape(1, HEAD_W), s0)


def _post_kernel(x_ref, oda_ref, ohg_ref, mk_ref, mv_ref, wout_ref, lnm_ref, wq_ref, wo_ref,
                 lnf_ref, wr_ref, br_ref, x2_ref, h3_ref, route_ref, om_ref, *, nb, tm, split):
    gt = tm // split
    groups = [_post_group(slice(g * gt, (g + 1) * gt), g * nb * gt, x_ref, oda_ref, ohg_ref,
                          mk_ref, mv_ref, wout_ref, lnm_ref, wq_ref, wo_ref, lnf_ref, wr_ref,
                          br_ref, x2_ref, h3_ref, route_ref, om_ref, nb=nb, gt=gt)
              for g in range(split)]
    _run_skewed(groups)


def _post_group(ts, om0, x_ref, oda_ref, ohg_ref, mk_ref, mv_ref, wout_ref, lnm_ref, wq_ref, wo_ref,
                lnf_ref, wr_ref, br_ref, x2_ref, h3_ref, route_ref, om_ref, *, nb, gt):
    rows = nb * gt
    x = x_ref[:, ts, :].reshape(rows, D_MODEL)
    mixed = (_dot(oda_ref[:, ts, :].reshape(rows, GROUP_W), wout_ref[:GROUP_W, :])
             + _dot(ohg_ref[:, ts, :].reshape(rows, GROUP_W), wout_ref[GROUP_W:, :]))
    x1 = x + mixed
    yield

    hm = _rms(x1, lnm_ref[...]).astype(BF16)
    yield
    qm = (_dot(hm, wq_ref[...]) * MEM_SCALE).astype(BF16)
    yield
    for b in range(nb):
        if len(mk_ref.shape) == 4:
            mk_b = mk_ref[b].reshape(N_MEM, D_MODEL).astype(BF16)
            mv_b = mv_ref[b].reshape(N_MEM, D_MODEL).astype(BF16)
        for h in range(MEM_HEADS):
            hs = slice(h * MEM_HEAD_DIM, (h + 1) * MEM_HEAD_DIM)
            if len(mk_ref.shape) == 4:
                mk, mv = mk_b[:, hs], mv_b[:, hs]
            else:
                mk, mv = mk_ref[b, :, hs], mv_ref[b, :, hs]
            s = _dot_nt(qm[b * gt:(b + 1) * gt, hs], mk.astype(BF16))
            e = jnp.exp(s - jnp.max(s, axis=-1, keepdims=True))
            p = e / jnp.sum(e, axis=-1, keepdims=True)
            om_ref[om0 + b * gt:om0 + (b + 1) * gt, hs] = _dot(
                p.astype(BF16), mv.astype(BF16)).astype(BF16)
    yield
    x2 = x1 + _dot(om_ref[om0:om0 + rows, :], wo_ref[...])
    x2_ref[:, ts, :] = x2.reshape(nb, gt, D_MODEL)
    yield

    h3 = _rms(x2, lnf_ref[...])
    h3_ref[:, ts] = h3.reshape(nb, gt, N_SLAB, HEAD_W)
    yield

    r = _dot(h3.astype(BF16), wr_ref[...]) + br_ref[...]
    lane = lax.broadcasted_iota(jnp.int32, r.shape, 1).astype(F32)
    big = float(4 * HEAD_W)
    g_mask = lane < N_GROUPS
    gl = jnp.where(g_mask, r, NEG_INF)
    g_max = jnp.max(gl, axis=-1, keepdims=True)
    g_idx = jnp.min(jnp.where(gl == g_max, lane, big), axis=-1, keepdims=True)
    g_w = 1.0 / jnp.sum(jnp.where(g_mask, jnp.exp(r - g_max), 0.0), axis=-1, keepdims=True)
    e_lo = N_GROUPS + EXPERTS_PER_GROUP * g_idx
    el = jnp.where((lane >= e_lo) & (lane < e_lo + EXPERTS_PER_GROUP), r, NEG_INF)
    v1 = jnp.max(el, axis=-1, keepdims=True)
    i1 = jnp.min(jnp.where(el == v1, lane, big), axis=-1, keepdims=True)
    el2 = jnp.where(lane == i1, NEG_INF, el)
    v2 = jnp.max(el2, axis=-1, keepdims=True)
    i2 = jnp.min(jnp.where(el2 == v2, lane, big), axis=-1, keepdims=True)
    t = jnp.exp(v2 - v1)
    p1 = 1.0 / (1.0 + t)
    rec = jnp.where(lane == 0.0, i1 - N_GROUPS,
          jnp.where(lane == 1.0, i2 - N_GROUPS,
          jnp.where(lane == 2.0, p1 * g_w,
          jnp.where(lane == 3.0, t * p1 * g_w, 0.0))))
    route_ref[:, ts, :] = rec[:, :ROUTE_W].reshape(nb, gt, ROUTE_W)


def _post_mix(x, oda, ohg, mem_k, mem_v, w_out, ln_mem, w_q, w_o, ln_ffn, w_r, b_r, nb, tm, split):
    B, T, _ = x.shape
    tok = lambda w: pl.BlockSpec((nb, tm, w), lambda b, t: (b, t, 0))
    if mem_k.ndim == 4:
        memspec = pl.BlockSpec((nb, N_MEM, MEM_HEADS, MEM_HEAD_DIM), lambda b, t: (b, 0, 0, 0))
    else:
        memspec = pl.BlockSpec((nb, N_MEM, D_MODEL), lambda b, t: (b, 0, 0))
    fix = lambda s: pl.BlockSpec(s, lambda b, t: (0, 0))
    return pl.pallas_call(
        functools.partial(_post_kernel, nb=nb, tm=tm, split=split),
        grid=(B // nb, T // tm),
        in_specs=[tok(D_MODEL), tok(GROUP_W), tok(GROUP_W), memspec, memspec,
                  fix((D_MODEL, D_MODEL)), fix((1, D_MODEL)), fix((D_MODEL, D_MODEL)),
                  fix((D_MODEL, D_MODEL)), fix((1, D_MODEL)), fix((D_MODEL, HEAD_W)),
                  fix((1, HEAD_W))],
        out_specs=[tok(D_MODEL),
                   pl.BlockSpec((nb, tm, N_SLAB, HEAD_W), lambda b, t: (b, t, 0, 0)),
                   tok(ROUTE_W)],
        out_shape=[jax.ShapeDtypeStruct((B, T, D_MODEL), F32),
                   jax.ShapeDtypeStruct((B, T, N_SLAB, HEAD_W), F32),
                   jax.ShapeDtypeStruct((B, T, ROUTE_W), F32)],
        scratch_shapes=[pltpu.VMEM((nb * tm, D_MODEL), BF16)],
        compiler_params=_cparams(("parallel", "parallel")),
        name="post_mix",
    )(x, oda, ohg, mem_k, mem_v, w_out, ln_mem.reshape(1, D_MODEL), w_q, w_o,
      ln_ffn.reshape(1, D_MODEL), w_r, b_r)


def _experts_kernel(blk_e_ref, n_used_ref, src_next_ref, dst_prev_ref, src0_ref, dst_ref, h_hbm,
                    wg_ref, wu_ref, wd_ref, y_hbm, xbuf, ybuf, gsem, ssem, *, bm, m_tot):
    del blk_e_ref
    i = pl.program_id(0)
    n_used = n_used_ref[0]
    slot = i % 2

    def gather_row(tab_ref, r, s, priority=0):
        pltpu.make_async_copy(h_hbm.at[tab_ref[0, 0, r]], xbuf.at[s, r],
                              gsem.at[s]).start(priority=priority)

    def scatter_row(tab_ref, r, s, priority=0):
        pltpu.make_async_copy(ybuf.at[s, r], y_hbm.at[tab_ref[0, 0, r]],
                              ssem.at[s]).start(priority=priority)

    def wait_block(buf, sem, s):
        pltpu.make_async_copy(buf.at[s], buf.at[s], sem.at[s]).wait()

    def spare_fill(s):
        return pltpu.make_async_copy(ybuf.at[s], y_hbm.at[pl.ds(m_tot + s * bm, bm)], ssem.at[s])

    @pl.when(i == 0)
    def _():
        ybuf[...] = jnp.zeros(ybuf.shape, F32)
        spare_fill(0).start()
        spare_fill(1).start()
        spare_fill(1).wait()

        def body(r, c):
            gather_row(src0_ref, r, 0)
            return c
        lax.fori_loop(0, bm, body, 0)

    @pl.when(i < n_used)
    def _():
        wait_block(xbuf, gsem, slot)
        wait_block(ybuf, ssem, slot)
        x = xbuf[slot].reshape(bm, D_MODEL).astype(BF16)
        hg = _dot(x, wg_ref[0].astype(BF16))
        hu = _dot(x, wu_ref[0].astype(BF16))
        hb = (hg * jax.nn.sigmoid(hg) * hu).astype(BF16)
        ybuf[slot] = _dot(hb, wd_ref[0].astype(BF16)).reshape(bm, N_SLAB, HEAD_W)
        for r in range(bm):
            gather_row(src_next_ref, r, 1 - slot, priority=r % 2)
            scatter_row(dst_prev_ref, r, 1 - slot, priority=(r + 1) % 2)

    @pl.when(i == n_used - 1)
    def _():
        def body(r, c):
            scatter_row(dst_ref, r, slot)
            return c
        lax.fori_loop(0, bm, body, 0)
        wait_block(xbuf, gsem, 1 - slot)
        wait_block(ybuf, ssem, 1 - slot)
        wait_block(ybuf, ssem, slot)


def _experts(h3, blk_e, n_used, src_tab, dst_tab, dst_prev_tab, wg, wu, wd, bm):
    n_blocks = src_tab.shape[0]
    m_tot = TOP_K * h3.shape[0]
    tab = lambda f: pl.BlockSpec((1, 1, bm), f, memory_space=pltpu.SMEM)
    cur = lambda i, e, n: (i, 0, 0)
    nxt = lambda i, e, n: (jnp.minimum(i + 1, n_blocks - 1), 0, 0)
    first = lambda i, e, n: (0, 0, 0)
    wspec = lambda a, b: pl.BlockSpec((1, a, b), lambda i, e, n: (e[i], 0, 0))
    grid_spec = pltpu.PrefetchScalarGridSpec(
        num_scalar_prefetch=2,
        grid=(n_blocks,),
        in_specs=[tab(nxt), tab(cur), tab(first), tab(cur), pl.BlockSpec(memory_space=pl.ANY),
                  wspec(D_MODEL, EXPERT_FF), wspec(D_MODEL, EXPERT_FF), wspec(EXPERT_FF, D_MODEL)],
        out_specs=pl.BlockSpec(memory_space=pl.ANY),
        scratch_shapes=[pltpu.VMEM((2, bm, N_SLAB, HEAD_W), F32), pltpu.VMEM((2, bm, N_SLAB, HEAD_W), F32),
                        pltpu.SemaphoreType.DMA((2,)), pltpu.SemaphoreType.DMA((2,))],
    )
    return pl.pallas_call(
        functools.partial(_experts_kernel, bm=bm, m_tot=m_tot),
        grid_spec=grid_spec,
        out_shape=jax.ShapeDtypeStruct((m_tot + 2 * bm, N_SLAB, HEAD_W), F32),
        compiler_params=_cparams(("arbitrary",)),
        name="experts",
    )(blk_e, n_used, src_tab, dst_prev_tab, src_tab, dst_tab, h3, wg, wu, wd)


def _combine_kernel(x_ref, y0_ref, y1_ref, route_ref, g_ref, o_ref):
    rt = route_ref[...]
    tm = x_ref.shape[0]
    y0 = y0_ref[...].reshape(tm, D_MODEL)
    y1 = y1_ref[...].reshape(tm, D_MODEL)
    y = x_ref[...] + rt[:, 2:3] * y0 + rt[:, 3:4] * y1
    o_ref[...] = _rms(y, g_ref[...])


def _combine(x2, y, route, final_g, tm):
    n = x2.shape[0]
    nt = n // tm
    return pl.pallas_call(
        _combine_kernel,
        grid=(nt,),
        in_specs=[pl.BlockSpec((tm, D_MODEL), lambda i: (i, 0)),
                  pl.BlockSpec((tm, N_SLAB, HEAD_W), lambda i: (i, 0, 0)),
                  pl.BlockSpec((tm, N_SLAB, HEAD_W), lambda i: (i + nt, 0, 0)),
                  pl.BlockSpec((tm, ROUTE_W), lambda i: (i, 0)),
                  pl.BlockSpec((1, D_MODEL), lambda i: (0, 0))],
        out_specs=pl.BlockSpec((tm, D_MODEL), lambda i: (i, 0)),
        out_shape=jax.ShapeDtypeStruct((n, D_MODEL), F32),
        compiler_params=_cparams(("parallel",)),
        name="combine",
    )(x2, y, y, route, final_g.reshape(1, D_MODEL))


def _routing_tables(route, bm):
    n = route.shape[0]
    m_tot = TOP_K * n
    flat_e = jnp.concatenate([route[:, 0], route[:, 1]]).astype(jnp.int32)
    experts = jnp.arange(N_EXPERTS, dtype=jnp.int32)
    counts = jnp.sum(flat_e[:, None] == experts[None, :], axis=0, dtype=jnp.int32)
    n_blk_e = (counts + bm - 1) // bm
    blk_end = jnp.cumsum(n_blk_e)
    n_blocks = m_tot // bm + N_EXPERTS
    blk = jnp.arange(n_blocks, dtype=jnp.int32)
    blk_e = jnp.minimum(jnp.sum(blk[:, None] >= blk_end[None, :], axis=1, dtype=jnp.int32),
                        N_EXPERTS - 1)
    r = jnp.arange(bm, dtype=jnp.int32)[None, :]
    n_pad = (n_blk_e * bm - counts)[:, None]
    pad_keys = jnp.where(r < n_pad, 2 * experts[:, None] + 1, 2 * N_EXPERTS + 1)
    payload_bits = max(m_tot.bit_length(), 1)
    assert (2 * N_EXPERTS + 2) << payload_bits < 2 ** 31
    packed = jnp.concatenate([(2 * flat_e << payload_bits) + jnp.arange(1, m_tot + 1, dtype=jnp.int32),
                              pad_keys.reshape(-1) << payload_bits])
    m = ((lax.sort(packed) & ((1 << payload_bits) - 1)) - 1).reshape(n_blocks, bm)
    real = m >= 0
    spare = m_tot + (blk[:, None] % 2) * bm + r
    src = jnp.where(real, m % n, 0)
    dst = jnp.where(real, m, spare)
    dst_prev = jnp.concatenate([m_tot + bm + r, dst[:-1]], axis=0)
    shape = (n_blocks, 1, bm)
    return (blk_e, blk_end[N_EXPERTS - 1:].astype(jnp.int32), src.reshape(shape), dst.reshape(shape),
            dst_prev.reshape(shape))


def _layer(x, past_k, past_v, s0, mem_k, mem_v, p, cfg):
    B, T, _ = x.shape
    n = B * T
    (q, kf, vf, kb, vt, qh, lf, kk, hi, sg) = _in_proj(
        x, p["ln_mix"], p["w_in"], p["lb"], cfg["nb"], cfg["tm_proj"], cfg["split_proj"])
    if past_k is None:
        oda = _diff_attn(q, kb, vt, p["slopes"], p["lam"], p["da_subln"], cfg["tq"], cfg["tk"])
    else:
        oda = _diff_attn_step(q, kb, vt, past_k, past_v, p["slopes"], p["lam"], p["da_subln"])
    ohg, s_new = _hgrn(qh, kk, lf, hi, sg, p["hg_gnorm"], s0, cfg["tb"])
    x2, h3, route = _post_mix(x, oda, ohg, mem_k, mem_v, p["w_out"], p["ln_mem"], p["w_mem_q"],
                              p["w_mem_o"], p["ln_ffn"], p["w_r"], p["b_r"], cfg["nb"], cfg["tm_post"],
                              cfg["split"])
    route2 = route.reshape(n, ROUTE_W)
    blk_e, n_used, src_tab, dst_tab, dst_prev_tab = _routing_tables(route2, cfg["bm"])
    y = _experts(h3.reshape(n, N_SLAB, HEAD_W), blk_e, n_used, src_tab, dst_tab, dst_prev_tab,
                 p["e_gate"], p["e_up"], p["e_down"], cfg["bm"])
    out = _combine(x2.reshape(n, D_MODEL), y, route2, p["final_g"], cfg["tm_in"])
    return out.reshape(B, T, D_MODEL), kf[None], vf[None], s_new[None]


PROMPT_CFG = dict(tm_in=512, tq=512, tk=256, tb=256, nb=1, tm_proj=512, split_proj=2, tm_post=1024, split=4,
                  bm=512)
SAMPLE_CFG = dict(tm_in=256, tb=64, nb=4, tm_proj=64, split_proj=1, tm_post=64, split=1, bm=128)


def kernel(x_prompt, x_sample, mem_prompt, cache_diff_k, cache_diff_v, state_hgrn, cache_mem_k, cache_mem_v, ln_mix_g, w_in, da_lambda, da_subln_g, hg_lb_logits, hg_gnorm_g, w_out, ln_mem_g, mem_norm_g, w_mem_q, w_mem_k, w_mem_v, w_mem_o, ln_ffn_g, router_group_w, router_group_b, router_expert_w, router_expert_b, exp_w_gate, exp_w_up, exp_w_down, final_norm_g):
    assert w_in.shape[0] == 1, "single-layer configuration"
    lb_all = jnp.cumsum(jax.nn.softmax(hg_lb_logits.astype(F32), axis=0), axis=0)
    lp = da_lambda[0].astype(F32)
    lam = jnp.exp(jnp.sum(lp[0] * lp[1])) - jnp.exp(jnp.sum(lp[2] * lp[3])) + LAM_INIT
    w_r = jnp.zeros((D_MODEL, HEAD_W), F32)
    w_r = w_r.at[:, :N_GROUPS].set(router_group_w[0]).at[:, N_GROUPS:N_GROUPS + N_EXPERTS].set(router_expert_w[0])
    b_r = jnp.zeros((1, HEAD_W), F32)
    b_r = b_r.at[0, :N_GROUPS].set(router_group_b[0]).at[0, N_GROUPS:N_GROUPS + N_EXPERTS].set(router_expert_b[0])
    p = {
        "ln_mix": ln_mix_g[0], "w_in": w_in[0].astype(BF16), "lb": lb_all[0],
        "slopes": jnp.exp2(-8.0 * jnp.arange(1, N_HEADS + 1, dtype=F32) / N_HEADS),
        "lam": lam.reshape(1), "da_subln": da_subln_g[0], "hg_gnorm": hg_gnorm_g[0],
        "w_out": w_out[0].astype(BF16), "ln_mem": ln_mem_g[0], "w_mem_q": w_mem_q[0].astype(BF16),
        "w_mem_o": w_mem_o[0].astype(BF16), "ln_ffn": ln_ffn_g[0], "w_r": w_r.astype(BF16), "b_r": b_r,
        "e_gate": exp_w_gate[0], "e_up": exp_w_up[0],
        "e_down": exp_w_down[0], "final_g": final_norm_g,
    }
    mkf, mvf, mkb, mvb = _memory_kv(mem_prompt, mem_norm_g[0], w_mem_k[0].astype(BF16),
                                    w_mem_v[0].astype(BF16))
    zero_state = jnp.zeros((x_prompt.shape[0], N_HEADS, HEAD_W, HEAD_W), F32)
    yp, kp, vp, sp = _layer(x_prompt, None, None, zero_state, mkb, mvb, p, PROMPT_CFG)
    ys, ks, vs, ss = _layer(
        x_sample, cache_diff_k[0], cache_diff_v[0], state_hgrn[0], cache_mem_k[0], cache_mem_v[0],
        p, SAMPLE_CFG)
    return (yp, ys, kp, vp, sp, mkf[None], mvf[None], ks, vs, ss)
```
